```python
import jax, jax.numpy as jnp
from jax import lax
import numpy as np

D_MODEL = 1024
BATCH = 32
SEQ = 256
DEPTH = 2
DEC_BATCH = 4
DEC_SEQ = 4096
PAST_LEN = 256

GRID_W = 64
NORM_EPS = 1e-6
NEG_BIG = -1e30
LB_EPS = 1e-6
A_HEADS = 4
A_DK = 128
A_DV = 128
A_CHUNK = 64
SHORT_CONV = 5
B_HEADS = 4
B_DK = 128
B_DV = 128
B_CHUNK = 32
C_QHEADS = 8
C_KVHEADS = 2
C_GROUP = C_QHEADS // C_KVHEADS
C_HD = 64
C_WINDOW = 128
C_BLOCK = 128
ROPE_THETA = 10000.0
N_BRANCH = 3
BRANCH_W = 512
D_FF = 2816
FFN_CONV = 3

A_QK_W = A_HEADS * A_DK
A_V_W = A_HEADS * A_DV
B_K_W = B_HEADS * B_DK
B_V_W = B_HEADS * B_DV
C_Q_W = C_QHEADS * C_HD
C_KV_W = C_KVHEADS * C_HD
SPLIT_SIZES = (A_QK_W, A_QK_W, A_V_W, A_V_W, 2 * A_HEADS, 2 * A_HEADS,
               B_K_W, B_V_W, 2 * B_K_W, B_V_W,
               C_Q_W, C_KV_W, C_KV_W,
               N_BRANCH * D_MODEL)
SPLIT_POINTS = tuple(np.cumsum(SPLIT_SIZES)[:-1].tolist())
D_IN = int(sum(SPLIT_SIZES))

kernel_name = 'hybrid_diffusion_trunk_step'


def _rms(x, w):
    xf = x.astype(jnp.float32)
    y = xf * lax.rsqrt(jnp.mean(jnp.square(xf), axis=-1, keepdims=True) + NORM_EPS)
    return (y * w.astype(jnp.float32)).astype(x.dtype)


def _l2(x):
    xf = x.astype(jnp.float32)
    return xf * lax.rsqrt(jnp.sum(jnp.square(xf), axis=-1, keepdims=True) + NORM_EPS)


def _dwconv(x, w):
    width = w.shape[0]
    return lax.conv_general_dilated(x, w[:, None, :].astype(x.dtype), window_strides=(1,),
                                    padding=[(width // 2, width // 2)],
                                    dimension_numbers=('NWC', 'WIO', 'NWC'),
                                    feature_group_count=x.shape[-1])


def _flip(a):
    return jnp.flip(a, axis=1)


def _to_chunks(a, chunk):
    b, t = a.shape[:2]
    a = a.reshape(b, t // chunk, chunk, *a.shape[2:])
    return jnp.swapaxes(jnp.moveaxis(a, 1, 0), 2, 3)


def _from_chunks(o):
    n, b, h, c, d = o.shape
    return jnp.moveaxis(jnp.swapaxes(o, 2, 3), 0, 1).reshape(b, n * c, h, d)


def _masked_exp(mask, diff):
    return jnp.where(mask, jnp.exp(jnp.where(mask, diff, 0.0)), 0.0)


def _gated_delta(q, k, v, beta, g, s0):
    c = A_CHUNK
    q, k, v = _to_chunks(q, c), _to_chunks(k, c), _to_chunks(v, c)
    beta, g = _to_chunks(beta, c), _to_chunks(g, c)
    gc = jnp.cumsum(g, axis=-1)
    idx = jnp.arange(c)
    incl = idx[:, None] >= idx[None, :]
    strict = idx[:, None] > idx[None, :]
    decay = _masked_exp(incl, gc[..., :, None] - gc[..., None, :])
    kb = k * beta[..., None]
    m = jnp.where(strict, jnp.einsum('nbhid,nbhjd->nbhij', kb, k) * decay, 0.0)
    eye = jnp.eye(c, dtype=m.dtype)
    t_inv = lax.linalg.triangular_solve(eye + m, jnp.broadcast_to(eye, m.shape),
                                        left_side=True, lower=True, unit_diagonal=True)
    u = t_inv @ (v * beta[..., None])
    w = t_inv @ (kb * jnp.exp(gc)[..., None])
    qk = jnp.einsum('nbhid,nbhjd->nbhij', q, k) * decay
    qg = q * jnp.exp(gc)[..., None]
    g_last = gc[..., -1]
    kd = k * jnp.exp(g_last[..., None] - gc)[..., None]

    def step(s, xs):
        qg_n, qk_n, u_n, w_n, kd_n, gl_n = xs
        v_new = u_n - jnp.einsum('bhck,bhkv->bhcv', w_n, s)
        o = jnp.einsum('bhck,bhkv->bhcv', qg_n, s) + jnp.einsum('bhcs,bhsv->bhcv', qk_n, v_new)
        s = s * jnp.exp(gl_n)[..., None, None] + jnp.einsum('bhck,bhcv->bhkv', kd_n, v_new)
        return s, o

    s, o = lax.scan(step, s0, (qg, qk, u, w, kd, g_last))
    return _from_chunks(o), s


def _hgrn2(q, k, v, lf, s0):
    c = B_CHUNK
    q, k, v, lf = _to_chunks(q, c), _to_chunks(k, c), _to_chunks(v, c), _to_chunks(lf, c)
    idx = jnp.arange(c)
    incl = (idx[:, None] >= idx[None, :])[:, :, None]

    def step(s, xs):
        q_n, k_n, v_n, lf_n = xs
        b = jnp.cumsum(lf_n, axis=2)
        rel = _masked_exp(incl, b[:, :, :, None, :] - b[:, :, None, :, :])
        att = jnp.einsum('bhtk,bhsk,bhtsk->bhts', q_n, k_n, rel)
        o = jnp.einsum('bhtk,bhkv->bhtv', q_n * jnp.exp(b), s) + jnp.einsum('bhts,bhsv->bhtv', att, v_n)
        b_last = b[:, :, -1]
        s = s * jnp.exp(b_last)[..., None] + jnp.einsum('bhsk,bhsv->bhkv', k_n * jnp.exp(b_last[:, :, None] - b), v_n)
        return s, o

    s, o = lax.scan(step, s0, (q, k, v, lf))
    return _from_chunks(o), s


def _axial_rope(x):
    t, dh = x.shape[1], x.shape[-1]
    rows = t // GRID_W
    row = jnp.repeat(jnp.arange(rows, dtype=jnp.float32), GRID_W)
    col = jnp.tile(jnp.arange(GRID_W, dtype=jnp.float32), rows)
    nf = dh // 4
    inv = ROPE_THETA ** (-jnp.arange(nf, dtype=jnp.float32) / nf)
    xf = x.astype(jnp.float32)

    def rot(xh, pos):
        ang = pos[:, None] * inv
        cos, sin = jnp.cos(ang)[None, :, None, :], jnp.sin(ang)[None, :, None, :]
        x1, x2 = xh[..., :nf], xh[..., nf:]
        return jnp.concatenate([x1 * cos - x2 * sin, x2 * cos + x1 * sin], axis=-1)

    out = jnp.concatenate([rot(xf[..., :dh // 2], row), rot(xf[..., dh // 2:], col)], axis=-1)
    return out.astype(x.dtype)


def _sink_probs(s, sink):
    sk = sink.astype(jnp.float32)[:, :, None, None]
    m = jnp.maximum(jnp.max(s, axis=-1, keepdims=True), sk)
    p = jnp.exp(s - m)
    return p / (jnp.sum(p, axis=-1, keepdims=True) + jnp.exp(sk - m))


def _ctx_attention(q, k, v, sink):
    b, t = q.shape[:2]
    qg = q.reshape(b, t, C_KVHEADS, C_GROUP, C_HD)
    s = jnp.einsum('bqhgd,bkhd->bhgqk', qg, k, preferred_element_type=jnp.float32) * (C_HD ** -0.5)
    p = _sink_probs(s, sink).astype(v.dtype)
    o = jnp.einsum('bhgqk,bkhd->bqhgd', p, v)
    return o.reshape(b, t, C_Q_W)


def _latent_attention(q, k, v, k_ctx, v_ctx, sink):
    b, t = q.shape[:2]
    nb = t // C_BLOCK
    qb = q.reshape(b, nb, C_BLOCK, C_KVHEADS, C_GROUP, C_HD)

    def neigh(a):
        a = a.reshape(b, nb, C_BLOCK, C_KVHEADS, C_HD)
        a = jnp.pad(a, ((0, 0), (1, 1), (0, 0), (0, 0), (0, 0)))
        return jnp.concatenate([a[:, :-2], a[:, 1:-1], a[:, 2:]], axis=2)

    kn, vn = neigh(k), neigh(v)
    scale = C_HD ** -0.5
    s_loc = jnp.einsum('bnqhgd,bnkhd->bnhgqk', qb, kn, preferred_element_type=jnp.float32) * scale
    blk = jnp.arange(nb)[:, None] * C_BLOCK
    qpos = blk + jnp.arange(C_BLOCK)[None, :]
    kpos = blk - C_BLOCK + jnp.arange(3 * C_BLOCK)[None, :]
    mask = ((jnp.abs(qpos[:, :, None] - kpos[:, None, :]) <= C_WINDOW)
            & (kpos >= 0)[:, None, :] & (kpos < t)[:, None, :])
    s_loc = jnp.where(mask[None, :, None, None], s_loc, NEG_BIG)
    s_ctx = jnp.einsum('bnqhgd,bphd->bnhgqp', qb, k_ctx.astype(q.dtype), preferred_element_type=jnp.float32) * scale
    p = _sink_probs(jnp.concatenate([s_loc, s_ctx], axis=-1), sink).astype(v.dtype)
    nk = 3 * C_BLOCK
    o = (jnp.einsum('bnhgqk,bnkhd->bnqhgd', p[..., :nk], vn)
         + jnp.einsum('bnhgqp,bphd->bnqhgd', p[..., nk:], v_ctx.astype(v.dtype)))
    return o.reshape(b, t, C_Q_W)


def _layer(x, cond, l, P, past):
    f32 = jnp.float32
    bx, t, _ = x.shape
    mod = jax.nn.silu(cond) @ P['ada_w'][l] + P['ada_b'][l]
    sh1, sc1, g1, sh2, sc2, g2 = jnp.split(mod[:, None, :], 6, axis=-1)
    h = _rms(x, P['norm1_w'][l]) * (1 + sc1) + sh1
    proj = h @ P['w_in'][l]
    (qa, ka, va, ga, beta_raw, alpha_raw, qb, ib, fb, gb,
     q_c, k_c, v_c, mg) = jnp.split(proj, SPLIT_POINTS, axis=-1)

    if past is None:
        sa0 = jnp.zeros((bx, 2, A_HEADS, A_DK, A_DV), f32)
        sb0 = jnp.zeros((bx, 2, B_HEADS, B_DK, B_DV), f32)
    else:
        sa0 = past[0].astype(f32)
        sb0 = past[1].astype(f32)

    qkv = jax.nn.silu(_dwconv(jnp.concatenate([qa, ka, va], axis=-1), P['conv_a'][l]))
    qa, ka, va = jnp.split(qkv, [A_QK_W, 2 * A_QK_W], axis=-1)
    q = _l2(qa.reshape(bx, t, A_HEADS, A_DK)) * (A_DK ** -0.5)
    k = _l2(ka.reshape(bx, t, A_HEADS, A_DK))
    v = va.reshape(bx, t, A_HEADS, A_DV).astype(f32)
    beta = jax.nn.sigmoid(beta_raw.astype(f32)).reshape(bx, t, 2, A_HEADS)
    g = -jnp.exp(P['a_log'][l].astype(f32)) * jax.nn.softplus(
        alpha_raw.astype(f32).reshape(bx, t, 2, A_HEADS) + P['dt_bias'][l].astype(f32))
    o_f, sa_f = _gated_delta(q, k, v, beta[:, :, 0], g[:, :, 0], sa0[:, 0])
    o_b, sa_b = _gated_delta(_flip(q), _flip(k), _flip(v), _flip(beta[:, :, 1]), _flip(g[:, :, 1]), sa0[:, 1])
    o_a = _rms(o_f + _flip(o_b), P['norm_a'][l]) * jax.nn.silu(ga.astype(f32).reshape(bx, t, A_HEADS, A_DV))
    o_a = o_a.reshape(bx, t, BRANCH_W).astype(x.dtype)
    state_a = jnp.stack([sa_f, sa_b], axis=1)

    z = fb.astype(f32).reshape(bx, t, 2, B_K_W)
    if l == 0:
        lf = jax.nn.log_sigmoid(z)
    else:
        lb_p = jax.nn.softmax(P['lb_logits'].astype(f32), axis=1)
        lb = jnp.clip(jnp.sum(lb_p[:, 1:l + 1], axis=1), LB_EPS, 1.0 - LB_EPS)
        lf = jnp.logaddexp(jnp.log(lb), jnp.log1p(-lb) + jax.nn.log_sigmoid(z))
    lf = lf.reshape(bx, t, 2, B_HEADS, B_DK)
    kf = -jnp.expm1(lf)
    qh = jax.nn.silu(qb.astype(f32)).reshape(bx, t, B_HEADS, B_DK)
    ih = ib.astype(f32).reshape(bx, t, B_HEADS, B_DV)
    o_f, sb_f = _hgrn2(qh, kf[:, :, 0], ih, lf[:, :, 0], sb0[:, 0])
    o_b, sb_b = _hgrn2(_flip(qh), _flip(kf[:, :, 1]), _flip(ih), _flip(lf[:, :, 1]), sb0[:, 1])
    o_b2 = _rms(o_f + _flip(o_b), P['norm_b'][l]) * jax.nn.silu(gb.astype(f32).reshape(bx, t, B_HEADS, B_DV))
    o_b2 = o_b2.reshape(bx, t, BRANCH_W).astype(x.dtype)
    state_b = jnp.stack([sb_f, sb_b], axis=1)

    qh_c = _rms(q_c.reshape(bx, t, C_QHEADS, C_HD), P['q_norm'][l])
    kh_c = _rms(k_c.reshape(bx, t, C_KVHEADS, C_HD), P['k_norm'][l])
    vh_c = v_c.reshape(bx, t, C_KVHEADS, C_HD)
    sink = P['sink'][l].reshape(C_KVHEADS, C_GROUP)
    if past is None:
        o_c = _ctx_attention(qh_c, kh_c, vh_c, sink)
    else:
        o_c = _latent_attention(_axial_rope(qh_c), _axial_rope(kh_c), vh_c, past[2], past[3], sink)

    gates = jax.nn.sigmoid(mg.astype(f32).reshape(bx, t, N_BRANCH, D_MODEL))
    br = jnp.einsum('btrw,rwd->btrd', jnp.stack([o_a, o_b2, o_c], axis=2), P['w_branch'][l])
    merged = jnp.sum(gates * br.astype(f32), axis=2).astype(x.dtype)
    x = x + g1 * (merged @ P['w_out'][l])

    h = _rms(x, P['norm2_w'][l]) * (1 + sc2) + sh2
    u = _dwconv(h @ P['w_up'][l], P['conv_ffn'][l])
    a, u = jnp.split(u, 2, axis=-1)
    x = x + g2 * ((jax.nn.silu(a) * u) @ P['w_down'][l])
    return x, (state_a, state_b, kh_c, vh_c)


def setup_inputs(seed: int = 0) -> dict:
    key = jax.random.key(seed)
    ks = jax.random.split(key, 32)
    f32 = jnp.float32

    def nrm(k, shape, s):
        return jax.random.normal(k, shape, f32) * s

    dt = jnp.exp(jax.random.uniform(ks[13], (DEPTH, 2, A_HEADS), f32, float(np.log(1e-3)), float(np.log(1e-1))))
    return {
        'x_prompt': nrm(ks[0], (BATCH, SEQ, D_MODEL), 1.0),
        'x_sample': nrm(ks[1], (DEC_BATCH, DEC_SEQ, D_MODEL), 1.0),
        'state_delta': nrm(ks[2], (DEC_BATCH, DEPTH, 2, A_HEADS, A_DK, A_DV), 0.1),
        'state_hgrn': nrm(ks[3], (DEC_BATCH, DEPTH, 2, B_HEADS, B_DK, B_DV), 0.5),
        'cache_k': nrm(ks[4], (DEC_BATCH, DEPTH, PAST_LEN, C_KVHEADS, C_HD), 1.0),
        'cache_v': nrm(ks[5], (DEC_BATCH, DEPTH, PAST_LEN, C_KVHEADS, C_HD), 1.0),
        'c': nrm(ks[6], (DEC_BATCH, D_MODEL), 1.0),
        'c_ctx': nrm(ks[7], (D_MODEL,), 1.0),
        'ada_w': nrm(ks[8], (DEPTH, D_MODEL, 6 * D_MODEL), 0.5 * D_MODEL ** -0.5),
        'ada_b': nrm(ks[9], (DEPTH, 6 * D_MODEL), 0.01),
        'norm1_w': 1.0 + nrm(ks[10], (DEPTH, D_MODEL), 0.02),
        'w_in': nrm(ks[11], (DEPTH, D_MODEL, D_IN), D_MODEL ** -0.5),
        'conv_a': nrm(ks[12], (DEPTH, SHORT_CONV, 2 * A_QK_W + A_V_W), SHORT_CONV ** -0.5),
        'a_log': jnp.log(jax.random.uniform(ks[14], (DEPTH, 2, A_HEADS), f32, 1.0, 16.0)),
        'dt_bias': jnp.log(jnp.expm1(dt)),
        'norm_a': 1.0 + nrm(ks[15], (DEPTH, A_DV), 0.02),
        'lb_logits': nrm(ks[16], (2, DEPTH, B_K_W), 1.0),
        'norm_b': 1.0 + nrm(ks[17], (DEPTH, B_DV), 0.02),
        'q_norm': 1.0 + nrm(ks[18], (DEPTH, C_HD), 0.02),
        'k_norm': 1.0 + nrm(ks[19], (DEPTH, C_HD), 0.02),
        'sink': nrm(ks[20], (DEPTH, C_QHEADS), 0.5),
        'w_branch': nrm(ks[21], (DEPTH, N_BRANCH, BRANCH_W, D_MODEL), BRANCH_W ** -0.5),
        'w_out': nrm(ks[22], (DEPTH, D_MODEL, D_MODEL), D_MODEL ** -0.5),
        'norm2_w': 1.0 + nrm(ks[23], (DEPTH, D_MODEL), 0.02),
        'w_up': nrm(ks[24], (DEPTH, D_MODEL, 2 * D_FF), D_MODEL ** -0.5),
        'conv_ffn': nrm(ks[25], (DEPTH, FFN_CONV, 2 * D_FF), FFN_CONV ** -0.5),
        'w_down': nrm(ks[26], (DEPTH, D_FF, D_MODEL), D_FF ** -0.5),
    }


def reference(x_prompt, x_sample, state_delta, state_hgrn, cache_k, cache_v, c, c_ctx,
              ada_w, ada_b, norm1_w, w_in, conv_a, a_log, dt_bias, norm_a, lb_logits, norm_b,
              q_norm, k_norm, sink, w_branch, w_out, norm2_w, w_up, conv_ffn, w_down):
    P = dict(ada_w=ada_w, ada_b=ada_b, norm1_w=norm1_w, w_in=w_in, conv_a=conv_a, a_log=a_log,
             dt_bias=dt_bias, norm_a=norm_a, lb_logits=lb_logits, norm_b=norm_b, q_norm=q_norm,
             k_norm=k_norm, sink=sink, w_branch=w_branch, w_out=w_out, norm2_w=norm2_w,
             w_up=w_up, conv_ffn=conv_ffn, w_down=w_down)

    y_prompt = x_prompt
    cond_ctx = c_ctx[None, :]
    st_a, st_b, ck, cv = [], [], [], []
    for l in range(DEPTH):
        y_prompt, (s_a, s_b, k_l, v_l) = _layer(y_prompt, cond_ctx, l, P, None)
        st_a.append(s_a)
        st_b.append(s_b)
        ck.append(k_l)
        cv.append(v_l)

    y_sample = x_sample
    for l in range(DEPTH):
        past = (state_delta[:, l], state_hgrn[:, l], cache_k[:, l], cache_v[:, l])
        y_sample, _ = _layer(y_sample, c, l, P, past)

    new_state_delta = jnp.stack(st_a, axis=1)
    new_state_hgrn = jnp.stack(st_b, axis=1)
    new_cache_k = jnp.stack(ck, axis=1)
    new_cache_v = jnp.stack(cv, axis=1)
    return (y_prompt, y_sample, new_state_delta, new_state_hgrn, new_cache_k, new_cache_v)
```

```python
import functools

import numpy as np
import jax
import jax.numpy as jnp
from jax import lax
from jax.experimental import pallas as pl
from jax.experimental.pallas import tpu as pltpu

F32 = jnp.float32
BF16 = jnp.bfloat16

D_MODEL = 1024
NORM_EPS = 1e-6
LB_EPS = 1e-6
NEG_BIG = -1e30
GRID_W = 64
ROPE_THETA = 10000.0

HEADS = 4
HEAD_W = 128
SHORT_CONV = 5
C_QHEADS = 8
C_KVHEADS = 2
C_GROUP = C_QHEADS // C_KVHEADS
C_HD = 64
C_WINDOW = 128
C_BLOCK = 128
BRANCH_W = 512
D_FF = 2816

LANES = 128
SUBLANES = 8
BF16_ROWS = 16
VMEM_LIMIT = 56 * 1024 * 1024

COL_MG = 0
COL_QA = 3072
COL_KA = 3584
COL_VA = 4096
COL_GA = 4608
COL_QB = 5120
COL_IB = 5632
COL_FB = 6144
COL_GB = 7168
COL_QC = 7680
COL_KC = 8192
COL_VC = 8320
PROJ_W = 8448
_SRC = dict(qa=0, ka=512, va=1024, ga=1536, beta=2048, alpha=2056, qb=2064, ib=2576, fb=3088, gb=4112,
            qc=4624, kc=5136, vc=5264, mg=5392, end=8464)

CHUNK_A = 256
CHUNK_B = 64
TRI_BASE = 16


def _cparams(sem):
    return pltpu.CompilerParams(dimension_semantics=sem, vmem_limit_bytes=VMEM_LIMIT)


def _dot(a, b):
    return jnp.dot(a.astype(BF16), b.astype(BF16), preferred_element_type=F32)


def _dot_nt(a, b):
    return lax.dot_general(a.astype(BF16), b.astype(BF16), (((1,), (1,)), ((), ())), preferred_element_type=F32)


def _dot_tn(a, b):
    return lax.dot_general(a.astype(BF16), b.astype(BF16), (((0,), (0,)), ((), ())), preferred_element_type=F32)


def _sigmoid(x):
    return 1.0 / (1.0 + jnp.exp(-x))


def _silu(x):
    return x * _sigmoid(x)


def _softplus(x):
    return jnp.maximum(x, 0.0) + jnp.log1p(jnp.exp(-jnp.abs(x)))


def _log_sigmoid(x):
    return jnp.minimum(x, 0.0) - jnp.log1p(jnp.exp(-jnp.abs(x)))


def _rms_rows(x, w):
    ms = jnp.mean(x * x, axis=-1, keepdims=True)
    return x * lax.rsqrt(ms + NORM_EPS) * w


def _split_bf16(x, n):
    parts, r = [], x
    for _ in range(n):
        p = r.astype(BF16)
        parts.append(p)
        r = r - p.astype(F32)
    return parts


def _dot_sel(sel_bf16, x):
    out = None
    for p in _split_bf16(x, 3):
        t = jnp.dot(sel_bf16, p, preferred_element_type=F32)
        out = t if out is None else out + t
    return out


def _mod_kernel(c_ref, w_ref, b_ref, o_ref):
    c = c_ref[...]
    o_ref[0] = _dot(_silu(c), w_ref[0]) + b_ref[0]


def _mod_call(cond, ada_w, ada_b):
    depth, d, n = ada_w.shape
    rows = cond.shape[0]
    tn = 768
    return pl.pallas_call(
        _mod_kernel,
        out_shape=jax.ShapeDtypeStruct((depth, rows, n), F32),
        grid=(depth, n // tn),
        in_specs=[pl.BlockSpec((rows, d), lambda l, j: (0, 0)),
                  pl.BlockSpec((1, d, tn), lambda l, j: (l, 0, j)),
                  pl.BlockSpec((1, 1, tn), lambda l, j: (l, 0, j))],
        out_specs=pl.BlockSpec((1, rows, tn), lambda l, j: (l, 0, j)),
        compiler_params=_cparams(("arbitrary", "arbitrary")),
        name="mod",
    )(cond, ada_w, ada_b.reshape(depth, 1, n))


def _in_proj_kernel(x_ref, mod_ref, nw_ref, w_ref, wg_ref, o_ref, og_ref, h_scr):
    @pl.when(pl.program_id(1) == 0)
    def _():
        h = _rms_rows(x_ref[...], nw_ref[...]) * (1.0 + mod_ref[0, 1:2, :]) + mod_ref[0, 0:1, :]
        hb = h.astype(BF16)
        h_scr[...] = hb
        og_ref[...] = jnp.dot(hb, wg_ref[...], preferred_element_type=F32)

    o_ref[...] = jnp.dot(h_scr[...], w_ref[...], preferred_element_type=F32)


def _in_proj_call(x2d, mod, nw, w_p, w_g, tm, tiles_per_cond):
    m, d = x2d.shape
    tn = 768
    return pl.pallas_call(
        _in_proj_kernel,
        out_shape=(jax.ShapeDtypeStruct((m, PROJ_W), F32), jax.ShapeDtypeStruct((m, LANES), F32)),
        grid=(m // tm, PROJ_W // tn),
        in_specs=[pl.BlockSpec((tm, d), lambda i, j: (i, 0)),
                  pl.BlockSpec((1, 6, d), lambda i, j: (i // tiles_per_cond, 0, 0)),
                  pl.BlockSpec((1, d), lambda i, j: (0, 0)),
                  pl.BlockSpec((d, tn), lambda i, j: (0, j)),
                  pl.BlockSpec((d, LANES), lambda i, j: (0, 0))],
        out_specs=(pl.BlockSpec((tm, tn), lambda i, j: (i, j)),
                   pl.BlockSpec((tm, LANES), lambda i, j: (i, 0))),
        scratch_shapes=[pltpu.VMEM((tm, d), BF16)],
        compiler_params=_cparams(("arbitrary", "arbitrary")),
        name="in_proj",
    )(x2d, mod, nw, w_p, w_g)


ROW_BLOCK = 256


def _gated_norm_epilogue(of_scr, ob_scr, gate_ref, nw_ref, o_ref, t):
    nw = nw_ref[...]

    def body(r, carry):
        r0 = pl.multiple_of(r * ROW_BLOCK, ROW_BLOCK)
        o = of_scr[pl.ds(r0, ROW_BLOCK), :] + ob_scr[pl.ds(r0, ROW_BLOCK), :]
        y = _rms_rows(o, nw) * _silu(gate_ref[pl.ds(r0, ROW_BLOCK), :])
        o_ref[pl.ds(r0, ROW_BLOCK), :] = y.astype(o_ref.dtype)
        return carry

    lax.fori_loop(0, t // ROW_BLOCK, body, 0)


def _tri_inverse(mat, ri, ci, c):
    def same_block(n):
        s = int(np.log2(n))
        return (ri >> s) == (ci >> s)

    mb = jnp.where(same_block(TRI_BASE), mat, 0.0)
    y = -mb
    p = _dot(mb, mb)
    steps = int(np.log2(TRI_BASE)) - 1
    for s in range(steps):
        y = y + p + _dot(y, p)
        if s + 1 < steps:
            p = _dot(p, p)
    n = TRI_BASE
    while n < c:
        cn = jnp.where(jnp.logical_and(same_block(2 * n), jnp.logical_not(same_block(n))), mat, 0.0)
        z = cn + _dot(cn, y)
        y = y - (z + _dot(y, z))
        n *= 2
    return jnp.where(ri == ci, 1.0, 0.0) + y


def _delta_chunk(q, k, v, beta_row, alpha_row, a_neg, dt_b, s, reverse, c):
    ri = lax.broadcasted_iota(jnp.int32, (c, c), 0)
    ci = lax.broadcasted_iota(jnp.int32, (c, c), 1)
    if reverse:
        incl, strict = ri <= ci, ri < ci
    else:
        incl, strict = ri >= ci, ri > ci
    eye = ri == ci
    beta_r = _sigmoid(beta_row)
    g_r = a_neg * _softplus(alpha_row + dt_b)
    gc_col = jnp.sum(jnp.where(incl, jnp.broadcast_to(g_r, (c, c)), 0.0), axis=1, keepdims=True)
    beta_col = jnp.sum(jnp.where(eye, jnp.broadcast_to(beta_r, (c, c)), 0.0), axis=1, keepdims=True)
    gc_row = jnp.sum(jnp.where(eye, jnp.broadcast_to(gc_col, (c, c)), 0.0), axis=0, keepdims=True)
    g_tot = jnp.sum(g_r, axis=1, keepdims=True)
    decay = jnp.where(incl, jnp.exp(jnp.where(incl, gc_col - gc_row, 0.0)), 0.0)

    kb = k.astype(BF16)
    gram = _dot_nt(kb, kb)
    t_inv = _tri_inverse(jnp.where(strict, gram * beta_col * decay, 0.0), ri, ci, c)
    e_gc = jnp.exp(gc_col)
    rhs = jnp.concatenate([v * beta_col, k * (beta_col * e_gc)], axis=1)
    uw = _dot(t_inv, rhs)
    u, w = uw[:, :HEAD_W], uw[:, HEAD_W:]
    qk = _dot_nt(q, kb) * decay
    wq = _dot(jnp.concatenate([w, q * e_gc], axis=0), s)
    v_new = u - wq[:c]
    o = wq[c:] + _dot(qk, v_new)
    kd = k * jnp.exp(g_tot - gc_col)
    s_new = s * jnp.exp(g_tot) + _dot_tn(kd, v_new)
    return o, s_new


def _mixer_a_kernel(*refs, t, c, has_past, emit_state):
    refs = list(refs)
    q_ref, k_ref, v_ref, ga_ref, gr_ref, cwq_ref, cwk_ref, cwv_ref, alog_ref, dtb_ref, nw_ref = refs[:11]
    pos = 11
    s0_ref = None
    if has_past:
        s0_ref = refs[pos]
        pos += 1
    o_ref = refs[pos]
    pos += 1
    sfin_ref = None
    if emit_state:
        sfin_ref = refs[pos]
        pos += 1
    xp_scr, qn_scr, kn_scr, vn_scr, of_scr, ob_scr, s_scr = refs[pos:]

    h = pl.program_id(1)
    nrb = t // ROW_BLOCK
    pad = SUBLANES
    half = SHORT_CONV // 2

    xp_scr[0:pad, :] = jnp.zeros((pad, HEAD_W), F32)
    xp_scr[t + pad:t + 2 * pad, :] = jnp.zeros((pad, HEAD_W), F32)

    def conv_pass(x_ref, cw_ref, dst_scr, l2, scale):
        def cp(r, carry):
            r0 = pl.multiple_of(r * ROW_BLOCK, ROW_BLOCK)
            xp_scr[pl.ds(r0 + pad, ROW_BLOCK), :] = x_ref[pl.ds(r0, ROW_BLOCK), :]
            return carry

        lax.fori_loop(0, nrb, cp, 0)
        cw = cw_ref[...]

        def body(r, carry):
            r0 = pl.multiple_of(r * ROW_BLOCK, ROW_BLOCK)
            win = xp_scr[pl.ds(r0, ROW_BLOCK + 2 * pad), :]
            y = None
            for j in range(SHORT_CONV):
                lo = pad - half + j
                term = win[lo:lo + ROW_BLOCK, :] * cw[j:j + 1, :]
                y = term if y is None else y + term
            y = _silu(y)
            if l2:
                y = y * (lax.rsqrt(jnp.sum(y * y, axis=-1, keepdims=True) + NORM_EPS) * scale)
            dst_scr[pl.ds(r0, ROW_BLOCK), :] = y
            return carry

        lax.fori_loop(0, nrb, body, 0)

    conv_pass(q_ref, cwq_ref, qn_scr, True, HEAD_W ** -0.5)
    conv_pass(k_ref, cwk_ref, kn_scr, True, 1.0)
    conv_pass(v_ref, cwv_ref, vn_scr, False, 1.0)

    if has_past:
        s_scr[0] = s0_ref[0, 0, 0, 0]
        s_scr[1] = s0_ref[0, 0, 1, 0]
    else:
        s_scr[...] = jnp.zeros(s_scr.shape, F32)

    ones = jnp.ones((1, c), F32)
    a_neg = [-jnp.exp(ones * alog_ref[d, h]) for d in range(2)]
    dt_b = [dtb_ref[d, h] for d in range(2)]
    nc = t // c

    def chunk_pair(n, carry):
        for d in range(2):
            cidx = n if d == 0 else nc - 1 - n
            r0 = pl.multiple_of(cidx * c, c)
            gr = gr_ref[0, 0, cidx]
            o, s_new = _delta_chunk(qn_scr[pl.ds(r0, c), :], kn_scr[pl.ds(r0, c), :], vn_scr[pl.ds(r0, c), :],
                                    gr[d:d + 1, :], gr[2 + d:3 + d, :], a_neg[d], dt_b[d], s_scr[d], d == 1, c)
            s_scr[d] = s_new
            dst = of_scr if d == 0 else ob_scr
            dst[pl.ds(r0, c), :] = o
        return carry

    lax.fori_loop(0, nc, chunk_pair, 0)

    _gated_norm_epilogue(of_scr, ob_scr, ga_ref, nw_ref, o_ref, t)
    if emit_state:
        sfin_ref[0, 0, 0] = s_scr[0]
        sfin_ref[0, 1, 0] = s_scr[1]


def _mixer_a_call(proj, gates_r, conv_w, a_log, dt_bias, norm_w, past, layer, nseq, t, emit_state):
    c = min(CHUNK_A, t)
    nc = t // c
    cb = lambda off: (lambda b, h: (b, off // HEAD_W + h))
    in_specs = [pl.BlockSpec((t, HEAD_W), cb(COL_QA)), pl.BlockSpec((t, HEAD_W), cb(COL_KA)),
                pl.BlockSpec((t, HEAD_W), cb(COL_VA)), pl.BlockSpec((t, HEAD_W), cb(COL_GA)),
                pl.BlockSpec((1, 1, nc, 4, c), lambda b, h: (b, h, 0, 0, 0)),
                pl.BlockSpec((SHORT_CONV, HEAD_W), lambda b, h: (0, h)),
                pl.BlockSpec((SHORT_CONV, HEAD_W), lambda b, h: (0, HEADS + h)),
                pl.BlockSpec((SHORT_CONV, HEAD_W), lambda b, h: (0, 2 * HEADS + h)),
                pl.BlockSpec(memory_space=pltpu.SMEM), pl.BlockSpec(memory_space=pltpu.SMEM),
                pl.BlockSpec((1, HEAD_W), lambda b, h: (0, 0))]
    args = [proj, proj, proj, proj, gates_r, conv_w, conv_w, conv_w, a_log, dt_bias, norm_w]
    if past is not None:
        in_specs.append(pl.BlockSpec((1, 1, 2, 1, HEAD_W, HEAD_W), lambda b, h: (b, layer, 0, h, 0, 0)))
        args.append(past)
    out_shape = [jax.ShapeDtypeStruct((nseq * t, HEADS * HEAD_W), BF16)]
    out_specs = [pl.BlockSpec((t, HEAD_W), lambda b, h: (b, h))]
    if emit_state:
        out_shape.append(jax.ShapeDtypeStruct((nseq, 2, HEADS, HEAD_W, HEAD_W), F32))
        out_specs.append(pl.BlockSpec((1, 2, 1, HEAD_W, HEAD_W), lambda b, h: (b, 0, h, 0, 0)))
    seq = pltpu.VMEM((t, HEAD_W), F32)
    return pl.pallas_call(
        functools.partial(_mixer_a_kernel, t=t, c=c, has_past=past is not None, emit_state=emit_state),
        out_shape=tuple(out_shape), grid=(nseq, HEADS), in_specs=in_specs, out_specs=tuple(out_specs),
        scratch_shapes=[pltpu.VMEM((t + 2 * SUBLANES, HEAD_W), F32), seq, seq, seq, seq, seq,
                        pltpu.VMEM((2, HEAD_W, HEAD_W), F32)],
        compiler_params=_cparams(("arbitrary", "arbitrary")),
        name="mixer_a",
    )(*args)


def _hgrn_diag(q, kf, v, b, reverse, c):
    nb = c // SUBLANES
    q3, k3, v3, b3 = (a.reshape(nb, SUBLANES, HEAD_W) for a in (q, kf, v, b))
    sub = lax.broadcasted_iota(jnp.int32, (nb, SUBLANES, HEAD_W), 1)
    o3 = jnp.zeros((nb, SUBLANES, HEAD_W), F32)
    for j in range(SUBLANES):
        mask = (sub <= j) if reverse else (sub >= j)
        e = jnp.exp(jnp.where(mask, b3 - b3[:, j:j + 1, :], 0.0))
        a = jnp.sum(jnp.where(mask, q3 * e * k3[:, j:j + 1, :], 0.0), axis=-1, keepdims=True)
        o3 = o3 + a * v3[:, j:j + 1, :]
    return o3.reshape(c, HEAD_W)


def _hgrn_chunk(q, kf, v, lf, st, reverse, c):
    ri = lax.broadcasted_iota(jnp.int32, (c, c), 0)
    ci = lax.broadcasted_iota(jnp.int32, (c, c), 1)
    incl = (ri <= ci) if reverse else (ri >= ci)
    b = _dot_sel(jnp.where(incl, 1.0, 0.0).astype(BF16), lf)
    b_tot = jnp.sum(lf, axis=0, keepdims=True)
    o = _dot_nt(q * jnp.exp(b), st)
    row = lax.broadcasted_iota(jnp.int32, (c, 1), 0)
    att = jnp.zeros((c, c), F32)
    n = SUBLANES
    while n < c:
        ngroups = c // (2 * n)
        pieces = []
        for g in range(ngroups):
            r = g * 2 * n + (n if reverse else n - 1)
            pieces.append(jnp.broadcast_to(b[r:r + 1, :], (2 * n, HEAD_W)))
        ref = pieces[0] if ngroups == 1 else jnp.concatenate(pieces, axis=0)
        e = jnp.exp(-jnp.abs(b - ref))
        s = int(np.log2(n))
        second = ((row >> s) & 1) == 1
        q_part = jnp.logical_not(second) if reverse else second
        qt = jnp.where(q_part, q * e, 0.0)
        kt = jnp.where(q_part, 0.0, kf * e)
        att = att + jnp.where((ri >> (s + 1)) == (ci >> (s + 1)), _dot_nt(qt, kt), 0.0)
        n *= 2
    o = o + _dot(att, v) + _hgrn_diag(q, kf, v, b, reverse, c)
    kh = kf * jnp.exp(b_tot - b)
    st_new = st * jnp.exp(b_tot) + _dot_tn(v, kh)
    return o, st_new


def _mixer_b_kernel(*refs, t, c, layer, has_past, emit_state):
    refs = list(refs)
    q_ref, i_ref, f0_ref, f1_ref, g_ref, lb_ref, nw_ref = refs[:7]
    pos = 7
    s0_ref = None
    if has_past:
        s0_ref = refs[pos]
        pos += 1
    o_ref = refs[pos]
    pos += 1
    sfin_ref = None
    if emit_state:
        sfin_ref = refs[pos]
        pos += 1
    of_scr, ob_scr, st_scr = refs[pos:]

    if has_past:
        st_scr[0] = s0_ref[0, 0, 0, 0].T
        st_scr[1] = s0_ref[0, 0, 1, 0].T
    else:
        st_scr[...] = jnp.zeros(st_scr.shape, F32)

    lb_terms = None
    if layer > 0:
        lb_terms = []
        for d in range(2):
            lg = lb_ref[d]
            ex = jnp.exp(lg - jnp.max(lg, axis=0, keepdims=True))
            pr = ex / jnp.sum(ex, axis=0, keepdims=True)
            lb = jnp.clip(jnp.sum(pr[1:layer + 1], axis=0, keepdims=True), LB_EPS, 1.0 - LB_EPS)
            lb_terms.append((jnp.log(lb), jnp.log1p(-lb), 1.0 - lb))

    f_refs = (f0_ref, f1_ref)
    nc = t // c

    def chunk_pair(n, carry):
        for d in range(2):
            cidx = n if d == 0 else nc - 1 - n
            r0 = pl.multiple_of(cidx * c, c)
            z = f_refs[d][pl.ds(r0, c), :]
            if layer == 0:
                lf = _log_sigmoid(z)
                kf = _sigmoid(-z)
            else:
                log_lb, log1m_lb, one_m_lb = lb_terms[d]
                a2 = log1m_lb + _log_sigmoid(z)
                lf = jnp.maximum(log_lb, a2) + jnp.log1p(jnp.exp(-jnp.abs(log_lb - a2)))
                kf = one_m_lb * _sigmoid(-z)
            q = _silu(q_ref[pl.ds(r0, c), :])
            o, st_new = _hgrn_chunk(q, kf, i_ref[pl.ds(r0, c), :], lf, st_scr[d], d == 1, c)
            st_scr[d] = st_new
            dst = of_scr if d == 0 else ob_scr
            dst[pl.ds(r0, c), :] = o
        return carry

    lax.fori_loop(0, nc, chunk_pair, 0)

    _gated_norm_epilogue(of_scr, ob_scr, g_ref, nw_ref, o_ref, t)
    if emit_state:
        sfin_ref[0, 0, 0] = st_scr[0].T
        sfin_ref[0, 1, 0] = st_scr[1].T


def _mixer_b_call(proj, lb_logits, norm_w, past, layer, nseq, t, emit_state):
    c = min(CHUNK_B, t)
    depth = lb_logits.shape[1]
    cb = lambda off: (lambda b, h: (b, off // HEAD_W + h))
    in_specs = [pl.BlockSpec((t, HEAD_W), cb(COL_QB)), pl.BlockSpec((t, HEAD_W), cb(COL_IB)),
                pl.BlockSpec((t, HEAD_W), cb(COL_FB)), pl.BlockSpec((t, HEAD_W), cb(COL_FB + HEADS * HEAD_W)),
                pl.BlockSpec((t, HEAD_W), cb(COL_GB)),
                pl.BlockSpec((2, depth, HEAD_W), lambda b, h: (0, 0, h)),
                pl.BlockSpec((1, HEAD_W), lambda b, h: (0, 0))]
    args = [proj, proj, proj, proj, proj, lb_logits, norm_w]
    if past is not None:
        in_specs.append(pl.BlockSpec((1, 1, 2, 1, HEAD_W, HEAD_W), lambda b, h: (b, layer, 0, h, 0, 0)))
        args.append(past)
    out_shape = [jax.ShapeDtypeStruct((nseq * t, HEADS * HEAD_W), BF16)]
    out_specs = [pl.BlockSpec((t, HEAD_W), lambda b, h: (b, h))]
    if emit_state:
        out_shape.append(jax.ShapeDtypeStruct((nseq, 2, HEADS, HEAD_W, HEAD_W), F32))
        out_specs.append(pl.BlockSpec((1, 2, 1, HEAD_W, HEAD_W), lambda b, h: (b, 0, h, 0, 0)))
    seq = pltpu.VMEM((t, HEAD_W), F32)
    return pl.pallas_call(
        functools.partial(_mixer_b_kernel, t=t, c=c, layer=layer, has_past=past is not None, emit_state=emit_state),
        out_shape=tuple(out_shape), grid=(nseq, HEADS), in_specs=in_specs, out_specs=tuple(out_specs),
        scratch_shapes=[seq, seq, pltpu.VMEM((2, HEAD_W, HEAD_W), F32)],
        compiler_params=_cparams(("arbitrary", "arbitrary")),
        name="mixer_b",
    )(*args)


def _rms_head_pairs(x, w2):
    lane = lax.broadcasted_iota(jnp.int32, x.shape, 1)
    left = lane < C_HD
    sq = x * x
    s0 = jnp.sum(jnp.where(left, sq, 0.0), axis=-1, keepdims=True)
    s1 = jnp.sum(jnp.where(left, 0.0, sq), axis=-1, keepdims=True)
    ms = jnp.where(left, s0, s1) * (1.0 / C_HD)
    return x * lax.rsqrt(ms + NORM_EPS) * w2


def _rope_pairs(x, cos2, sin2):
    lane = lax.broadcasted_iota(jnp.int32, x.shape, 1)
    quarter = C_HD // 4
    swapped = jnp.where((lane & (2 * quarter - 1)) < quarter,
                        pltpu.roll(x, LANES - quarter, axis=1), pltpu.roll(x, quarter, axis=1))
    return x * cos2 + swapped * sin2


def _softmax_sink_av(scores, values, sink):
    m = sink
    for s in scores:
        m = jnp.maximum(m, jnp.max(s, axis=-1, keepdims=True))
    den = jnp.exp(sink - m)
    acc = None
    for s, v in zip(scores, values):
        p = jnp.exp(s - m)
        den = den + jnp.sum(p, axis=-1, keepdims=True)
        t = jnp.dot(p.astype(BF16), v, preferred_element_type=F32)
        acc = t if acc is None else acc + t
    return acc / den


def _attn_ctx_kernel(q_ref, k_ref, v_ref, qn_ref, kn_ref, sink_ref, o_ref, ko_ref, vo_ref, *, t):
    qw, kw = qn_ref[...], kn_ref[...]
    kn = _rms_head_pairs(k_ref[...], kw)
    ko_ref[0] = kn
    v = v_ref[...]
    vo_ref[0] = v
    knb, vb = kn.astype(BF16), v.astype(BF16)
    scale = C_HD ** -0.5
    for pair in range(C_QHEADS // 2):
        qp = (_rms_head_pairs(q_ref[:, pair * LANES:(pair + 1) * LANES], qw) * scale).astype(BF16)
        outs = []
        for half in range(2):
            hq = 2 * pair + half
            hk = hq // C_GROUP
            qh = qp[:, half * C_HD:(half + 1) * C_HD]
            s = _dot_nt(qh, knb[:, hk * C_HD:(hk + 1) * C_HD])
            sink = jnp.full((1, 1), sink_ref[hq], F32)
            outs.append(_softmax_sink_av([s], [vb[:, hk * C_HD:(hk + 1) * C_HD]], sink))
        o_ref[:, pair * LANES:(pair + 1) * LANES] = jnp.concatenate(outs, axis=1).astype(o_ref.dtype)


def _attn_ctx_call(proj, q_norm2, k_norm2, sink, nseq, t):
    return pl.pallas_call(
        functools.partial(_attn_ctx_kernel, t=t),
        out_shape=(jax.ShapeDtypeStruct((nseq * t, C_QHEADS * C_HD), BF16),
                   jax.ShapeDtypeStruct((nseq, t, LANES), F32), jax.ShapeDtypeStruct((nseq, t, LANES), F32)),
        grid=(nseq,),
        in_specs=[pl.BlockSpec((t, C_QHEADS * C_HD), lambda b: (b, COL_QC // (C_QHEADS * C_HD))),
                  pl.BlockSpec((t, LANES), lambda b: (b, COL_KC // LANES)),
                  pl.BlockSpec((t, LANES), lambda b: (b, COL_VC // LANES)),
                  pl.BlockSpec((1, LANES), lambda b: (0, 0)), pl.BlockSpec((1, LANES), lambda b: (0, 0)),
                  pl.BlockSpec(memory_space=pltpu.SMEM)],
        out_specs=(pl.BlockSpec((t, C_QHEADS * C_HD), lambda b: (b, 0)),
                   pl.BlockSpec((1, t, LANES), lambda b: (b, 0, 0)), pl.BlockSpec((1, t, LANES), lambda b: (b, 0, 0))),
        compiler_params=_cparams(("arbitrary",)),
        name="attn_ctx",
    )(proj, proj, proj, q_norm2, k_norm2, sink)


def _attn_lat_kernel(q_ref, k_ref, v_ref, kc_ref, vc_ref, qn_ref, kn_ref, cos_ref, sin_ref, sink_ref, o_ref,
                     qs_scr, ks_scr, vs_scr, *, t, past_len):
    qw, kw = qn_ref[...], kn_ref[...]
    scale = C_HD ** -0.5
    nrb = t // ROW_BLOCK
    blk = C_BLOCK

    ks_scr[0:blk, :] = jnp.zeros((blk, LANES), BF16)
    vs_scr[0:blk, :] = jnp.zeros((blk, LANES), BF16)
    ks_scr[t + blk:t + 2 * blk, :] = jnp.zeros((blk, LANES), BF16)
    vs_scr[t + blk:t + 2 * blk, :] = jnp.zeros((blk, LANES), BF16)

    def prep(r, carry):
        r0 = pl.multiple_of(r * ROW_BLOCK, ROW_BLOCK)
        cos2, sin2 = cos_ref[pl.ds(r0, ROW_BLOCK), :], sin_ref[pl.ds(r0, ROW_BLOCK), :]
        kn = _rope_pairs(_rms_head_pairs(k_ref[pl.ds(r0, ROW_BLOCK), :], kw), cos2, sin2)
        ks_scr[pl.ds(r0 + blk, ROW_BLOCK), :] = kn.astype(BF16)
        vs_scr[pl.ds(r0 + blk, ROW_BLOCK), :] = v_ref[pl.ds(r0, ROW_BLOCK), :].astype(BF16)
        for pair in range(C_QHEADS // 2):
            qp = _rms_head_pairs(q_ref[pl.ds(r0, ROW_BLOCK), pair * LANES:(pair + 1) * LANES], qw)
            qp = _rope_pairs(qp, cos2, sin2) * scale
            qs_scr[pl.ds(r0, ROW_BLOCK), pair * LANES:(pair + 1) * LANES] = qp.astype(BF16)
        return carry

    lax.fori_loop(0, nrb, prep, 0)

    kcb = kc_ref[0, 0].astype(BF16)
    vcb = vc_ref[0, 0].astype(BF16)
    qi = lax.broadcasted_iota(jnp.int32, (blk, 3 * blk), 0)
    kj = lax.broadcasted_iota(jnp.int32, (blk, 3 * blk), 1)
    band = jnp.logical_and(kj >= qi, kj <= qi + 2 * C_WINDOW)

    def qblock(n, carry):
        r0 = pl.multiple_of(n * blk, blk)
        kpos = kj + (r0 - blk)
        mask = jnp.logical_and(band, jnp.logical_and(kpos >= 0, kpos < t))
        kwin = ks_scr[pl.ds(r0, 3 * blk), :]
        vwin = vs_scr[pl.ds(r0, 3 * blk), :]
        for pair in range(C_QHEADS // 2):
            qp = qs_scr[pl.ds(r0, blk), pair * LANES:(pair + 1) * LANES]
            outs = []
            for half in range(2):
                hq = 2 * pair + half
                hk = hq // C_GROUP
                lo, hi = hk * C_HD, (hk + 1) * C_HD
                qh = qp[:, half * C_HD:(half + 1) * C_HD]
                s_loc = jnp.where(mask, _dot_nt(qh, kwin[:, lo:hi]), NEG_BIG)
                s_ctx = _dot_nt(qh, kcb[:, lo:hi])
                sink = jnp.full((1, 1), sink_ref[hq], F32)
                outs.append(_softmax_sink_av([s_loc, s_ctx], [vwin[:, lo:hi], vcb[:, lo:hi]], sink))
            o_ref[pl.ds(r0, blk), pair * LANES:(pair + 1) * LANES] = jnp.concatenate(outs, axis=1).astype(o_ref.dtype)
        return carry

    lax.fori_loop(0, t // blk, qblock, 0)


def _attn_lat_call(proj, cache_k, cache_v, q_norm2, k_norm2, cos2, sin2, sink, layer, nseq, t):
    past_len = cache_k.shape[2]
    qw = C_QHEADS * C_HD
    return pl.pallas_call(
        functools.partial(_attn_lat_kernel, t=t, past_len=past_len),
        out_shape=jax.ShapeDtypeStruct((nseq * t, qw), BF16),
        grid=(nseq,),
        in_specs=[pl.BlockSpec((t, qw), lambda b: (b, COL_QC // qw)),
                  pl.BlockSpec((t, LANES), lambda b: (b, COL_KC // LANES)),
                  pl.BlockSpec((t, LANES), lambda b: (b, COL_VC // LANES)),
                  pl.BlockSpec((1, 1, past_len, LANES), lambda b: (b, layer, 0, 0)),
                  pl.BlockSpec((1, 1, past_len, LANES), lambda b: (b, layer, 0, 0)),
                  pl.BlockSpec((1, LANES), lambda b: (0, 0)), pl.BlockSpec((1, LANES), lambda b: (0, 0)),
                  pl.BlockSpec((t, LANES), lambda b: (0, 0)), pl.BlockSpec((t, LANES), lambda b: (0, 0)),
                  pl.BlockSpec(memory_space=pltpu.SMEM)],
        out_specs=pl.BlockSpec((t, qw), lambda b: (b, 0)),
        scratch_shapes=[pltpu.VMEM((t, qw), BF16), pltpu.VMEM((t + 2 * C_BLOCK, LANES), BF16),
                        pltpu.VMEM((t + 2 * C_BLOCK, LANES), BF16)],
        compiler_params=_cparams(("arbitrary",)),
        name="attn_lat",
    )(proj, proj, proj, cache_k, cache_v, q_norm2, k_norm2, cos2, sin2, sink)


def _rope_tables(t):
    rows = t // GRID_W
    row = jnp.repeat(jnp.arange(rows, dtype=F32), GRID_W)
    col = jnp.tile(jnp.arange(GRID_W, dtype=F32), rows)
    nf = C_HD // 4
    inv = ROPE_THETA ** (-jnp.arange(nf, dtype=F32) / nf)
    ar, ac = row[:, None] * inv, col[:, None] * inv
    cos = jnp.concatenate([jnp.cos(ar), jnp.cos(ar), jnp.cos(ac), jnp.cos(ac)], axis=1)
    sin = jnp.concatenate([-jnp.sin(ar), jnp.sin(ar), -jnp.sin(ac), jnp.sin(ac)], axis=1)
    return jnp.tile(cos, (1, 2)), jnp.tile(sin, (1, 2))


def _merge_kernel(x_ref, g0_ref, g1_ref, g2_ref, oa_ref, ob_ref, oc_ref, mod_ref, wbr_ref, wout_ref, o_ref):
    merged = (_sigmoid(g0_ref[...]) * jnp.dot(oa_ref[...], wbr_ref[0], preferred_element_type=F32)
              + _sigmoid(g1_ref[...]) * jnp.dot(ob_ref[...], wbr_ref[1], preferred_element_type=F32)
              + _sigmoid(g2_ref[...]) * jnp.dot(oc_ref[...], wbr_ref[2], preferred_element_type=F32))
    res = jnp.dot(merged.astype(BF16), wout_ref[...], preferred_element_type=F32)
    o_ref[...] = x_ref[...] + mod_ref[0, 2:3, :] * res


def _merge_call(x2d, proj, oa, ob, oc, mod, w_br, w_out, tm, tiles_per_cond):
    m, d = x2d.shape
    mg = lambda r: pl.BlockSpec((tm, d), lambda i: (i, COL_MG // d + r))
    br = pl.BlockSpec((tm, BRANCH_W), lambda i: (i, 0))
    return pl.pallas_call(
        _merge_kernel,
        out_shape=jax.ShapeDtypeStruct((m, d), F32),
        grid=(m // tm,),
        in_specs=[pl.BlockSpec((tm, d), lambda i: (i, 0)), mg(0), mg(1), mg(2), br, br, br,
                  pl.BlockSpec((1, 6, d), lambda i: (i // tiles_per_cond, 0, 0)),
                  pl.BlockSpec((3, BRANCH_W, d), lambda i: (0, 0, 0)),
                  pl.BlockSpec((d, d), lambda i: (0, 0))],
        out_specs=pl.BlockSpec((tm, d), lambda i: (i, 0)),
        compiler_params=_cparams(("arbitrary",)),
        name="merge",
    )(x2d, proj, proj, proj, oa, ob, oc, mod, w_br, w_out)


FF_CHUNK = 256
HALO = BF16_ROWS


def _ffn_kernel(x_ref, xp_ref, xn_ref, mod_ref, nw_ref, wa_ref, wu_ref, cwa_ref, cwu_ref, wd_ref, o_ref,
                h_scr, acc_scr, *, tm, seq_len, nk):
    i, k = pl.program_id(0), pl.program_id(1)

    @pl.when(k == 0)
    def _():
        nw, sh, sc = nw_ref[...], mod_ref[0, 3:4, :], mod_ref[0, 4:5, :]

        def norm(x):
            return (_rms_rows(x, nw) * (1.0 + sc) + sh).astype(BF16)

        h_scr[0:HALO, :] = norm(xp_ref[...])
        h_scr[HALO:HALO + tm, :] = norm(x_ref[...])
        h_scr[HALO + tm:2 * HALO + tm, :] = norm(xn_ref[...])
        acc_scr[...] = jnp.zeros(acc_scr.shape, F32)

    h = h_scr[...]
    pos = (i * tm + lax.broadcasted_iota(jnp.int32, (tm, 1), 0)) & (seq_len - 1)
    has_prev, has_next = pos != 0, pos != seq_len - 1

    def conv(u, cw):
        prev, cur, nxt = u[HALO - 1:HALO - 1 + tm], u[HALO:HALO + tm], u[HALO + 1:HALO + 1 + tm]
        return jnp.where(has_prev, prev, 0.0) * cw[0:1, :] + cur * cw[1:2, :] + jnp.where(has_next, nxt, 0.0) * cw[2:3, :]

    a = conv(jnp.dot(h, wa_ref[...], preferred_element_type=F32), cwa_ref[...])
    u = conv(jnp.dot(h, wu_ref[...], preferred_element_type=F32), cwu_ref[...])
    acc_scr[...] += jnp.dot((_silu(a) * u).astype(BF16), wd_ref[...], preferred_element_type=F32)

    @pl.when(k == nk - 1)
    def _():
        o_ref[...] = x_ref[...] + mod_ref[0, 5:6, :] * acc_scr[...]


def _ffn_call(x2d, mod, nw, w_up, conv_w, w_down, tm, tiles_per_cond, seq_len):
    m, d = x2d.shape
    nk = D_FF // FF_CHUNK
    hb = tm // HALO
    last = m // HALO - 1
    return pl.pallas_call(
        functools.partial(_ffn_kernel, tm=tm, seq_len=seq_len, nk=nk),
        out_shape=jax.ShapeDtypeStruct((m, d), F32),
        grid=(m // tm, nk),
        in_specs=[pl.BlockSpec((tm, d), lambda i, k: (i, 0)),
                  pl.BlockSpec((HALO, d), lambda i, k: (jnp.maximum(i * hb - 1, 0), 0)),
                  pl.BlockSpec((HALO, d), lambda i, k: (jnp.minimum((i + 1) * hb, last), 0)),
                  pl.BlockSpec((1, 6, d), lambda i, k: (i // tiles_per_cond, 0, 0)),
                  pl.BlockSpec((1, d), lambda i, k: (0, 0)),
                  pl.BlockSpec((d, FF_CHUNK), lambda i, k: (0, k)),
                  pl.BlockSpec((d, FF_CHUNK), lambda i, k: (0, nk + k)),
                  pl.BlockSpec((3, FF_CHUNK), lambda i, k: (0, k)),
                  pl.BlockSpec((3, FF_CHUNK), lambda i, k: (0, nk + k)),
                  pl.BlockSpec((FF_CHUNK, d), lambda i, k: (k, 0))],
        out_specs=pl.BlockSpec((tm, d), lambda i, k: (i, 0)),
        scratch_shapes=[pltpu.VMEM((tm + 2 * HALO, d), BF16), pltpu.VMEM((tm, d), F32)],
        compiler_params=_cparams(("arbitrary", "arbitrary")),
        name="ffn",
    )(x2d, x2d, x2d, mod, nw, w_up, w_up, conv_w, conv_w, w_down)


def _permute_w_in(w):
    s = _SRC
    w_p = jnp.concatenate([w[:, s["mg"]:s["end"]], w[:, s["qa"]:s["beta"]], w[:, s["qb"]:s["mg"]]], axis=1)
    w_g = jnp.pad(w[:, s["beta"]:s["qb"]], ((0, 0), (0, LANES - (s["qb"] - s["beta"]))))
    return w_p.astype(BF16), w_g.astype(BF16)


def _gate_rows(gates, nseq, t, c):
    g = gates[:, :4 * HEADS].reshape(nseq, t // c, c, 2, 2, HEADS)
    return jnp.transpose(g, (0, 5, 1, 3, 4, 2)).reshape(nseq, HEADS, t // c, 4, c)


def _row_tile(rows, t):
    tm = 512
    while rows % tm or (t % tm and tm % t):
        tm //= 2
    return tm


def _group_forward(x3d, mod_g, prm, past, tables):
    nseq, t, d = x3d.shape
    x = x3d.reshape(nseq * t, d)
    tm = _row_tile(nseq * t, t)
    tiles_per_cond = (nseq * t) // tm if mod_g.shape[1] == 1 else t // tm
    depth = prm["w_p"].__len__()
    states_a, states_b, keys, vals = [], [], [], []
    for l in range(depth):
        mod = mod_g[l]
        proj, gates = _in_proj_call(x, mod, prm["norm1_w"][l], prm["w_p"][l], prm["w_g"][l], tm, tiles_per_cond)
        gates_r = _gate_rows(gates, nseq, t, min(CHUNK_A, t))
        emit = past is None
        res_a = _mixer_a_call(proj, gates_r, prm["conv_a"][l], prm["a_log"][l], prm["dt_bias"][l],
                              prm["norm_a"][l], None if emit else past[0], l, nseq, t, emit)
        res_b = _mixer_b_call(proj, prm["lb_logits"], prm["norm_b"][l], None if emit else past[1], l, nseq, t, emit)
        if emit:
            oc, kn, vn = _attn_ctx_call(proj, prm["q_norm2"][l], prm["k_norm2"][l], prm["sink"][l], nseq, t)
            states_a.append(res_a[1])
            states_b.append(res_b[1])
            keys.append(kn.reshape(nseq, t, C_KVHEADS, C_HD))
            vals.append(vn.reshape(nseq, t, C_KVHEADS, C_HD))
        else:
            oc = _attn_lat_call(proj, past[2], past[3], prm["q_norm2"][l], prm["k_norm2"][l], tables[0], tables[1],
                                prm["sink"][l], l, nseq, t)
        x = _merge_call(x, proj, res_a[0], res_b[0], oc, mod, prm["w_branch"][l], prm["w_out"][l], tm, tiles_per_cond)
        x = _ffn_call(x, mod, prm["norm2_w"][l], prm["w_up"][l], prm["conv_ffn"][l], prm["w_down"][l],
                      tm, tiles_per_cond, t)
    return x.reshape(nseq, t, d), states_a, states_b, keys, vals


def kernel(x_prompt, x_sample, state_delta, state_hgrn, cache_k, cache_v, c, c_ctx, ada_w, ada_b, norm1_w, w_in, conv_a, a_log, dt_bias, norm_a, lb_logits, norm_b, q_norm, k_norm, sink, w_branch, w_out, norm2_w, w_up, conv_ffn, w_down):
    depth = w_in.shape[0]
    d = x_prompt.shape[-1]
    n_lat = c.shape[0]

    cond = jnp.concatenate([c_ctx[None, :], c], axis=0)
    rows = -(-cond.shape[0] // SUBLANES) * SUBLANES
    cond = jnp.pad(cond, ((0, rows - cond.shape[0]), (0, 0)))
    mod_all = _mod_call(cond, ada_w, ada_b).reshape(depth, rows, 6, d)

    perm = [_permute_w_in(w_in[l]) for l in range(depth)]
    prm = dict(
        w_p=[p[0] for p in perm], w_g=[p[1] for p in perm],
        norm1_w=norm1_w.reshape(depth, 1, d), norm2_w=norm2_w.reshape(depth, 1, d),
        conv_a=conv_a, a_log=a_log, dt_bias=dt_bias, norm_a=norm_a.reshape(depth, 1, HEAD_W),
        lb_logits=lb_logits, norm_b=norm_b.reshape(depth, 1, HEAD_W),
        q_norm2=jnp.tile(q_norm, (1, 2)).reshape(depth, 1, LANES), k_norm2=jnp.tile(k_norm, (1, 2)).reshape(depth, 1, LANES),
        sink=sink, w_branch=w_branch.astype(BF16), w_out=w_out.astype(BF16),
        w_up=w_up.astype(BF16), conv_ffn=conv_ffn, w_down=w_down.astype(BF16))

    y_prompt, st_a, st_b, keys, vals = _group_forward(x_prompt, mod_all[:, 0:1], prm, None, None)

    past_len = cache_k.shape[2]
    past = (state_delta, state_hgrn,
            cache_k.reshape(cache_k.shape[0], depth, past_len, C_KVHEADS * C_HD),
            cache_v.reshape(cache_v.shape[0], depth, past_len, C_KVHEADS * C_HD))
    y_sample, _, _, _, _ = _group_forward(x_sample, mod_all[:, 1:1 + n_lat], prm, past, _rope_tables(x_sample.shape[1]))

    return (y_prompt, y_sample, jnp.stack(st_a, axis=1), jnp.stack(st_b, axis=1),
            jnp.stack(keys, axis=1), jnp.stack(vals, axis=1))
```

```python
import functools

import numpy as np
import jax
import jax.numpy as jnp
from jax import lax
from jax.experimental import pallas as pl
from jax.experimental.pallas import tpu as pltpu

F32 = jnp.float32
BF16 = jnp.bfloat16

D_MODEL = 1024
NORM_EPS = 1e-6
LB_EPS = 1e-6
NEG_BIG = -1e30
GRID_W = 64
ROPE_THETA = 10000.0

HEADS = 4
HEAD_W = 128
SHORT_CONV = 5
C_QHEADS = 8
C_KVHEADS = 2
C_GROUP = C_QHEADS // C_KVHEADS
C_HD = 64
C_WINDOW = 128
C_BLOCK = 128
BRANCH_W = 512
D_FF = 2816

LANES = 128
SUBLANES = 8
BF16_ROWS = 16
VMEM_LIMIT = 56 * 1024 * 1024

COL_MG = 0
COL_QA = 3072
COL_KA = 3584
COL_VA = 4096
COL_GA = 4608
COL_QB = 5120
COL_IB = 5632
COL_GB = 6144
COL_QC = 6656
COL_KC = 7168
COL_VC = 7296
PROJ_W = 7424
COL32_FB = 0
COL32_GATES = 1024
PROJ32_W = 1152
_SRC = dict(qa=0, ka=512, va=1024, ga=1536, beta=2048, alpha=2056, qb=2064, ib=2576, fb=3088, gb=4112,
            qc=4624, kc=5136, vc=5264, mg=5392, end=8464)

CHUNK_A = 256
CHUNK_B = 64
TRI_BASE = 16
TRI_MERGE = 2
MM_TILE = 768


def _cparams(sem):
    return pltpu.CompilerParams(dimension_semantics=sem, vmem_limit_bytes=VMEM_LIMIT)


def _resident(shape, index_map):
    return pl.BlockSpec(shape, index_map, pipeline_mode=pl.Buffered(1))


def _dot(a, b):
    return jnp.dot(a.astype(BF16), b.astype(BF16), preferred_element_type=F32)


def _dot_nt(a, b):
    return lax.dot_general(a.astype(BF16), b.astype(BF16), (((1,), (1,)), ((), ())), preferred_element_type=F32)


def _dot_tn(a, b):
    return lax.dot_general(a.astype(BF16), b.astype(BF16), (((0,), (0,)), ((), ())), preferred_element_type=F32)


def _sigmoid(x):
    return 1.0 / (1.0 + jnp.exp(-x))


def _silu(x):
    return x * _sigmoid(x)


def _softplus(x):
    return jnp.maximum(x, 0.0) + jnp.log(1.0 + jnp.exp(-jnp.abs(x)))


def _log_sigmoid(x):
    return jnp.minimum(x, 0.0) - jnp.log(1.0 + jnp.exp(-jnp.abs(x)))


def _rms_rows(x, w):
    ms = jnp.mean(x * x, axis=-1, keepdims=True)
    return x * lax.rsqrt(ms + NORM_EPS) * w


def _split_bf16(x, n):
    parts, r = [], x
    for _ in range(n):
        p = r.astype(BF16)
        parts.append(p)
        r = r - p.astype(F32)
    return parts


def _dot_sel(sel_bf16, x):
    out = None
    for p in _split_bf16(x, 3):
        t = jnp.dot(sel_bf16, p, preferred_element_type=F32)
        out = t if out is None else out + t
    return out


def _col_chunks(width, step):
    return [(lo, min(lo + step, width)) for lo in range(0, width, step)]


def _mod_kernel(c_ref, w_ref, b_ref, o_ref):
    c = c_ref[...]
    o_ref[0] = _dot(_silu(c), w_ref[0]) + b_ref[0]


def _mod_call(cond, ada_w, ada_b):
    depth, d, n = ada_w.shape
    rows = cond.shape[0]
    tn = 768
    return pl.pallas_call(
        _mod_kernel,
        out_shape=jax.ShapeDtypeStruct((depth, rows, n), F32),
        grid=(depth, n // tn),
        in_specs=[pl.BlockSpec((rows, d), lambda l, j: (0, 0)),
                  pl.BlockSpec((1, d, tn), lambda l, j: (l, 0, j)),
                  pl.BlockSpec((1, 1, tn), lambda l, j: (l, 0, j))],
        out_specs=pl.BlockSpec((1, rows, tn), lambda l, j: (l, 0, j)),
        compiler_params=_cparams(("arbitrary", "arbitrary")),
        name="mod",
    )(cond, ada_w, ada_b.reshape(depth, 1, n))


def _in_proj_kernel(x_ref, mod_ref, nw_ref, w16_ref, w32_ref, o16_ref, o32_ref):
    h = _rms_rows(x_ref[...], nw_ref[...]) * (1.0 + mod_ref[0, 1:2, :]) + mod_ref[0, 0:1, :]
    hb = h.astype(BF16)
    for lo, hi in _col_chunks(PROJ_W, MM_TILE):
        o16_ref[:, lo:hi] = jnp.dot(hb, w16_ref[:, lo:hi], preferred_element_type=F32).astype(BF16)
    for lo, hi in _col_chunks(PROJ32_W, MM_TILE):
        o32_ref[:, lo:hi] = jnp.dot(hb, w32_ref[:, lo:hi], preferred_element_type=F32)


def _in_proj_call(x2d, mod, nw, w16, w32, tm, tiles_per_cond):
    m, d = x2d.shape
    return pl.pallas_call(
        _in_proj_kernel,
        out_shape=(jax.ShapeDtypeStruct((m, PROJ_W), BF16), jax.ShapeDtypeStruct((m, PROJ32_W), F32)),
        grid=(m // tm,),
        in_specs=[pl.BlockSpec((tm, d), lambda i: (i, 0)),
                  pl.BlockSpec((1, 6, d), lambda i: (i // tiles_per_cond, 0, 0)),
                  _resident((1, d), lambda i: (0, 0)),
                  _resident((d, PROJ_W), lambda i: (0, 0)),
                  _resident((d, PROJ32_W), lambda i: (0, 0))],
        out_specs=(pl.BlockSpec((tm, PROJ_W), lambda i: (i, 0)),
                   pl.BlockSpec((tm, PROJ32_W), lambda i: (i, 0))),
        compiler_params=_cparams(("arbitrary",)),
        name="in_proj",
    )(x2d, mod, nw, w16, w32)


ROW_BLOCK = 256


def _gated_norm_epilogue(of_scr, ob_scr, gate_ref, nw_ref, o_ref, t):
    nw = nw_ref[...]

    def body(r, carry):
        r0 = pl.multiple_of(r * ROW_BLOCK, ROW_BLOCK)
        o = of_scr[pl.ds(r0, ROW_BLOCK), :] + ob_scr[pl.ds(r0, ROW_BLOCK), :]
        y = _rms_rows(o, nw) * _silu(gate_ref[pl.ds(r0, ROW_BLOCK), :].astype(F32))
        o_ref[pl.ds(r0, ROW_BLOCK), :] = y.astype(o_ref.dtype)
        return carry

    lax.fori_loop(0, t // ROW_BLOCK, body, 0)


def _tri_inverse(mat, ri, ci, c):
    shift = int(np.log2(TRI_BASE))
    mb = jnp.where((ri >> shift) == (ci >> shift), mat, 0.0)
    y = -mb
    p = _dot(mb, mb)
    for _ in range(shift - 2):
        yp = _dot(jnp.concatenate([y, p], axis=0), p)
        y = y + p + yp[:c]
        p = yp[c:]
    y = y + p + _dot(y, p)
    total = int(np.log2(c))
    while shift < total:
        factors = min(TRI_MERGE, total - shift)
        inner = (ri >> shift) == (ci >> shift)
        outer = (ri >> (shift + factors)) == (ci >> (shift + factors))
        cm = jnp.where(jnp.logical_and(outer, jnp.logical_not(inner)), mat, 0.0)
        w = cm + _dot(y, cm)
        if factors == 1:
            y = y - w - _dot(w, y)
        else:
            r = _dot(w, jnp.concatenate([w, y], axis=1))
            p, y = r[:, :c], y - w - r[:, c:]
            for f in range(1, factors):
                if f + 1 < factors:
                    r = _dot(p, jnp.concatenate([p, y], axis=1))
                    p, y = r[:, :c], y + p + r[:, c:]
                else:
                    y = y + p + _dot(p, y)
        shift += factors
    return jnp.where(ri == ci, 1.0, 0.0) + y


def _delta_chunk(q, k, v, beta_row, alpha_row, a_neg, dt_b, s, reverse, c):
    ri = lax.broadcasted_iota(jnp.int32, (c, c), 0)
    ci = lax.broadcasted_iota(jnp.int32, (c, c), 1)
    if reverse:
        incl, strict = ri <= ci, ri < ci
    else:
        incl, strict = ri >= ci, ri > ci
    eye = ri == ci
    beta_r = _sigmoid(beta_row)
    g_r = a_neg * _softplus(alpha_row + dt_b)
    gc_col = jnp.sum(jnp.where(incl, jnp.broadcast_to(g_r, (c, c)), 0.0), axis=1, keepdims=True)
    beta_col = jnp.sum(jnp.where(eye, jnp.broadcast_to(beta_r, (c, c)), 0.0), axis=1, keepdims=True)
    gc_row = jnp.sum(jnp.where(eye, jnp.broadcast_to(gc_col, (c, c)), 0.0), axis=0, keepdims=True)
    g_tot = jnp.sum(g_r, axis=1, keepdims=True)
    decay = jnp.where(incl, jnp.exp(jnp.where(incl, gc_col - gc_row, 0.0)), 0.0)

    kb = k.astype(BF16)
    kq = _dot_nt(jnp.concatenate([kb, q.astype(BF16)], axis=0), kb)
    t_inv = _tri_inverse(jnp.where(strict, kq[:c] * beta_col * decay, 0.0), ri, ci, c)
    e_gc = jnp.exp(gc_col)
    uw = _dot(t_inv, jnp.concatenate([v * beta_col, k * (beta_col * e_gc)], axis=1))
    wq = _dot(jnp.concatenate([uw[:, HEAD_W:], q * e_gc], axis=0), s)
    v_new = uw[:, :HEAD_W] - wq[:c]
    kd_t = (k * jnp.exp(g_tot - gc_col)).T
    os_ = _dot(jnp.concatenate([kq[c:] * decay, kd_t], axis=0), v_new)
    return wq[c:] + os_[:c], s * jnp.exp(g_tot) + os_[c:]


def _mixer_a_kernel(*refs, t, c, has_past, emit_state):
    refs = list(refs)
    q_ref, k_ref, v_ref, ga_ref, gr_ref, cwq_ref, cwk_ref, cwv_ref, alog_ref, dtb_ref, nw_ref = refs[:11]
    pos = 11
    s0_ref = None
    if has_past:
        s0_ref = refs[pos]
        pos += 1
    o_ref = refs[pos]
    pos += 1
    sfin_ref = None
    if emit_state:
        sfin_ref = refs[pos]
        pos += 1
    xp_scr, qn_scr, kn_scr, vn_scr, of_scr, ob_scr, s_scr = refs[pos:]

    h = pl.program_id(1)
    nrb = t // ROW_BLOCK
    pad = SUBLANES
    half = SHORT_CONV // 2

    xp_scr[0:pad, :] = jnp.zeros((pad, HEAD_W), F32)
    xp_scr[t + pad:t + 2 * pad, :] = jnp.zeros((pad, HEAD_W), F32)

    def conv_pass(x_ref, cw_ref, dst_scr, l2, scale):
        def cp(r, carry):
            r0 = pl.multiple_of(r * ROW_BLOCK, ROW_BLOCK)
            xp_scr[pl.ds(r0 + pad, ROW_BLOCK), :] = x_ref[pl.ds(r0, ROW_BLOCK), :].astype(F32)
            return carry

        lax.fori_loop(0, nrb, cp, 0)
        cw = cw_ref[...]

        def body(r, carry):
            r0 = pl.multiple_of(r * ROW_BLOCK, ROW_BLOCK)
            win = xp_scr[pl.ds(r0, ROW_BLOCK + 2 * pad), :]
            y = None
            for j in range(SHORT_CONV):
                lo = pad - half + j
                term = win[lo:lo + ROW_BLOCK, :] * cw[j:j + 1, :]
                y = term if y is None else y + term
            y = _silu(y)
            if l2:
                y = y * (lax.rsqrt(jnp.sum(y * y, axis=-1, keepdims=True) + NORM_EPS) * scale)
            dst_scr[pl.ds(r0, ROW_BLOCK), :] = y
            return carry

        lax.fori_loop(0, nrb, body, 0)

    conv_pass(q_ref, cwq_ref, qn_scr, True, HEAD_W ** -0.5)
    conv_pass(k_ref, cwk_ref, kn_scr, True, 1.0)
    conv_pass(v_ref, cwv_ref, vn_scr, False, 1.0)

    ones = jnp.ones((1, c), F32)
    a_neg = [-jnp.exp(ones * alog_ref[d, h]) for d in range(2)]
    dt_b = [dtb_ref[d, h] for d in range(2)]
    nc = t // c

    if has_past:
        s_scr[0] = s0_ref[0, 0, 0, 0]
        s_scr[1] = s0_ref[0, 0, 1, 0]
    else:
        s_scr[...] = jnp.zeros(s_scr.shape, F32)

    def chunk_pair(n, carry):
        for d in range(2):
            cidx = n if d == 0 else nc - 1 - n
            r0 = pl.multiple_of(cidx * c, c)
            gr = gr_ref[0, 0, cidx]
            o, s_new = _delta_chunk(qn_scr[pl.ds(r0, c), :], kn_scr[pl.ds(r0, c), :], vn_scr[pl.ds(r0, c), :],
                                    gr[d:d + 1, :], gr[2 + d:3 + d, :], a_neg[d], dt_b[d], s_scr[d], d == 1, c)
            s_scr[d] = s_new
            dst = of_scr if d == 0 else ob_scr
            dst[pl.ds(r0, c), :] = o
        return carry

    lax.fori_loop(0, nc, chunk_pair, 0)

    _gated_norm_epilogue(of_scr, ob_scr, ga_ref, nw_ref, o_ref, t)
    if emit_state:
        sfin_ref[0, 0, 0] = s_scr[0]
        sfin_ref[0, 1, 0] = s_scr[1]


def _mixer_a_call(proj, gates_r, conv_w, a_log, dt_bias, norm_w, past, layer, nseq, t, emit_state):
    c = min(CHUNK_A, t)
    nc = t // c
    cb = lambda off: (lambda b, h: (b, off // HEAD_W + h))
    in_specs = [pl.BlockSpec((t, HEAD_W), cb(COL_QA)), pl.BlockSpec((t, HEAD_W), cb(COL_KA)),
                pl.BlockSpec((t, HEAD_W), cb(COL_VA)), pl.BlockSpec((t, HEAD_W), cb(COL_GA)),
                pl.BlockSpec((1, 1, nc, 4, c), lambda b, h: (b, h, 0, 0, 0)),
                pl.BlockSpec((SHORT_CONV, HEAD_W), lambda b, h: (0, h)),
                pl.BlockSpec((SHORT_CONV, HEAD_W), lambda b, h: (0, HEADS + h)),
                pl.BlockSpec((SHORT_CONV, HEAD_W), lambda b, h: (0, 2 * HEADS + h)),
                pl.BlockSpec(memory_space=pltpu.SMEM), pl.BlockSpec(memory_space=pltpu.SMEM),
                pl.BlockSpec((1, HEAD_W), lambda b, h: (0, 0))]
    args = [proj, proj, proj, proj, gates_r, conv_w, conv_w, conv_w, a_log, dt_bias, norm_w]
    if past is not None:
        in_specs.append(pl.BlockSpec((1, 1, 2, 1, HEAD_W, HEAD_W), lambda b, h: (b, layer, 0, h, 0, 0)))
        args.append(past)
    out_shape = [jax.ShapeDtypeStruct((nseq * t, HEADS * HEAD_W), BF16)]
    out_specs = [pl.BlockSpec((t, HEAD_W), lambda b, h: (b, h))]
    if emit_state:
        out_shape.append(jax.ShapeDtypeStruct((nseq, 2, HEADS, HEAD_W, HEAD_W), F32))
        out_specs.append(pl.BlockSpec((1, 2, 1, HEAD_W, HEAD_W), lambda b, h: (b, 0, h, 0, 0)))
    seq = pltpu.VMEM((t, HEAD_W), F32)
    return pl.pallas_call(
        functools.partial(_mixer_a_kernel, t=t, c=c, has_past=past is not None, emit_state=emit_state),
        out_shape=tuple(out_shape), grid=(nseq, HEADS), in_specs=in_specs, out_specs=tuple(out_specs),
        scratch_shapes=[pltpu.VMEM((t + 2 * SUBLANES, HEAD_W), F32), seq, seq, seq, seq, seq,
                        pltpu.VMEM((2, HEAD_W, HEAD_W), F32)],
        compiler_params=_cparams(("arbitrary", "arbitrary")),
        name="mixer_a",
    )(*args)


def _hgrn_diag(q, kf, v, b, reverse, c):
    nb = c // SUBLANES
    q3, k3, v3, b3 = (a.reshape(nb, SUBLANES, HEAD_W) for a in (q, kf, v, b))
    sub = lax.broadcasted_iota(jnp.int32, (nb, SUBLANES, HEAD_W), 1)
    o3 = jnp.zeros((nb, SUBLANES, HEAD_W), F32)
    for j in range(SUBLANES):
        mask = (sub <= j) if reverse else (sub >= j)
        e = jnp.exp(jnp.where(mask, b3 - b3[:, j:j + 1, :], 0.0))
        a = jnp.sum(jnp.where(mask, q3 * e * k3[:, j:j + 1, :], 0.0), axis=-1, keepdims=True)
        o3 = o3 + a * v3[:, j:j + 1, :]
    return o3.reshape(c, HEAD_W)


def _hgrn_chunk(q, kf, v, lf, st, reverse, c):
    ri = lax.broadcasted_iota(jnp.int32, (c, c), 0)
    ci = lax.broadcasted_iota(jnp.int32, (c, c), 1)
    incl = (ri <= ci) if reverse else (ri >= ci)
    b = _dot_sel(jnp.where(incl, 1.0, 0.0).astype(BF16), lf)
    b_tot = jnp.sum(lf, axis=0, keepdims=True)
    o = _dot_nt(q * jnp.exp(b), st)
    row = lax.broadcasted_iota(jnp.int32, (c, 1), 0)
    att = jnp.zeros((c, c), F32)
    n = SUBLANES
    while n < c:
        ngroups = c // (2 * n)
        pieces = []
        for g in range(ngroups):
            r = g * 2 * n + (n if reverse else n - 1)
            pieces.append(jnp.broadcast_to(b[r:r + 1, :], (2 * n, HEAD_W)))
        ref = pieces[0] if ngroups == 1 else jnp.concatenate(pieces, axis=0)
        e = jnp.exp(-jnp.abs(b - ref))
        s = int(np.log2(n))
        second = ((row >> s) & 1) == 1
        q_part = jnp.logical_not(second) if reverse else second
        qt = jnp.where(q_part, q * e, 0.0)
        kt = jnp.where(q_part, 0.0, kf * e)
        att = att + jnp.where((ri >> (s + 1)) == (ci >> (s + 1)), _dot_nt(qt, kt), 0.0)
        n *= 2
    o = o + _dot(att, v) + _hgrn_diag(q, kf, v, b, reverse, c)
    kh = kf * jnp.exp(b_tot - b)
    st_new = st * jnp.exp(b_tot) + _dot_tn(v, kh)
    return o, st_new


def _mixer_b_kernel(*refs, t, c, layer, has_past, emit_state):
    refs = list(refs)
    q_ref, i_ref, f0_ref, f1_ref, g_ref, lb_ref, nw_ref = refs[:7]
    pos = 7
    s0_ref = None
    if has_past:
        s0_ref = refs[pos]
        pos += 1
    o_ref = refs[pos]
    pos += 1
    sfin_ref = None
    if emit_state:
        sfin_ref = refs[pos]
        pos += 1
    of_scr, ob_scr, st_scr = refs[pos:]

    if has_past:
        st_scr[0] = s0_ref[0, 0, 0, 0].T
        st_scr[1] = s0_ref[0, 0, 1, 0].T
    else:
        st_scr[...] = jnp.zeros(st_scr.shape, F32)

    lb_terms = None
    if layer > 0:
        lb_terms = []
        for d in range(2):
            lg = lb_ref[d]
            ex = jnp.exp(lg - jnp.max(lg, axis=0, keepdims=True))
            pr = ex / jnp.sum(ex, axis=0, keepdims=True)
            lb = jnp.clip(jnp.sum(pr[1:layer + 1], axis=0, keepdims=True), LB_EPS, 1.0 - LB_EPS)
            lb_terms.append((jnp.log(lb), jnp.log1p(-lb), 1.0 - lb))

    f_refs = (f0_ref, f1_ref)
    nc = t // c

    def chunk_pair(n, carry):
        for d in range(2):
            cidx = n if d == 0 else nc - 1 - n
            r0 = pl.multiple_of(cidx * c, c)
            z = f_refs[d][pl.ds(r0, c), :]
            if layer == 0:
                lf = _log_sigmoid(z)
                kf = _sigmoid(-z)
            else:
                log_lb, log1m_lb, one_m_lb = lb_terms[d]
                a2 = log1m_lb + _log_sigmoid(z)
                lf = jnp.maximum(log_lb, a2) + jnp.log(1.0 + jnp.exp(-jnp.abs(log_lb - a2)))
                kf = one_m_lb * _sigmoid(-z)
            q = _silu(q_ref[pl.ds(r0, c), :].astype(F32))
            o, st_new = _hgrn_chunk(q, kf, i_ref[pl.ds(r0, c), :].astype(F32), lf, st_scr[d], d == 1, c)
            st_scr[d] = st_new
            dst = of_scr if d == 0 else ob_scr
            dst[pl.ds(r0, c), :] = o
        return carry

    lax.fori_loop(0, nc, chunk_pair, 0, unroll=2 if nc % 2 == 0 else 1)

    _gated_norm_epilogue(of_scr, ob_scr, g_ref, nw_ref, o_ref, t)
    if emit_state:
        sfin_ref[0, 0, 0] = st_scr[0].T
        sfin_ref[0, 1, 0] = st_scr[1].T


def _mixer_b_call(proj, proj32, lb_logits, norm_w, past, layer, nseq, t, emit_state):
    c = min(CHUNK_B, t)
    depth = lb_logits.shape[1]
    cb = lambda off: (lambda b, h: (b, off // HEAD_W + h))
    in_specs = [pl.BlockSpec((t, HEAD_W), cb(COL_QB)), pl.BlockSpec((t, HEAD_W), cb(COL_IB)),
                pl.BlockSpec((t, HEAD_W), cb(COL32_FB)), pl.BlockSpec((t, HEAD_W), cb(COL32_FB + HEADS * HEAD_W)),
                pl.BlockSpec((t, HEAD_W), cb(COL_GB)),
                pl.BlockSpec((2, depth, HEAD_W), lambda b, h: (0, 0, h)),
                pl.BlockSpec((1, HEAD_W), lambda b, h: (0, 0))]
    args = [proj, proj, proj32, proj32, proj, lb_logits, norm_w]
    if past is not None:
        in_specs.append(pl.BlockSpec((1, 1, 2, 1, HEAD_W, HEAD_W), lambda b, h: (b, layer, 0, h, 0, 0)))
        args.append(past)
    out_shape = [jax.ShapeDtypeStruct((nseq * t, HEADS * HEAD_W), BF16)]
    out_specs = [pl.BlockSpec((t, HEAD_W), lambda b, h: (b, h))]
    if emit_state:
        out_shape.append(jax.ShapeDtypeStruct((nseq, 2, HEADS, HEAD_W, HEAD_W), F32))
        out_specs.append(pl.BlockSpec((1, 2, 1, HEAD_W, HEAD_W), lambda b, h: (b, 0, h, 0, 0)))
    seq = pltpu.VMEM((t, HEAD_W), F32)
    return pl.pallas_call(
        functools.partial(_mixer_b_kernel, t=t, c=c, layer=layer, has_past=past is not None, emit_state=emit_state),
        out_shape=tuple(out_shape), grid=(nseq, HEADS), in_specs=in_specs, out_specs=tuple(out_specs),
        scratch_shapes=[seq, seq, pltpu.VMEM((2, HEAD_W, HEAD_W), F32)],
        compiler_params=_cparams(("arbitrary", "arbitrary")),
        name="mixer_b",
    )(*args)


def _rms_head_pairs(x, w2):
    lane = lax.broadcasted_iota(jnp.int32, x.shape, 1)
    left = lane < C_HD
    sq = x * x
    s0 = jnp.sum(jnp.where(left, sq, 0.0), axis=-1, keepdims=True)
    s1 = jnp.sum(jnp.where(left, 0.0, sq), axis=-1, keepdims=True)
    ms = jnp.where(left, s0, s1) * (1.0 / C_HD)
    return x * lax.rsqrt(ms + NORM_EPS) * w2


def _rope_pairs(x, cos2, sin2):
    lane = lax.broadcasted_iota(jnp.int32, x.shape, 1)
    quarter = C_HD // 4
    swapped = jnp.where((lane & (2 * quarter - 1)) < quarter,
                        pltpu.roll(x, LANES - quarter, axis=1), pltpu.roll(x, quarter, axis=1))
    return x * cos2 + swapped * sin2


def _softmax_sink_av(scores, values, sink):
    m = sink
    for s in scores:
        m = jnp.maximum(m, jnp.max(s, axis=-1, keepdims=True))
    den = jnp.exp(sink - m)
    acc = None
    for s, v in zip(scores, values):
        p = jnp.exp(s - m)
        den = den + jnp.sum(p, axis=-1, keepdims=True)
        t = jnp.dot(p.astype(BF16), v, preferred_element_type=F32)
        acc = t if acc is None else acc + t
    return acc / den


def _attn_ctx_kernel(q_ref, k_ref, v_ref, qn_ref, kn_ref, sink_ref, o_ref, ko_ref, vo_ref, *, t):
    qw, kw = qn_ref[...], kn_ref[...]
    kn = _rms_head_pairs(k_ref[...].astype(F32), kw)
    ko_ref[0] = kn
    v = v_ref[...]
    vo_ref[0] = v.astype(F32)
    knb, vb = kn.astype(BF16), v
    scale = C_HD ** -0.5
    for pair in range(C_QHEADS // 2):
        qp = (_rms_head_pairs(q_ref[:, pair * LANES:(pair + 1) * LANES].astype(F32), qw) * scale).astype(BF16)
        outs = []
        for half in range(2):
            hq = 2 * pair + half
            hk = hq // C_GROUP
            qh = qp[:, half * C_HD:(half + 1) * C_HD]
            s = _dot_nt(qh, knb[:, hk * C_HD:(hk + 1) * C_HD])
            sink = jnp.full((1, 1), sink_ref[hq], F32)
            outs.append(_softmax_sink_av([s], [vb[:, hk * C_HD:(hk + 1) * C_HD]], sink))
        o_ref[:, pair * LANES:(pair + 1) * LANES] = jnp.concatenate(outs, axis=1).astype(o_ref.dtype)


def _attn_ctx_call(proj, q_norm2, k_norm2, sink, nseq, t):
    return pl.pallas_call(
        functools.partial(_attn_ctx_kernel, t=t),
        out_shape=(jax.ShapeDtypeStruct((nseq * t, C_QHEADS * C_HD), BF16),
                   jax.ShapeDtypeStruct((nseq, t, LANES), F32), jax.ShapeDtypeStruct((nseq, t, LANES), F32)),
        grid=(nseq,),
        in_specs=[pl.BlockSpec((t, C_QHEADS * C_HD), lambda b: (b, COL_QC // (C_QHEADS * C_HD))),
                  pl.BlockSpec((t, LANES), lambda b: (b, COL_KC // LANES)),
                  pl.BlockSpec((t, LANES), lambda b: (b, COL_VC // LANES)),
                  pl.BlockSpec((1, LANES), lambda b: (0, 0)), pl.BlockSpec((1, LANES), lambda b: (0, 0)),
                  pl.BlockSpec(memory_space=pltpu.SMEM)],
        out_specs=(pl.BlockSpec((t, C_QHEADS * C_HD), lambda b: (b, 0)),
                   pl.BlockSpec((1, t, LANES), lambda b: (b, 0, 0)), pl.BlockSpec((1, t, LANES), lambda b: (b, 0, 0))),
        compiler_params=_cparams(("arbitrary",)),
        name="attn_ctx",
    )(proj, proj, proj, q_norm2, k_norm2, sink)


def _attn_lat_kernel(q_ref, k_ref, v_ref, kc_ref, vc_ref, qn_ref, kn_ref, cos_ref, sin_ref, sink_ref, o_ref,
                     qs_scr, ks_scr, vs_scr, *, t, past_len):
    qw, kw = qn_ref[...], kn_ref[...]
    scale = C_HD ** -0.5
    nrb = t // ROW_BLOCK
    blk = C_BLOCK

    ks_scr[0:blk, :] = jnp.zeros((blk, LANES), BF16)
    vs_scr[0:blk, :] = jnp.zeros((blk, LANES), BF16)
    ks_scr[t + blk:t + 2 * blk, :] = jnp.zeros((blk, LANES), BF16)
    vs_scr[t + blk:t + 2 * blk, :] = jnp.zeros((blk, LANES), BF16)

    def prep(r, carry):
        r0 = pl.multiple_of(r * ROW_BLOCK, ROW_BLOCK)
        cos2, sin2 = cos_ref[pl.ds(r0, ROW_BLOCK), :], sin_ref[pl.ds(r0, ROW_BLOCK), :]
        kn = _rope_pairs(_rms_head_pairs(k_ref[pl.ds(r0, ROW_BLOCK), :].astype(F32), kw), cos2, sin2)
        ks_scr[pl.ds(r0 + blk, ROW_BLOCK), :] = kn.astype(BF16)
        vs_scr[pl.ds(r0 + blk, ROW_BLOCK), :] = v_ref[pl.ds(r0, ROW_BLOCK), :]
        for pair in range(C_QHEADS // 2):
            qp = _rms_head_pairs(q_ref[pl.ds(r0, ROW_BLOCK), pair * LANES:(pair + 1) * LANES].astype(F32), qw)
            qp = _rope_pairs(qp, cos2, sin2) * scale
            qs_scr[pl.ds(r0, ROW_BLOCK), pair * LANES:(pair + 1) * LANES] = qp.astype(BF16)
        return carry

    lax.fori_loop(0, nrb, prep, 0)

    kcb = kc_ref[0, 0].astype(BF16)
    vcb = vc_ref[0, 0].astype(BF16)
    qi = lax.broadcasted_iota(jnp.int32, (blk, 3 * blk), 0)
    kj = lax.broadcasted_iota(jnp.int32, (blk, 3 * blk), 1)
    band = jnp.logical_and(kj >= qi, kj <= qi + 2 * C_WINDOW)

    def qblock(n, carry):
        r0 = pl.multiple_of(n * blk, blk)
        kpos = kj + (r0 - blk)
        mask = jnp.logical_and(band, jnp.logical_and(kpos >= 0, kpos < t))
        kwin = ks_scr[pl.ds(r0, 3 * blk), :]
        vwin = vs_scr[pl.ds(r0, 3 * blk), :]
        for pair in range(C_QHEADS // 2):
            qp = qs_scr[pl.ds(r0, blk), pair * LANES:(pair + 1) * LANES]
            outs = []
            for half in range(2):
                hq = 2 * pair + half
                hk = hq // C_GROUP
                lo, hi = hk * C_HD, (hk + 1) * C_HD
                qh = qp[:, half * C_HD:(half + 1) * C_HD]
                s_loc = jnp.where(mask, _dot_nt(qh, kwin[:, lo:hi]), NEG_BIG)
                s_ctx = _dot_nt(qh, kcb[:, lo:hi])
                sink = jnp.full((1, 1), sink_ref[hq], F32)
                outs.append(_softmax_sink_av([s_loc, s_ctx], [vwin[:, lo:hi], vcb[:, lo:hi]], sink))
            o_ref[pl.ds(r0, blk), pair * LANES:(pair + 1) * LANES] = jnp.concatenate(outs, axis=1).astype(o_ref.dtype)
        return carry

    lax.fori_loop(0, t // blk, qblock, 0)


def _attn_lat_call(proj, cache_k, cache_v, q_norm2, k_norm2, cos2, sin2, sink, layer, nseq, t):
    past_len = cache_k.shape[2]
    qw = C_QHEADS * C_HD
    return pl.pallas_call(
        functools.partial(_attn_lat_kernel, t=t, past_len=past_len),
        out_shape=jax.ShapeDtypeStruct((nseq * t, qw), BF16),
        grid=(nseq,),
        in_specs=[pl.BlockSpec((t, qw), lambda b: (b, COL_QC // qw)),
                  pl.BlockSpec((t, LANES), lambda b: (b, COL_KC // LANES)),
                  pl.BlockSpec((t, LANES), lambda b: (b, COL_VC // LANES)),
                  pl.BlockSpec((1, 1, past_len, LANES), lambda b: (b, layer, 0, 0)),
                  pl.BlockSpec((1, 1, past_len, LANES), lambda b: (b, layer, 0, 0)),
                  pl.BlockSpec((1, LANES), lambda b: (0, 0)), pl.BlockSpec((1, LANES), lambda b: (0, 0)),
                  _resident((t, LANES), lambda b: (0, 0)), _resident((t, LANES), lambda b: (0, 0)),
                  pl.BlockSpec(memory_space=pltpu.SMEM)],
        out_specs=pl.BlockSpec((t, qw), lambda b: (b, 0)),
        scratch_shapes=[pltpu.VMEM((t, qw), BF16), pltpu.VMEM((t + 2 * C_BLOCK, LANES), BF16),
                        pltpu.VMEM((t + 2 * C_BLOCK, LANES), BF16)],
        compiler_params=_cparams(("arbitrary",)),
        name="attn_lat",
    )(proj, proj, proj, cache_k, cache_v, q_norm2, k_norm2, cos2, sin2, sink)


def _rope_tables(t):
    rows = t // GRID_W
    row = jnp.repeat(jnp.arange(rows, dtype=F32), GRID_W)
    col = jnp.tile(jnp.arange(GRID_W, dtype=F32), rows)
    nf = C_HD // 4
    inv = ROPE_THETA ** (-jnp.arange(nf, dtype=F32) / nf)
    ar, ac = row[:, None] * inv, col[:, None] * inv
    cos = jnp.concatenate([jnp.cos(ar), jnp.cos(ar), jnp.cos(ac), jnp.cos(ac)], axis=1)
    sin = jnp.concatenate([-jnp.sin(ar), jnp.sin(ar), -jnp.sin(ac), jnp.sin(ac)], axis=1)
    return jnp.tile(cos, (1, 2)), jnp.tile(sin, (1, 2))


def _merge_kernel(x_ref, g0_ref, g1_ref, g2_ref, oa_ref, ob_ref, oc_ref, mod_ref, wbr_ref, wout_ref, o_ref):
    merged = (_sigmoid(g0_ref[...].astype(F32)) * jnp.dot(oa_ref[...], wbr_ref[0], preferred_element_type=F32)
              + _sigmoid(g1_ref[...].astype(F32)) * jnp.dot(ob_ref[...], wbr_ref[1], preferred_element_type=F32)
              + _sigmoid(g2_ref[...].astype(F32)) * jnp.dot(oc_ref[...], wbr_ref[2], preferred_element_type=F32))
    res = jnp.dot(merged.astype(BF16), wout_ref[...], preferred_element_type=F32)
    o_ref[...] = x_ref[...] + mod_ref[0, 2:3, :] * res


def _merge_call(x2d, proj, oa, ob, oc, mod, w_br, w_out, tm, tiles_per_cond):
    m, d = x2d.shape
    mg = lambda r: pl.BlockSpec((tm, d), lambda i: (i, COL_MG // d + r))
    br = pl.BlockSpec((tm, BRANCH_W), lambda i: (i, 0))
    return pl.pallas_call(
        _merge_kernel,
        out_shape=jax.ShapeDtypeStruct((m, d), F32),
        grid=(m // tm,),
        in_specs=[pl.BlockSpec((tm, d), lambda i: (i, 0)), mg(0), mg(1), mg(2), br, br, br,
                  pl.BlockSpec((1, 6, d), lambda i: (i // tiles_per_cond, 0, 0)),
                  _resident((3, BRANCH_W, d), lambda i: (0, 0, 0)),
                  _resident((d, d), lambda i: (0, 0))],
        out_specs=pl.BlockSpec((tm, d), lambda i: (i, 0)),
        compiler_params=_cparams(("arbitrary",)),
        name="merge",
    )(x2d, proj, proj, proj, oa, ob, oc, mod, w_br, w_out)


FF_CHUNK = 256
HALO = BF16_ROWS


def _ffn_kernel(x_ref, xp_ref, xn_ref, mod_ref, nw_ref, wup_ref, cw_ref, wd_ref, o_ref, h_scr, act_scr, *,
                tm, seq_len):
    i = pl.program_id(0)
    nseg = max(1, tm // seq_len)
    seg = tm // nseg
    nw, sh, sc = nw_ref[...], mod_ref[0, 3:4, :], mod_ref[0, 4:5, :]

    def norm(x):
        return _rms_rows(x, nw) * (1.0 + sc) + sh

    has_prev = ((i * tm) & (seq_len - 1)) != 0
    has_next = (((i + 1) * tm) & (seq_len - 1)) != 0
    zero_halo = jnp.zeros((HALO, x_ref.shape[1]), BF16)
    for s in range(nseg):
        h_scr[s, HALO:HALO + seg, :] = norm(x_ref[s * seg:(s + 1) * seg, :]).astype(BF16)
        if s == 0:
            h_scr[s, 0:HALO, :] = (norm(xp_ref[...]) * jnp.where(has_prev, 1.0, 0.0)).astype(BF16)
        else:
            h_scr[s, 0:HALO, :] = zero_halo
        if s == nseg - 1:
            h_scr[s, HALO + seg:2 * HALO + seg, :] = (norm(xn_ref[...]) * jnp.where(has_next, 1.0, 0.0)).astype(BF16)
        else:
            h_scr[s, HALO + seg:2 * HALO + seg, :] = zero_halo

    def conv(u, cw):
        return (u[HALO - 1:HALO - 1 + seg] * cw[0:1, :] + u[HALO:HALO + seg] * cw[1:2, :]
                + u[HALO + 1:HALO + 1 + seg] * cw[2:3, :])

    for lo, hi in _col_chunks(D_FF, FF_CHUNK):
        for s in range(nseg):
            h = h_scr[s]
            a = conv(jnp.dot(h, wup_ref[:, lo:hi], preferred_element_type=F32), cw_ref[:, lo:hi])
            u = conv(jnp.dot(h, wup_ref[:, D_FF + lo:D_FF + hi], preferred_element_type=F32),
                     cw_ref[:, D_FF + lo:D_FF + hi])
            act_scr[s * seg:(s + 1) * seg, lo:hi] = (_silu(a) * u).astype(BF16)

    o_ref[...] = x_ref[...] + mod_ref[0, 5:6, :] * jnp.dot(act_scr[...], wd_ref[...], preferred_element_type=F32)


def _ffn_call(x2d, mod, nw, w_up, conv_w, w_down, tm, tiles_per_cond, seq_len):
    m, d = x2d.shape
    hb = tm // HALO
    last = m // HALO - 1
    nseg = max(1, tm // seq_len)
    seg = tm // nseg
    return pl.pallas_call(
        functools.partial(_ffn_kernel, tm=tm, seq_len=seq_len),
        out_shape=jax.ShapeDtypeStruct((m, d), F32),
        grid=(m // tm,),
        in_specs=[pl.BlockSpec((tm, d), lambda i: (i, 0)),
                  pl.BlockSpec((HALO, d), lambda i: (jnp.maximum(i * hb - 1, 0), 0)),
                  pl.BlockSpec((HALO, d), lambda i: (jnp.minimum((i + 1) * hb, last), 0)),
                  pl.BlockSpec((1, 6, d), lambda i: (i // tiles_per_cond, 0, 0)),
                  _resident((1, d), lambda i: (0, 0)),
                  _resident((d, 2 * D_FF), lambda i: (0, 0)),
                  _resident((3, 2 * D_FF), lambda i: (0, 0)),
                  _resident((D_FF, d), lambda i: (0, 0))],
        out_specs=pl.BlockSpec((tm, d), lambda i: (i, 0)),
        scratch_shapes=[pltpu.VMEM((nseg, seg + 2 * HALO, d), BF16), pltpu.VMEM((tm, D_FF), BF16)],
        compiler_params=_cparams(("arbitrary",)),
        name="ffn",
    )(x2d, x2d, x2d, mod, nw, w_up, conv_w, w_down)


def _permute_w_in(w):
    s = _SRC
    w16 = jnp.concatenate([w[:, s["mg"]:s["end"]], w[:, s["qa"]:s["beta"]], w[:, s["qb"]:s["fb"]],
                           w[:, s["gb"]:s["mg"]]], axis=1)
    n_gate = s["qb"] - s["beta"]
    w32 = jnp.concatenate([w[:, s["fb"]:s["gb"]], jnp.pad(w[:, s["beta"]:s["qb"]], ((0, 0), (0, LANES - n_gate)))],
                          axis=1)
    return w16.astype(BF16), w32.astype(BF16)


def _gate_rows(proj32, nseq, t, c):
    g = proj32[:, COL32_GATES:COL32_GATES + 4 * HEADS].reshape(nseq, t // c, c, 2, 2, HEADS)
    return jnp.transpose(g, (0, 5, 1, 3, 4, 2)).reshape(nseq, HEADS, t // c, 4, c)


def _row_tile(rows, t):
    tm = 512
    while rows % tm or (t % tm and tm % t):
        tm //= 2
    return tm


def _group_forward(x3d, mod_g, prm, past, tables):
    nseq, t, d = x3d.shape
    x = x3d.reshape(nseq * t, d)
    tm = _row_tile(nseq * t, t)
    tiles_per_cond = (nseq * t) // tm if mod_g.shape[1] == 1 else t // tm
    emit = past is None
    states_a, states_b, keys, vals = [], [], [], []
    for l in range(len(prm["w16"])):
        mod = mod_g[l]
        proj, proj32 = _in_proj_call(x, mod, prm["norm1_w"][l], prm["w16"][l], prm["w32"][l], tm, tiles_per_cond)
        gates_r = _gate_rows(proj32, nseq, t, min(CHUNK_A, t))
        res_a = _mixer_a_call(proj, gates_r, prm["conv_a"][l], prm["a_log"][l], prm["dt_bias"][l],
                              prm["norm_a"][l], None if emit else past[0], l, nseq, t, emit)
        res_b = _mixer_b_call(proj, proj32, prm["lb_logits"], prm["norm_b"][l], None if emit else past[1],
                              l, nseq, t, emit)
        if emit:
            oc, kn, vn = _attn_ctx_call(proj, prm["q_norm2"][l], prm["k_norm2"][l], prm["sink"][l], nseq, t)
            states_a.append(res_a[1])
            states_b.append(res_b[1])
            keys.append(kn.reshape(nseq, t, C_KVHEADS, C_HD))
            vals.append(vn.reshape(nseq, t, C_KVHEADS, C_HD))
        else:
            oc = _attn_lat_call(proj, past[2], past[3], prm["q_norm2"][l], prm["k_norm2"][l], tables[0], tables[1],
                                prm["sink"][l], l, nseq, t)
        x = _merge_call(x, proj, res_a[0], res_b[0], oc, mod, prm["w_branch"][l], prm["w_out"][l], tm, tiles_per_cond)
        x = _ffn_call(x, mod, prm["norm2_w"][l], prm["w_up"][l], prm["conv_ffn"][l], prm["w_down"][l],
                      tm, tiles_per_cond, t)
    return x.reshape(nseq, t, d), states_a, states_b, keys, vals


def kernel(x_prompt, x_sample, state_delta, state_hgrn, cache_k, cache_v, c, c_ctx, ada_w, ada_b, norm1_w, w_in, conv_a, a_log, dt_bias, norm_a, lb_logits, norm_b, q_norm, k_norm, sink, w_branch, w_out, norm2_w, w_up, conv_ffn, w_down):
    depth = w_in.shape[0]
    d = x_prompt.shape[-1]
    n_lat = c.shape[0]

    cond = jnp.concatenate([c_ctx[None, :], c], axis=0)
    rows = -(-cond.shape[0] // SUBLANES) * SUBLANES
    cond = jnp.pad(cond, ((0, rows - cond.shape[0]), (0, 0)))
    mod_all = _mod_call(cond, ada_w, ada_b).reshape(depth, rows, 6, d)

    perm = [_permute_w_in(w_in[l]) for l in range(depth)]
    prm = dict(
        w16=[p[0] for p in perm], w32=[p[1] for p in perm],
        norm1_w=norm1_w.reshape(depth, 1, d), norm2_w=norm2_w.reshape(depth, 1, d),
        conv_a=conv_a, a_log=a_log, dt_bias=dt_bias, norm_a=norm_a.reshape(depth, 1, HEAD_W),
        lb_logits=lb_logits, norm_b=norm_b.reshape(depth, 1, HEAD_W),
        q_norm2=jnp.tile(q_norm, (1, 2)).reshape(depth, 1, LANES), k_norm2=jnp.tile(k_norm, (1, 2)).reshape(depth, 1, LANES),
        sink=sink, w_branch=w_branch.astype(BF16), w_out=w_out.astype(BF16),
        w_up=w_up.astype(BF16), conv_ffn=conv_ffn, w_down=w_down.astype(BF16))

    y_prompt, st_a, st_b, keys, vals = _group_forward(x_prompt, mod_all[:, 0:1], prm, None, None)

    past_len = cache_k.shape[2]
    past = (state_delta, state_hgrn,
            cache_k.reshape(cache_k.shape[0], depth, past_len, C_KVHEADS * C_HD),
            cache_v.reshape(cache_v.shape[0], depth, past_len, C_KVHEADS * C_HD))
    y_sample, _, _, _, _ = _group_forward(x_sample, mod_all[:, 1:1 + n_lat], prm, past, _rope_tables(x_sample.shape[1]))

    return (y_prompt, y_sample, jnp.stack(st_a, axis=1), jnp.stack(st_b, axis=1),
            jnp.stack(keys, axis=1), jnp.stack(vals, axis=1))
```

```python
import functools

import numpy as np
import jax
import jax.numpy as jnp
from jax import lax
from jax.experimental import pallas as pl
from jax.experimental.pallas import tpu as pltpu

F32 = jnp.float32
BF16 = jnp.bfloat16

D_MODEL = 1024
NORM_EPS = 1e-6
LB_EPS = 1e-6
NEG_BIG = -1e30
GRID_W = 64
ROPE_THETA = 10000.0

HEADS = 4
HEAD_W = 128
SHORT_CONV = 5
C_QHEADS = 8
C_KVHEADS = 2
C_GROUP = C_QHEADS // C_KVHEADS
C_HD = 64
C_WINDOW = 128
C_BLOCK = 128
BRANCH_W = 512
D_FF = 2816

LANES = 128
SUBLANES = 8
BF16_ROWS = 16
VMEM_LIMIT = 56 * 1024 * 1024

COL_MG = 0
COL_QA = 3072
COL_KA = 3584
COL_VA = 4096
COL_GA = 4608
COL_QB = 5120
COL_IB = 5632
COL_GB = 6144
COL_QC = 6656
COL_KC = 7168
COL_VC = 7296
PROJ_W = 7424
COL32_FB = 0
COL32_GATES = 1024
PROJ32_W = 1152
_SRC = dict(qa=0, ka=512, va=1024, ga=1536, beta=2048, alpha=2056, qb=2064, ib=2576, fb=3088, gb=4112,
            qc=4624, kc=5136, vc=5264, mg=5392, end=8464)

CHUNK_A = 256
CHUNK_B = 64
TRI_BASE = 16
TRI_MERGE = 2
MM_TILE = 768


def _cparams(sem):
    return pltpu.CompilerParams(dimension_semantics=sem, vmem_limit_bytes=VMEM_LIMIT)


def _resident(shape, index_map):
    return pl.BlockSpec(shape, index_map, pipeline_mode=pl.Buffered(1))


def _dot(a, b):
    return jnp.dot(a.astype(BF16), b.astype(BF16), preferred_element_type=F32)


def _dot_nt(a, b):
    return lax.dot_general(a.astype(BF16), b.astype(BF16), (((1,), (1,)), ((), ())), preferred_element_type=F32)


def _dot_tn(a, b):
    return lax.dot_general(a.astype(BF16), b.astype(BF16), (((0,), (0,)), ((), ())), preferred_element_type=F32)


def _sigmoid(x):
    return 1.0 / (1.0 + jnp.exp(-x))


def _silu(x):
    return x * _sigmoid(x)


def _softplus(x):
    return jnp.maximum(x, 0.0) + jnp.log(1.0 + jnp.exp(-jnp.abs(x)))


def _log_sigmoid(x):
    return jnp.minimum(x, 0.0) - jnp.log(1.0 + jnp.exp(-jnp.abs(x)))


def _rms_rows(x, w):
    ms = jnp.mean(x * x, axis=-1, keepdims=True)
    return x * lax.rsqrt(ms + NORM_EPS) * w


def _split_bf16(x, n):
    parts, r = [], x
    for _ in range(n):
        p = r.astype(BF16)
        parts.append(p)
        r = r - p.astype(F32)
    return parts


def _col_chunks(width, step):
    return [(lo, min(lo + step, width)) for lo in range(0, width, step)]


def _mod_kernel(c_ref, w_ref, b_ref, o_ref):
    c = c_ref[...]
    o_ref[0] = _dot(_silu(c), w_ref[0]) + b_ref[0]


def _mod_call(cond, ada_w, ada_b):
    depth, d, n = ada_w.shape
    rows = cond.shape[0]
    tn = 768
    return pl.pallas_call(
        _mod_kernel,
        out_shape=jax.ShapeDtypeStruct((depth, rows, n), F32),
        grid=(depth, n // tn),
        in_specs=[pl.BlockSpec((rows, d), lambda l, j: (0, 0)),
                  pl.BlockSpec((1, d, tn), lambda l, j: (l, 0, j)),
                  pl.BlockSpec((1, 1, tn), lambda l, j: (l, 0, j))],
        out_specs=pl.BlockSpec((1, rows, tn), lambda l, j: (l, 0, j)),
        compiler_params=_cparams(("arbitrary", "arbitrary")),
        name="mod",
    )(cond, ada_w, ada_b.reshape(depth, 1, n))


def _in_proj_kernel(x_ref, mod_ref, nw_ref, w16_ref, w32_ref, o16_ref, o32_ref):
    h = _rms_rows(x_ref[...], nw_ref[...]) * (1.0 + mod_ref[0, 1:2, :]) + mod_ref[0, 0:1, :]
    hb = h.astype(BF16)
    for lo, hi in _col_chunks(PROJ_W, MM_TILE):
        o16_ref[:, lo:hi] = jnp.dot(hb, w16_ref[:, lo:hi], preferred_element_type=F32).astype(BF16)
    for lo, hi in _col_chunks(PROJ32_W, MM_TILE):
        o32_ref[:, lo:hi] = jnp.dot(hb, w32_ref[:, lo:hi], preferred_element_type=F32)


def _in_proj_call(x2d, mod, nw, w16, w32, tm, tiles_per_cond):
    m, d = x2d.shape
    return pl.pallas_call(
        _in_proj_kernel,
        out_shape=(jax.ShapeDtypeStruct((m, PROJ_W), BF16), jax.ShapeDtypeStruct((m, PROJ32_W), F32)),
        grid=(m // tm,),
        in_specs=[pl.BlockSpec((tm, d), lambda i: (i, 0)),
                  pl.BlockSpec((1, 6, d), lambda i: (i // tiles_per_cond, 0, 0)),
                  _resident((1, d), lambda i: (0, 0)),
                  _resident((d, PROJ_W), lambda i: (0, 0)),
                  _resident((d, PROJ32_W), lambda i: (0, 0))],
        out_specs=(pl.BlockSpec((tm, PROJ_W), lambda i: (i, 0)),
                   pl.BlockSpec((tm, PROJ32_W), lambda i: (i, 0))),
        compiler_params=_cparams(("arbitrary",)),
        name="in_proj",
    )(x2d, mod, nw, w16, w32)


ROW_BLOCK = 256


def _gated_norm_epilogue(of_scr, ob_scr, gate_ref, nw_ref, o_ref, t):
    nw = nw_ref[...]

    def body(r, carry):
        r0 = pl.multiple_of(r * ROW_BLOCK, ROW_BLOCK)
        o = of_scr[pl.ds(r0, ROW_BLOCK), :] + ob_scr[pl.ds(r0, ROW_BLOCK), :]
        y = _rms_rows(o, nw) * _silu(gate_ref[pl.ds(r0, ROW_BLOCK), :].astype(F32))
        o_ref[pl.ds(r0, ROW_BLOCK), :] = y.astype(o_ref.dtype)
        return carry

    lax.fori_loop(0, t // ROW_BLOCK, body, 0)


def _tri_inverse(mats, ri, ci, c, tick):
    n = range(len(mats))
    shift = int(np.log2(TRI_BASE))
    base = (ri >> shift) == (ci >> shift)
    mb = [jnp.where(base, m, 0.0) for m in mats]
    y = [-m for m in mb]
    p = [_dot(m, m) for m in mb]
    tick()
    for _ in range(shift - 2):
        yp = [_dot(jnp.concatenate([y[i], p[i]], axis=0), p[i]) for i in n]
        tick()
        y = [y[i] + p[i] + yp[i][:c] for i in n]
        p = [r[c:] for r in yp]
    yp = [_dot(y[i], p[i]) for i in n]
    tick()
    y = [y[i] + p[i] + yp[i] for i in n]
    total = int(np.log2(c))
    while shift < total:
        factors = min(TRI_MERGE, total - shift)
        inner = (ri >> shift) == (ci >> shift)
        outer = (ri >> (shift + factors)) == (ci >> (shift + factors))
        between = jnp.logical_and(outer, jnp.logical_not(inner))
        cm = [jnp.where(between, m, 0.0) for m in mats]
        w = [_dot(y[i], cm[i]) for i in n]
        tick()
        w = [cm[i] + w[i] for i in n]
        if factors == 1:
            wy = [_dot(w[i], y[i]) for i in n]
            tick()
            y = [y[i] - w[i] - wy[i] for i in n]
        else:
            r = [_dot(w[i], jnp.concatenate([w[i], y[i]], axis=1)) for i in n]
            tick()
            p, y = [x[:, :c] for x in r], [y[i] - w[i] - r[i][:, c:] for i in n]
            for f in range(1, factors):
                if f + 1 < factors:
                    r = [_dot(p[i], jnp.concatenate([p[i], y[i]], axis=1)) for i in n]
                    tick()
                    p, y = [x[:, :c] for x in r], [y[i] + p[i] + r[i][:, c:] for i in n]
                else:
                    py = [_dot(p[i], y[i]) for i in n]
                    tick()
                    y = [y[i] + p[i] + py[i] for i in n]
        shift += factors
    eye = jnp.where(ri == ci, 1.0, 0.0)
    return [eye + v for v in y]


def _delta_chunks(probs, c, tick):
    n = range(len(probs))
    ri = lax.broadcasted_iota(jnp.int32, (c, c), 0)
    ci = lax.broadcasted_iota(jnp.int32, (c, c), 1)
    eye = ri == ci
    pre = []
    for q, k, v, beta_row, alpha_row, a_neg, dt_b, s, reverse in probs:
        incl, strict = (ri <= ci, ri < ci) if reverse else (ri >= ci, ri > ci)
        beta_r = _sigmoid(beta_row)
        g_r = a_neg * _softplus(alpha_row + dt_b)
        gc_col = jnp.sum(jnp.where(incl, jnp.broadcast_to(g_r, (c, c)), 0.0), axis=1, keepdims=True)
        beta_col = jnp.sum(jnp.where(eye, jnp.broadcast_to(beta_r, (c, c)), 0.0), axis=1, keepdims=True)
        gc_row = jnp.sum(jnp.where(eye, jnp.broadcast_to(gc_col, (c, c)), 0.0), axis=0, keepdims=True)
        g_tot = jnp.sum(g_r, axis=1, keepdims=True)
        decay = jnp.where(incl, jnp.exp(jnp.where(incl, gc_col - gc_row, 0.0)), 0.0)
        pre.append((strict, gc_col, beta_col, g_tot, decay, jnp.exp(gc_col)))

    kb = [pr[1].astype(BF16) for pr in probs]
    kq = [_dot_nt(jnp.concatenate([kb[i], probs[i][0].astype(BF16)], axis=0), kb[i]) for i in n]
    tick()
    t_inv = _tri_inverse([jnp.where(pre[i][0], kq[i][:c] * pre[i][2] * pre[i][4], 0.0) for i in n], ri, ci, c, tick)
    uw = [_dot(t_inv[i], jnp.concatenate([probs[i][2] * pre[i][2], probs[i][1] * (pre[i][2] * pre[i][5])], axis=1))
          for i in n]
    tick()
    wq = [_dot(jnp.concatenate([uw[i][:, HEAD_W:], probs[i][0] * pre[i][5]], axis=0), probs[i][7]) for i in n]
    tick()
    v_new = [uw[i][:, :HEAD_W] - wq[i][:c] for i in n]
    kd_t = [(probs[i][1] * jnp.exp(pre[i][3] - pre[i][1])).T for i in n]
    os_ = [_dot(jnp.concatenate([kq[i][c:] * pre[i][4], kd_t[i]], axis=0), v_new[i]) for i in n]
    tick()
    return [(wq[i][c:] + os_[i][:c], probs[i][7] * jnp.exp(pre[i][3]) + os_[i][c:]) for i in n]


def _conv_silu_pass(x_ref, cw_ref, xp_scr, dst_scr, t, l2, scale):
    nrb = t // ROW_BLOCK
    pad = SUBLANES
    half = SHORT_CONV // 2

    def cp(r, carry):
        r0 = pl.multiple_of(r * ROW_BLOCK, ROW_BLOCK)
        xp_scr[pl.ds(r0 + pad, ROW_BLOCK), :] = x_ref[pl.ds(r0, ROW_BLOCK), :].astype(F32)
        return carry

    lax.fori_loop(0, nrb, cp, 0)
    cw = cw_ref[...]

    def body(r, carry):
        r0 = pl.multiple_of(r * ROW_BLOCK, ROW_BLOCK)
        win = xp_scr[pl.ds(r0, ROW_BLOCK + 2 * pad), :]
        y = None
        for j in range(SHORT_CONV):
            lo = pad - half + j
            term = win[lo:lo + ROW_BLOCK, :] * cw[j:j + 1, :]
            y = term if y is None else y + term
        y = _silu(y)
        if l2:
            y = y * (lax.rsqrt(jnp.sum(y * y, axis=-1, keepdims=True) + NORM_EPS) * scale)
        dst_scr[pl.ds(r0, ROW_BLOCK), :] = y
        return carry

    lax.fori_loop(0, nrb, body, 0)


def _hgrn_diag(q, kf, v, b, reverse, c):
    nb = c // SUBLANES
    q3, k3, v3, b3 = (a.reshape(nb, SUBLANES, HEAD_W) for a in (q, kf, v, b))
    sub = lax.broadcasted_iota(jnp.int32, (nb, SUBLANES, HEAD_W), 1)
    o3 = jnp.zeros((nb, SUBLANES, HEAD_W), F32)
    for j in range(SUBLANES):
        mask = (sub <= j) if reverse else (sub >= j)
        e = jnp.exp(jnp.where(mask, b3 - b3[:, j:j + 1, :], 0.0))
        a = jnp.sum(jnp.where(mask, q3 * e * k3[:, j:j + 1, :], 0.0), axis=-1, keepdims=True)
        o3 = o3 + a * v3[:, j:j + 1, :]
    return o3.reshape(c, HEAD_W)


def _hgrn_chunk(q, kf, v, lf, st, reverse, c):
    ri = lax.broadcasted_iota(jnp.int32, (c, c), 0)
    ci = lax.broadcasted_iota(jnp.int32, (c, c), 1)
    incl = (ri <= ci) if reverse else (ri >= ci)
    parts = jnp.concatenate(_split_bf16(lf, 3), axis=1)
    b3 = jnp.dot(jnp.where(incl, 1.0, 0.0).astype(BF16), parts, preferred_element_type=F32)
    b = b3[:, :HEAD_W] + b3[:, HEAD_W:2 * HEAD_W] + b3[:, 2 * HEAD_W:]
    b_tot = jnp.sum(lf, axis=0, keepdims=True)
    o = _dot_nt(q * jnp.exp(b), st)
    row = lax.broadcasted_iota(jnp.int32, (c, 1), 0)
    att = jnp.zeros((c, c), F32)
    n = SUBLANES
    while n < c:
        pieces = []
        for g in range(c // (2 * n)):
            r = g * 2 * n + (n if reverse else n - 1)
            pieces.append(jnp.broadcast_to(b[r:r + 1, :], (2 * n, HEAD_W)))
        ref = pieces[0] if len(pieces) == 1 else jnp.concatenate(pieces, axis=0)
        e = jnp.exp(-jnp.abs(b - ref))
        s = int(np.log2(n))
        second = ((row >> s) & 1) == 1
        q_part = jnp.logical_not(second) if reverse else second
        qt = jnp.where(q_part, q * e, 0.0)
        kt = jnp.where(q_part, 0.0, kf * e)
        att = att + jnp.where((ri >> (s + 1)) == (ci >> (s + 1)), _dot_nt(qt, kt), 0.0)
        n *= 2
    o = o + _dot(att, v) + _hgrn_diag(q, kf, v, b, reverse, c)
    kh = kf * jnp.exp(b_tot - b)
    st_new = st * jnp.exp(b_tot) + _dot_tn(v, kh)
    return o, st_new


def _mixer_ab_kernel(*refs, t, ca, cb, layer, has_past, emit_state):
    refs = list(refs)
    (qa_ref, ka_ref, va_ref, ga_ref, gr_ref, cwq_ref, cwk_ref, cwv_ref, alog_ref, dtb_ref, nwa_ref,
     qb_ref, ib_ref, f0_ref, f1_ref, gb_ref, lb_ref, nwb_ref) = refs[:18]
    pos = 18
    sa0_ref = sb0_ref = None
    if has_past:
        sa0_ref, sb0_ref = refs[pos:pos + 2]
        pos += 2
    oa_ref, ob_ref = refs[pos:pos + 2]
    pos += 2
    sfa_ref = sfb_ref = None
    if emit_state:
        sfa_ref, sfb_ref = refs[pos:pos + 2]
        pos += 2
    xp_scr, qn_scr, kn_scr, vn_scr, af_scr, ab_scr, bf_scr, bb_scr, sa_scr, sb_scr = refs[pos:]

    h = pl.program_id(1)
    pad = SUBLANES
    xp_scr[0:pad, :] = jnp.zeros((pad, HEAD_W), F32)
    xp_scr[t + pad:t + 2 * pad, :] = jnp.zeros((pad, HEAD_W), F32)
    _conv_silu_pass(qa_ref, cwq_ref, xp_scr, qn_scr, t, True, HEAD_W ** -0.5)
    _conv_silu_pass(ka_ref, cwk_ref, xp_scr, kn_scr, t, True, 1.0)
    _conv_silu_pass(va_ref, cwv_ref, xp_scr, vn_scr, t, False, 1.0)

    if has_past:
        for d in range(2):
            sa_scr[d] = sa0_ref[0, 0, d, 0]
            sb_scr[d] = sb0_ref[0, 0, d, 0].T
    else:
        sa_scr[...] = jnp.zeros(sa_scr.shape, F32)
        sb_scr[...] = jnp.zeros(sb_scr.shape, F32)

    ones = jnp.ones((1, ca), F32)
    a_neg = [-jnp.exp(ones * alog_ref[d, h]) for d in range(2)]
    dt_b = [dtb_ref[d, h] for d in range(2)]

    lb_terms = None
    if layer > 0:
        lb_terms = []
        for d in range(2):
            lg = lb_ref[d]
            ex = jnp.exp(lg - jnp.max(lg, axis=0, keepdims=True))
            pr = ex / jnp.sum(ex, axis=0, keepdims=True)
            lb = jnp.clip(jnp.sum(pr[1:layer + 1], axis=0, keepdims=True), LB_EPS, 1.0 - LB_EPS)
            lb_terms.append((jnp.log(lb), jnp.log1p(-lb), 1.0 - lb))

    f_refs = (f0_ref, f1_ref)
    nca, ncb = t // ca, t // cb
    ratio = ca // cb

    def hgrn_gates(z, d):
        if layer == 0:
            return _log_sigmoid(z), _sigmoid(-z)
        log_lb, log1m_lb, one_m_lb = lb_terms[d]
        a2 = log1m_lb + _log_sigmoid(z)
        return jnp.maximum(log_lb, a2) + jnp.log(1.0 + jnp.exp(-jnp.abs(log_lb - a2))), one_m_lb * _sigmoid(-z)

    def step(n, carry):
        a_rows = [pl.multiple_of((n if d == 0 else nca - 1 - n) * ca, ca) for d in range(2)]
        a_in = []
        for d in range(2):
            gr = gr_ref[0, 0, n if d == 0 else nca - 1 - n]
            a_in.append((qn_scr[pl.ds(a_rows[d], ca), :], kn_scr[pl.ds(a_rows[d], ca), :],
                         vn_scr[pl.ds(a_rows[d], ca), :], gr[d:d + 1, :], gr[2 + d:3 + d, :], sa_scr[d]))
        b_rows, b_in = [], []
        for j in range(ratio):
            m = n * ratio + j
            rows = [pl.multiple_of((m if d == 0 else ncb - 1 - m) * cb, cb) for d in range(2)]
            b_rows.append(rows)
            b_in.append([(f_refs[d][pl.ds(rows[d], cb), :], qb_ref[pl.ds(rows[d], cb), :],
                          ib_ref[pl.ds(rows[d], cb), :]) for d in range(2)])
        sb = [sb_scr[d] for d in range(2)]

        b_out = [[None, None] for _ in range(ratio)]
        pending = [(j, d) for j in range(ratio) for d in range(2)]

        def tick():
            if pending:
                j, d = pending.pop(0)
                z, qraw, iraw = b_in[j][d]
                lf, kf = hgrn_gates(z, d)
                b_out[j][d], sb[d] = _hgrn_chunk(_silu(qraw.astype(F32)), kf, iraw.astype(F32), lf, sb[d],
                                                 d == 1, cb)

        a_out = _delta_chunks([(*a_in[d][:5], a_neg[d], dt_b[d], a_in[d][5], d == 1) for d in range(2)], ca, tick)
        while pending:
            tick()

        for d in range(2):
            (af_scr if d == 0 else ab_scr)[pl.ds(a_rows[d], ca), :] = a_out[d][0]
            sa_scr[d] = a_out[d][1]
            sb_scr[d] = sb[d]
            for j in range(ratio):
                (bf_scr if d == 0 else bb_scr)[pl.ds(b_rows[j][d], cb), :] = b_out[j][d]
        return carry

    lax.fori_loop(0, nca, step, 0)

    _gated_norm_epilogue(af_scr, ab_scr, ga_ref, nwa_ref, oa_ref, t)
    _gated_norm_epilogue(bf_scr, bb_scr, gb_ref, nwb_ref, ob_ref, t)
    if emit_state:
        for d in range(2):
            sfa_ref[0, d, 0] = sa_scr[d]
            sfb_ref[0, d, 0] = sb_scr[d].T


def _mixer_ab_call(proj, proj32, gates_r, prm, past, layer, nseq, t, emit_state):
    ca, cb = min(CHUNK_A, t), min(CHUNK_B, t)
    depth = prm["lb_logits"].shape[1]
    col = lambda off: (lambda b, h: (b, off // HEAD_W + h))
    seq_in = lambda off: pl.BlockSpec((t, HEAD_W), col(off))
    conv = lambda part: pl.BlockSpec((SHORT_CONV, HEAD_W), lambda b, h: (0, part * HEADS + h))
    smem = pl.BlockSpec(memory_space=pltpu.SMEM)
    norm = pl.BlockSpec((1, HEAD_W), lambda b, h: (0, 0))
    in_specs = [seq_in(COL_QA), seq_in(COL_KA), seq_in(COL_VA), seq_in(COL_GA),
                pl.BlockSpec((1, 1, t // ca, 4, ca), lambda b, h: (b, h, 0, 0, 0)),
                conv(0), conv(1), conv(2), smem, smem, norm,
                seq_in(COL_QB), seq_in(COL_IB), seq_in(COL32_FB), seq_in(COL32_FB + HEADS * HEAD_W), seq_in(COL_GB),
                pl.BlockSpec((2, depth, HEAD_W), lambda b, h: (0, 0, h)), norm]
    args = [proj, proj, proj, proj, gates_r, prm["conv_a"][layer], prm["conv_a"][layer], prm["conv_a"][layer],
            prm["a_log"][layer], prm["dt_bias"][layer], prm["norm_a"][layer],
            proj, proj, proj32, proj32, proj, prm["lb_logits"], prm["norm_b"][layer]]
    state_in = pl.BlockSpec((1, 1, 2, 1, HEAD_W, HEAD_W), lambda b, h: (b, layer, 0, h, 0, 0))
    if past is not None:
        in_specs += [state_in, state_in]
        args += [past[0], past[1]]
    o_shape = jax.ShapeDtypeStruct((nseq * t, HEADS * HEAD_W), BF16)
    o_spec = pl.BlockSpec((t, HEAD_W), lambda b, h: (b, h))
    out_shape, out_specs = [o_shape, o_shape], [o_spec, o_spec]
    if emit_state:
        s_shape = jax.ShapeDtypeStruct((nseq, 2, HEADS, HEAD_W, HEAD_W), F32)
        s_spec = pl.BlockSpec((1, 2, 1, HEAD_W, HEAD_W), lambda b, h: (b, 0, h, 0, 0))
        out_shape += [s_shape, s_shape]
        out_specs += [s_spec, s_spec]
    seq = pltpu.VMEM((t, HEAD_W), F32)
    state = pltpu.VMEM((2, HEAD_W, HEAD_W), F32)
    return pl.pallas_call(
        functools.partial(_mixer_ab_kernel, t=t, ca=ca, cb=cb, layer=layer, has_past=past is not None,
                          emit_state=emit_state),
        out_shape=tuple(out_shape), grid=(nseq, HEADS), in_specs=in_specs, out_specs=tuple(out_specs),
        scratch_shapes=[pltpu.VMEM((t + 2 * SUBLANES, HEAD_W), F32), seq, seq, seq, seq, seq, seq, seq, state, state],
        compiler_params=_cparams(("arbitrary", "arbitrary")),
        name="mixer_ab",
    )(*args)


def _rms_head_pairs(x, w2):
    lane = lax.broadcasted_iota(jnp.int32, x.shape, 1)
    left = lane < C_HD
    sq = x * x
    s0 = jnp.sum(jnp.where(left, sq, 0.0), axis=-1, keepdims=True)
    s1 = jnp.sum(jnp.where(left, 0.0, sq), axis=-1, keepdims=True)
    ms = jnp.where(left, s0, s1) * (1.0 / C_HD)
    return x * lax.rsqrt(ms + NORM_EPS) * w2


def _rope_pairs(x, cos2, sin2):
    lane = lax.broadcasted_iota(jnp.int32, x.shape, 1)
    quarter = C_HD // 4
    swapped = jnp.where((lane & (2 * quarter - 1)) < quarter,
                        pltpu.roll(x, LANES - quarter, axis=1), pltpu.roll(x, quarter, axis=1))
    return x * cos2 + swapped * sin2


def _softmax_sink_av(scores, values, sink):
    m = sink
    for s in scores:
        m = jnp.maximum(m, jnp.max(s, axis=-1, keepdims=True))
    den = jnp.exp(sink - m)
    acc = None
    for s, v in zip(scores, values):
        p = jnp.exp(s - m)
        den = den + jnp.sum(p, axis=-1, keepdims=True)
        t = jnp.dot(p.astype(BF16), v, preferred_element_type=F32)
        acc = t if acc is None else acc + t
    return acc / den


def _attn_ctx_kernel(q_ref, k_ref, v_ref, qn_ref, kn_ref, sink_ref, o_ref, ko_ref, vo_ref, *, t):
    qw, kw = qn_ref[...], kn_ref[...]
    kn = _rms_head_pairs(k_ref[...].astype(F32), kw)
    ko_ref[0] = kn
    v = v_ref[...]
    vo_ref[0] = v.astype(F32)
    knb, vb = kn.astype(BF16), v
    scale = C_HD ** -0.5
    for pair in range(C_QHEADS // 2):
        qp = (_rms_head_pairs(q_ref[:, pair * LANES:(pair + 1) * LANES].astype(F32), qw) * scale).astype(BF16)
        outs = []
        for half in range(2):
            hq = 2 * pair + half
            hk = hq // C_GROUP
            qh = qp[:, half * C_HD:(half + 1) * C_HD]
            s = _dot_nt(qh, knb[:, hk * C_HD:(hk + 1) * C_HD])
            sink = jnp.full((1, 1), sink_ref[hq], F32)
            outs.append(_softmax_sink_av([s], [vb[:, hk * C_HD:(hk + 1) * C_HD]], sink))
        o_ref[:, pair * LANES:(pair + 1) * LANES] = jnp.concatenate(outs, axis=1).astype(o_ref.dtype)


def _attn_ctx_call(proj, q_norm2, k_norm2, sink, nseq, t):
    return pl.pallas_call(
        functools.partial(_attn_ctx_kernel, t=t),
        out_shape=(jax.ShapeDtypeStruct((nseq * t, C_QHEADS * C_HD), BF16),
                   jax.ShapeDtypeStruct((nseq, t, LANES), F32), jax.ShapeDtypeStruct((nseq, t, LANES), F32)),
        grid=(nseq,),
        in_specs=[pl.BlockSpec((t, C_QHEADS * C_HD), lambda b: (b, COL_QC // (C_QHEADS * C_HD))),
                  pl.BlockSpec((t, LANES), lambda b: (b, COL_KC // LANES)),
                  pl.BlockSpec((t, LANES), lambda b: (b, COL_VC // LANES)),
                  pl.BlockSpec((1, LANES), lambda b: (0, 0)), pl.BlockSpec((1, LANES), lambda b: (0, 0)),
                  pl.BlockSpec(memory_space=pltpu.SMEM)],
        out_specs=(pl.BlockSpec((t, C_QHEADS * C_HD), lambda b: (b, 0)),
                   pl.BlockSpec((1, t, LANES), lambda b: (b, 0, 0)), pl.BlockSpec((1, t, LANES), lambda b: (b, 0, 0))),
        compiler_params=_cparams(("arbitrary",)),
        name="attn_ctx",
    )(proj, proj, proj, q_norm2, k_norm2, sink)


def _attn_lat_kernel(q_ref, k_ref, v_ref, kc_ref, vc_ref, qn_ref, kn_ref, cos_ref, sin_ref, sink_ref, o_ref,
                     qs_scr, ks_scr, vs_scr, bias_scr, *, t, past_len):
    qw, kw = qn_ref[...], kn_ref[...]
    scale = C_HD ** -0.5
    nrb = t // ROW_BLOCK
    blk = C_BLOCK

    ks_scr[0:blk, :] = jnp.zeros((blk, LANES), BF16)
    vs_scr[0:blk, :] = jnp.zeros((blk, LANES), BF16)
    ks_scr[t + blk:t + 2 * blk, :] = jnp.zeros((blk, LANES), BF16)
    vs_scr[t + blk:t + 2 * blk, :] = jnp.zeros((blk, LANES), BF16)

    grp_rows = C_GROUP * blk

    def prep(r, carry):
        r0 = pl.multiple_of(r * ROW_BLOCK, ROW_BLOCK)
        cos2, sin2 = cos_ref[pl.ds(r0, ROW_BLOCK), :], sin_ref[pl.ds(r0, ROW_BLOCK), :]
        kn = _rope_pairs(_rms_head_pairs(k_ref[pl.ds(r0, ROW_BLOCK), :].astype(F32), kw), cos2, sin2)
        ks_scr[pl.ds(r0 + blk, ROW_BLOCK), :] = kn.astype(BF16)
        vs_scr[pl.ds(r0 + blk, ROW_BLOCK), :] = v_ref[pl.ds(r0, ROW_BLOCK), :]
        for pair in range(C_QHEADS // 2):
            qp = _rms_head_pairs(q_ref[pl.ds(r0, ROW_BLOCK), pair * LANES:(pair + 1) * LANES].astype(F32), qw)
            qp = (_rope_pairs(qp, cos2, sin2) * scale).astype(BF16)
            for half in range(2):
                hq = 2 * pair + half
                hk, g = hq // C_GROUP, hq % C_GROUP
                for sub in range(ROW_BLOCK // blk):
                    dst = pl.multiple_of((r * (ROW_BLOCK // blk) + sub) * grp_rows + g * blk, blk)
                    qs_scr[hk, pl.ds(dst, blk), :] = qp[sub * blk:(sub + 1) * blk, half * C_HD:(half + 1) * C_HD]
        return carry

    lax.fori_loop(0, nrb, prep, 0)

    nkeys = 3 * blk + past_len
    kcb = kc_ref[0, 0].astype(BF16)
    vcb = vc_ref[0, 0].astype(BF16)
    qi = lax.broadcasted_iota(jnp.int32, (grp_rows, nkeys), 0) & (blk - 1)
    kj = lax.broadcasted_iota(jnp.int32, (grp_rows, nkeys), 1)
    visible = jnp.logical_or(kj >= 3 * blk, jnp.logical_and(kj >= qi, kj <= qi + 2 * C_WINDOW))
    bias_scr[...] = jnp.where(visible, 0.0, NEG_BIG)
    head_of_row = lax.broadcasted_iota(jnp.int32, (grp_rows, 1), 0) // blk
    kcol = lax.broadcasted_iota(jnp.int32, (1, nkeys), 1)

    def qblock(n, carry):
        r0 = pl.multiple_of(n * blk, blk)
        kpos = kcol + (r0 - blk)
        in_seq = jnp.logical_or(kcol >= 3 * blk, jnp.logical_and(kpos >= 0, kpos < t))
        edge = jnp.where(in_seq, 0.0, NEG_BIG)
        kwin = ks_scr[pl.ds(r0, 3 * blk), :]
        vwin = vs_scr[pl.ds(r0, 3 * blk), :]
        outs = []
        for hk in range(C_KVHEADS):
            lo, hi = hk * C_HD, (hk + 1) * C_HD
            keys = jnp.concatenate([kwin[:, lo:hi], kcb[:, lo:hi]], axis=0)
            vals = jnp.concatenate([vwin[:, lo:hi], vcb[:, lo:hi]], axis=0)
            q_stack = qs_scr[hk, pl.ds(pl.multiple_of(n * grp_rows, grp_rows), grp_rows), :]
            s = _dot_nt(q_stack, keys) + bias_scr[...] + edge
            sink = jnp.full((grp_rows, 1), sink_ref[hk * C_GROUP], F32)
            for g in range(1, C_GROUP):
                sink = jnp.where(head_of_row == g, sink_ref[hk * C_GROUP + g], sink)
            o = _softmax_sink_av([s], [vals], sink)
            outs += [o[g * blk:(g + 1) * blk] for g in range(C_GROUP)]
        for pair in range(C_QHEADS // 2):
            o_ref[pl.ds(r0, blk), pair * LANES:(pair + 1) * LANES] = jnp.concatenate(
                outs[2 * pair:2 * pair + 2], axis=1).astype(o_ref.dtype)
        return carry

    lax.fori_loop(0, t // blk, qblock, 0)


def _attn_lat_call(proj, cache_k, cache_v, q_norm2, k_norm2, cos2, sin2, sink, layer, nseq, t):
    past_len = cache_k.shape[2]
    qw = C_QHEADS * C_HD
    return pl.pallas_call(
        functools.partial(_attn_lat_kernel, t=t, past_len=past_len),
        out_shape=jax.ShapeDtypeStruct((nseq * t, qw), BF16),
        grid=(nseq,),
        in_specs=[pl.BlockSpec((t, qw), lambda b: (b, COL_QC // qw)),
                  pl.BlockSpec((t, LANES), lambda b: (b, COL_KC // LANES)),
                  pl.BlockSpec((t, LANES), lambda b: (b, COL_VC // LANES)),
                  pl.BlockSpec((1, 1, past_len, LANES), lambda b: (b, layer, 0, 0)),
                  pl.BlockSpec((1, 1, past_len, LANES), lambda b: (b, layer, 0, 0)),
                  pl.BlockSpec((1, LANES), lambda b: (0, 0)), pl.BlockSpec((1, LANES), lambda b: (0, 0)),
                  _resident((t, LANES), lambda b: (0, 0)), _resident((t, LANES), lambda b: (0, 0)),
                  pl.BlockSpec(memory_space=pltpu.SMEM)],
        out_specs=pl.BlockSpec((t, qw), lambda b: (b, 0)),
        scratch_shapes=[pltpu.VMEM((C_KVHEADS, t * C_GROUP, C_HD), BF16),
                        pltpu.VMEM((t + 2 * C_BLOCK, LANES), BF16), pltpu.VMEM((t + 2 * C_BLOCK, LANES), BF16),
                        pltpu.VMEM((C_GROUP * C_BLOCK, 3 * C_BLOCK + past_len), F32)],
        compiler_params=_cparams(("arbitrary",)),
        name="attn_lat",
    )(proj, proj, proj, cache_k, cache_v, q_norm2, k_norm2, cos2, sin2, sink)


def _rope_tables(t):
    rows = t // GRID_W
    row = jnp.repeat(jnp.arange(rows, dtype=F32), GRID_W)
    col = jnp.tile(jnp.arange(GRID_W, dtype=F32), rows)
    nf = C_HD // 4
    inv = ROPE_THETA ** (-jnp.arange(nf, dtype=F32) / nf)
    ar, ac = row[:, None] * inv, col[:, None] * inv
    cos = jnp.concatenate([jnp.cos(ar), jnp.cos(ar), jnp.cos(ac), jnp.cos(ac)], axis=1)
    sin = jnp.concatenate([-jnp.sin(ar), jnp.sin(ar), -jnp.sin(ac), jnp.sin(ac)], axis=1)
    return jnp.tile(cos, (1, 2)), jnp.tile(sin, (1, 2))


def _merge_kernel(x_ref, g0_ref, g1_ref, g2_ref, oa_ref, ob_ref, oc_ref, mod_ref, wbr_ref, wout_ref, o_ref):
    merged = (_sigmoid(g0_ref[...].astype(F32)) * jnp.dot(oa_ref[...], wbr_ref[0], preferred_element_type=F32)
              + _sigmoid(g1_ref[...].astype(F32)) * jnp.dot(ob_ref[...], wbr_ref[1], preferred_element_type=F32)
              + _sigmoid(g2_ref[...].astype(F32)) * jnp.dot(oc_ref[...], wbr_ref[2], preferred_element_type=F32))
    res = jnp.dot(merged.astype(BF16), wout_ref[...], preferred_element_type=F32)
    o_ref[...] = x_ref[...] + mod_ref[0, 2:3, :] * res


def _merge_call(x2d, proj, oa, ob, oc, mod, w_br, w_out, tm, tiles_per_cond):
    m, d = x2d.shape
    mg = lambda r: pl.BlockSpec((tm, d), lambda i: (i, COL_MG // d + r))
    br = pl.BlockSpec((tm, BRANCH_W), lambda i: (i, 0))
    return pl.pallas_call(
        _merge_kernel,
        out_shape=jax.ShapeDtypeStruct((m, d), F32),
        grid=(m // tm,),
        in_specs=[pl.BlockSpec((tm, d), lambda i: (i, 0)), mg(0), mg(1), mg(2), br, br, br,
                  pl.BlockSpec((1, 6, d), lambda i: (i // tiles_per_cond, 0, 0)),
                  _resident((3, BRANCH_W, d), lambda i: (0, 0, 0)),
                  _resident((d, d), lambda i: (0, 0))],
        out_specs=pl.BlockSpec((tm, d), lambda i: (i, 0)),
        compiler_params=_cparams(("arbitrary",)),
        name="merge",
    )(x2d, proj, proj, proj, oa, ob, oc, mod, w_br, w_out)


FF_CHUNK = 256
HALO = BF16_ROWS


def _ffn_kernel(x_ref, xp_ref, xn_ref, mod_ref, nw_ref, wup_ref, cw_ref, wd_ref, o_ref, h_scr, act_scr, *,
                tm, seq_len):
    i = pl.program_id(0)
    nseg = max(1, tm // seq_len)
    seg = tm // nseg
    nw, sh, sc = nw_ref[...], mod_ref[0, 3:4, :], mod_ref[0, 4:5, :]

    def norm(x):
        return _rms_rows(x, nw) * (1.0 + sc) + sh

    has_prev = ((i * tm) & (seq_len - 1)) != 0
    has_next = (((i + 1) * tm) & (seq_len - 1)) != 0
    zero_halo = jnp.zeros((HALO, x_ref.shape[1]), BF16)
    for s in range(nseg):
        h_scr[s, HALO:HALO + seg, :] = norm(x_ref[s * seg:(s + 1) * seg, :]).astype(BF16)
        if s == 0:
            h_scr[s, 0:HALO, :] = (norm(xp_ref[...]) * jnp.where(has_prev, 1.0, 0.0)).astype(BF16)
        else:
            h_scr[s, 0:HALO, :] = zero_halo
        if s == nseg - 1:
            h_scr[s, HALO + seg:2 * HALO + seg, :] = (norm(xn_ref[...]) * jnp.where(has_next, 1.0, 0.0)).astype(BF16)
        else:
            h_scr[s, HALO + seg:2 * HALO + seg, :] = zero_halo

    def conv(u, cw):
        return (u[HALO - 1:HALO - 1 + seg] * cw[0:1, :] + u[HALO:HALO + seg] * cw[1:2, :]
                + u[HALO + 1:HALO + 1 + seg] * cw[2:3, :])

    for lo, hi in _col_chunks(D_FF, FF_CHUNK):
        for s in range(nseg):
            h = h_scr[s]
            a = conv(jnp.dot(h, wup_ref[:, lo:hi], preferred_element_type=F32), cw_ref[:, lo:hi])
            u = conv(jnp.dot(h, wup_ref[:, D_FF + lo:D_FF + hi], preferred_element_type=F32),
                     cw_ref[:, D_FF + lo:D_FF + hi])
            act_scr[s * seg:(s + 1) * seg, lo:hi] = (_silu(a) * u).astype(BF16)

    o_ref[...] = x_ref[...] + mod_ref[0, 5:6, :] * jnp.dot(act_scr[...], wd_ref[...], preferred_element_type=F32)


def _ffn_call(x2d, mod, nw, w_up, conv_w, w_down, tm, tiles_per_cond, seq_len):
    m, d = x2d.shape
    hb = tm // HALO
    last = m // HALO - 1
    nseg = max(1, tm // seq_len)
    seg = tm // nseg
    return pl.pallas_call(
        functools.partial(_ffn_kernel, tm=tm, seq_len=seq_len),
        out_shape=jax.ShapeDtypeStruct((m, d), F32),
        grid=(m // tm,),
        in_specs=[pl.BlockSpec((tm, d), lambda i: (i, 0)),
                  pl.BlockSpec((HALO, d), lambda i: (jnp.maximum(i * hb - 1, 0), 0)),
                  pl.BlockSpec((HALO, d), lambda i: (jnp.minimum((i + 1) * hb, last), 0)),
                  pl.BlockSpec((1, 6, d), lambda i: (i // tiles_per_cond, 0, 0)),
                  _resident((1, d), lambda i: (0, 0)),
                  _resident((d, 2 * D_FF), lambda i: (0, 0)),
                  _resident((3, 2 * D_FF), lambda i: (0, 0)),
                  _resident((D_FF, d), lambda i: (0, 0))],
        out_specs=pl.BlockSpec((tm, d), lambda i: (i, 0)),
        scratch_shapes=[pltpu.VMEM((nseg, seg + 2 * HALO, d), BF16), pltpu.VMEM((tm, D_FF), BF16)],
        compiler_params=_cparams(("arbitrary",)),
        name="ffn",
    )(x2d, x2d, x2d, mod, nw, w_up, conv_w, w_down)


def _permute_w_in(w):
    s = _SRC
    w16 = jnp.concatenate([w[:, s["mg"]:s["end"]], w[:, s["qa"]:s["beta"]], w[:, s["qb"]:s["fb"]],
                           w[:, s["gb"]:s["mg"]]], axis=1)
    n_gate = s["qb"] - s["beta"]
    w32 = jnp.concatenate([w[:, s["fb"]:s["gb"]], jnp.pad(w[:, s["beta"]:s["qb"]], ((0, 0), (0, LANES - n_gate)))],
                          axis=1)
    return w16.astype(BF16), w32.astype(BF16)


def _gate_rows(proj32, nseq, t, c):
    g = proj32[:, COL32_GATES:COL32_GATES + 4 * HEADS].reshape(nseq, t // c, c, 2, 2, HEADS)
    return jnp.transpose(g, (0, 5, 1, 3, 4, 2)).reshape(nseq, HEADS, t // c, 4, c)


def _row_tile(rows, t):
    tm = 512
    while rows % tm or (t % tm and tm % t):
        tm //= 2
    return tm


def _group_forward(x3d, mod_g, prm, past, tables):
    nseq, t, d = x3d.shape
    x = x3d.reshape(nseq * t, d)
    tm = _row_tile(nseq * t, t)
    tiles_per_cond = (nseq * t) // tm if mod_g.shape[1] == 1 else t // tm
    emit = past is None
    states_a, states_b, keys, vals = [], [], [], []
    for l in range(len(prm["w16"])):
        mod = mod_g[l]
        proj, proj32 = _in_proj_call(x, mod, prm["norm1_w"][l], prm["w16"][l], prm["w32"][l], tm, tiles_per_cond)
        gates_r = _gate_rows(proj32, nseq, t, min(CHUNK_A, t))
        res_ab = _mixer_ab_call(proj, proj32, gates_r, prm, past, l, nseq, t, emit)
        if emit:
            oc, kn, vn = _attn_ctx_call(proj, prm["q_norm2"][l], prm["k_norm2"][l], prm["sink"][l], nseq, t)
            states_a.append(res_ab[2])
            states_b.append(res_ab[3])
            keys.append(kn.reshape(nseq, t, C_KVHEADS, C_HD))
            vals.append(vn.reshape(nseq, t, C_KVHEADS, C_HD))
        else:
            oc = _attn_lat_call(proj, past[2], past[3], prm["q_norm2"][l], prm["k_norm2"][l], tables[0], tables[1],
                                prm["sink"][l], l, nseq, t)
        x = _merge_call(x, proj, res_ab[0], res_ab[1], oc, mod, prm["w_branch"][l], prm["w_out"][l], tm, tiles_per_cond)
        x = _ffn_call(x, mod, prm["norm2_w"][l], prm["w_up"][l], prm["conv_ffn"][l], prm["w_down"][l],
                      tm, tiles_per_cond, t)
    return x.reshape(nseq, t, d), states_a, states_b, keys, vals


def kernel(x_prompt, x_sample, state_delta, state_hgrn, cache_k, cache_v, c, c_ctx, ada_w, ada_b, norm1_w, w_in, conv_a, a_log, dt_bias, norm_a, lb_logits, norm_b, q_norm, k_norm, sink, w_branch, w_out, norm2_w, w_up, conv_ffn, w_down):
    depth = w_in.shape[0]
    d = x_prompt.shape[-1]
    n_lat = c.shape[0]

    cond = jnp.concatenate([c_ctx[None, :], c], axis=0)
    rows = -(-cond.shape[0] // SUBLANES) * SUBLANES
    cond = jnp.pad(cond, ((0, rows - cond.shape[0]), (0, 0)))
    mod_all = _mod_call(cond, ada_w, ada_b).reshape(depth, rows, 6, d)

    perm = [_permute_w_in(w_in[l]) for l in range(depth)]
    prm = dict(
        w16=[p[0] for p in perm], w32=[p[1] for p in perm],
        norm1_w=norm1_w.reshape(depth, 1, d), norm2_w=norm2_w.reshape(depth, 1, d),
        conv_a=conv_a, a_log=a_log, dt_bias=dt_bias, norm_a=norm_a.reshape(depth, 1, HEAD_W),
        lb_logits=lb_logits, norm_b=norm_b.reshape(depth, 1, HEAD_W),
        q_norm2=jnp.tile(q_norm, (1, 2)).reshape(depth, 1, LANES), k_norm2=jnp.tile(k_norm, (1, 2)).reshape(depth, 1, LANES),
        sink=sink, w_branch=w_branch.astype(BF16), w_out=w_out.astype(BF16),
        w_up=w_up.astype(BF16), conv_ffn=conv_ffn, w_down=w_down.astype(BF16))

    y_prompt, st_a, st_b, keys, vals = _group_forward(x_prompt, mod_all[:, 0:1], prm, None, None)

    past_len = cache_k.shape[2]
    past = (state_delta, state_hgrn,
            cache_k.reshape(cache_k.shape[0], depth, past_len, C_KVHEADS * C_HD),
            cache_v.reshape(cache_v.shape[0], depth, past_len, C_KVHEADS * C_HD))
    y_sample, _, _, _, _ = _group_forward(x_sample, mod_all[:, 1:1 + n_lat], prm, past, _rope_tables(x_sample.shape[1]))

    return (y_prompt, y_sample, jnp.stack(st_a, axis=1), jnp.stack(st_b, axis=1),
            jnp.stack(keys, axis=1), jnp.stack(vals, axis=1))
```

```python
import functools
import math

import numpy as np
import jax
import jax.numpy as jnp
from jax import lax
from jax.experimental import pallas as pl
from jax.experimental.pallas import tpu as pltpu

F32 = jnp.float32
BF16 = jnp.bfloat16

D_MODEL = 1024
NORM_EPS = 1e-6
LB_EPS = 1e-6
NEG_BIG = -1e30
GRID_W = 64
ROPE_THETA = 10000.0

HEADS = 4
HEAD_W = 128
SHORT_CONV = 5
C_QHEADS = 8
C_KVHEADS = 2
C_GROUP = C_QHEADS // C_KVHEADS
C_HD = 64
C_WINDOW = 128
C_BLOCK = 128
BRANCH_W = 512
D_FF = 2816

LANES = 128
SUBLANES = 8
BF16_ROWS = 16
VMEM_LIMIT = 56 * 1024 * 1024

COL_MG = 0
COL_QA = 3072
COL_KA = 3584
COL_VA = 4096
COL_GA = 4608
COL_QB = 5120
COL_IB = 5632
COL_GB = 6144
COL_QC = 6656
COL_KC = 7168
COL_VC = 7296
PROJ_W = 7424
COL32_FB = 0
COL32_GATES = 1024
PROJ32_W = 1152
_SRC = dict(qa=0, ka=512, va=1024, ga=1536, beta=2048, alpha=2056, qb=2064, ib=2576, fb=3088, gb=4112,
            qc=4624, kc=5136, vc=5264, mg=5392, end=8464)

CHUNK_A = 128
ROUNDS_A = 4
CHUNK_B = 64
TRI_BASE = 16
TRI_MERGE = 2
MM_TILE = 768


def _cparams(sem):
    return pltpu.CompilerParams(dimension_semantics=sem, vmem_limit_bytes=VMEM_LIMIT)


def _resident(shape, index_map):
    return pl.BlockSpec(shape, index_map, pipeline_mode=pl.Buffered(1))


def _dot(a, b):
    return jnp.dot(a.astype(BF16), b.astype(BF16), preferred_element_type=F32)


def _dot_nt(a, b):
    return lax.dot_general(a.astype(BF16), b.astype(BF16), (((1,), (1,)), ((), ())), preferred_element_type=F32)


def _dot_tn(a, b):
    return lax.dot_general(a.astype(BF16), b.astype(BF16), (((0,), (0,)), ((), ())), preferred_element_type=F32)


def _sigmoid(x):
    return 1.0 / (1.0 + jnp.exp(-x))


def _silu(x):
    return x * _sigmoid(x)


def _softplus(x):
    return jnp.maximum(x, 0.0) + jnp.log(1.0 + jnp.exp(-jnp.abs(x)))


def _log_sigmoid(x):
    return jnp.minimum(x, 0.0) - jnp.log(1.0 + jnp.exp(-jnp.abs(x)))


def _rms_rows(x, w):
    ms = jnp.mean(x * x, axis=-1, keepdims=True)
    return x * lax.rsqrt(ms + NORM_EPS) * w


def _split_bf16(x, n):
    parts, r = [], x
    for _ in range(n):
        p = r.astype(BF16)
        parts.append(p)
        r = r - p.astype(F32)
    return parts


def _col_chunks(width, step):
    return [(lo, min(lo + step, width)) for lo in range(0, width, step)]


def _mod_kernel(c_ref, w_ref, b_ref, o_ref):
    c = c_ref[...]
    o_ref[0] = _dot(_silu(c), w_ref[0]) + b_ref[0]


def _mod_call(cond, ada_w, ada_b):
    depth, d, n = ada_w.shape
    rows = cond.shape[0]
    tn = 768
    return pl.pallas_call(
        _mod_kernel,
        out_shape=jax.ShapeDtypeStruct((depth, rows, n), F32),
        grid=(depth, n // tn),
        in_specs=[pl.BlockSpec((rows, d), lambda l, j: (0, 0)),
                  pl.BlockSpec((1, d, tn), lambda l, j: (l, 0, j)),
                  pl.BlockSpec((1, 1, tn), lambda l, j: (l, 0, j))],
        out_specs=pl.BlockSpec((1, rows, tn), lambda l, j: (l, 0, j)),
        compiler_params=_cparams(("arbitrary", "arbitrary")),
        name="mod",
    )(cond, ada_w, ada_b.reshape(depth, 1, n))


def _in_proj_kernel(x_ref, mod_ref, nw_ref, w16_ref, w32_ref, o16_ref, o32_ref):
    h = _rms_rows(x_ref[...], nw_ref[...]) * (1.0 + mod_ref[0, 1:2, :]) + mod_ref[0, 0:1, :]
    hb = h.astype(BF16)
    for lo, hi in _col_chunks(PROJ_W, MM_TILE):
        o16_ref[:, lo:hi] = jnp.dot(hb, w16_ref[:, lo:hi], preferred_element_type=F32).astype(BF16)
    for lo, hi in _col_chunks(PROJ32_W, MM_TILE):
        o32_ref[:, lo:hi] = jnp.dot(hb, w32_ref[:, lo:hi], preferred_element_type=F32)


def _in_proj_call(x2d, mod, nw, w16, w32, tm, tiles_per_cond):
    m, d = x2d.shape
    return pl.pallas_call(
        _in_proj_kernel,
        out_shape=(jax.ShapeDtypeStruct((m, PROJ_W), BF16), jax.ShapeDtypeStruct((m, PROJ32_W), F32)),
        grid=(m // tm,),
        in_specs=[pl.BlockSpec((tm, d), lambda i: (i, 0)),
                  pl.BlockSpec((1, 6, d), lambda i: (i // tiles_per_cond, 0, 0)),
                  _resident((1, d), lambda i: (0, 0)),
                  _resident((d, PROJ_W), lambda i: (0, 0)),
                  _resident((d, PROJ32_W), lambda i: (0, 0))],
        out_specs=(pl.BlockSpec((tm, PROJ_W), lambda i: (i, 0)),
                   pl.BlockSpec((tm, PROJ32_W), lambda i: (i, 0))),
        compiler_params=_cparams(("arbitrary",)),
        name="in_proj",
    )(x2d, mod, nw, w16, w32)


ROW_BLOCK = 256


def _gated_norm_epilogue(of_scr, ob_scr, gate_ref, nw_ref, o_ref, t):
    nw = nw_ref[...]

    def body(r, carry):
        r0 = pl.multiple_of(r * ROW_BLOCK, ROW_BLOCK)
        o = of_scr[pl.ds(r0, ROW_BLOCK), :] + ob_scr[pl.ds(r0, ROW_BLOCK), :]
        y = _rms_rows(o, nw) * _silu(gate_ref[pl.ds(r0, ROW_BLOCK), :].astype(F32))
        o_ref[pl.ds(r0, ROW_BLOCK), :] = y.astype(o_ref.dtype)
        return carry

    lax.fori_loop(0, t // ROW_BLOCK, body, 0)


def _tri_inverse(mats, ri, ci, c, tick):
    n = range(len(mats))
    shift = int(np.log2(TRI_BASE))
    base = (ri >> shift) == (ci >> shift)
    mb = [jnp.where(base, m, 0.0) for m in mats]
    y = [-m for m in mb]
    p = [_dot(m, m) for m in mb]
    tick()
    for _ in range(shift - 2):
        yp = [_dot(jnp.concatenate([y[i], p[i]], axis=0), p[i]) for i in n]
        tick()
        y = [y[i] + p[i] + yp[i][:c] for i in n]
        p = [r[c:] for r in yp]
    yp = [_dot(y[i], p[i]) for i in n]
    tick()
    y = [y[i] + p[i] + yp[i] for i in n]
    total = int(np.log2(c))
    while shift < total:
        factors = min(TRI_MERGE, total - shift)
        inner = (ri >> shift) == (ci >> shift)
        outer = (ri >> (shift + factors)) == (ci >> (shift + factors))
        between = jnp.logical_and(outer, jnp.logical_not(inner))
        cm = [jnp.where(between, m, 0.0) for m in mats]
        w = [_dot(y[i], cm[i]) for i in n]
        tick()
        w = [cm[i] + w[i] for i in n]
        if factors == 1:
            wy = [_dot(w[i], y[i]) for i in n]
            tick()
            y = [y[i] - w[i] - wy[i] for i in n]
        else:
            r = [_dot(w[i], jnp.concatenate([w[i], y[i]], axis=1)) for i in n]
            tick()
            p, y = [x[:, :c] for x in r], [y[i] - w[i] - r[i][:, c:] for i in n]
            for f in range(1, factors):
                if f + 1 < factors:
                    r = [_dot(p[i], jnp.concatenate([p[i], y[i]], axis=1)) for i in n]
                    tick()
                    p, y = [x[:, :c] for x in r], [y[i] + p[i] + r[i][:, c:] for i in n]
                else:
                    py = [_dot(p[i], y[i]) for i in n]
                    tick()
                    y = [y[i] + p[i] + py[i] for i in n]
        shift += factors
    eye = jnp.where(ri == ci, 1.0, 0.0)
    return [eye + v for v in y]


def _delta_chunks(probs, states, c, tick):
    n = range(len(probs))
    ri = lax.broadcasted_iota(jnp.int32, (c, c), 0)
    ci = lax.broadcasted_iota(jnp.int32, (c, c), 1)
    eye = ri == ci
    pre = []
    for q, k, v, beta_row, alpha_row, a_neg, dt_b, chain, reverse in probs:
        incl, strict = (ri <= ci, ri < ci) if reverse else (ri >= ci, ri > ci)
        beta_r = _sigmoid(beta_row)
        g_r = a_neg * _softplus(alpha_row + dt_b)
        gc_col = jnp.sum(jnp.where(incl, jnp.broadcast_to(g_r, (c, c)), 0.0), axis=1, keepdims=True)
        beta_col = jnp.sum(jnp.where(eye, jnp.broadcast_to(beta_r, (c, c)), 0.0), axis=1, keepdims=True)
        gc_row = jnp.sum(jnp.where(eye, jnp.broadcast_to(gc_col, (c, c)), 0.0), axis=0, keepdims=True)
        g_tot = jnp.sum(g_r, axis=1, keepdims=True)
        decay = jnp.where(incl, jnp.exp(jnp.where(incl, gc_col - gc_row, 0.0)), 0.0)
        pre.append((strict, gc_col, beta_col, g_tot, decay, jnp.exp(gc_col)))

    kb = [pr[1].astype(BF16) for pr in probs]
    kq = [_dot_nt(jnp.concatenate([kb[i], probs[i][0].astype(BF16)], axis=0), kb[i]) for i in n]
    tick()
    t_inv = _tri_inverse([jnp.where(pre[i][0], kq[i][:c] * pre[i][2] * pre[i][4], 0.0) for i in n], ri, ci, c, tick)
    uw = [_dot(t_inv[i], jnp.concatenate([probs[i][2] * pre[i][2], probs[i][1] * (pre[i][2] * pre[i][5])], axis=1))
          for i in n]
    tick()
    wq_lhs = [jnp.concatenate([uw[i][:, HEAD_W:], probs[i][0] * pre[i][5]], axis=0).astype(BF16) for i in n]
    os_lhs = [jnp.concatenate([kq[i][c:] * pre[i][4], (probs[i][1] * jnp.exp(pre[i][3] - pre[i][1])).T],
                              axis=0).astype(BF16) for i in n]
    states = list(states)
    outs = [None] * len(probs)
    todo = list(n)
    while todo:
        front, seen = [], set()
        for i in todo:
            if probs[i][7] not in seen:
                seen.add(probs[i][7])
                front.append(i)
        todo = [i for i in todo if i not in front]
        wq = [_dot(wq_lhs[i], states[probs[i][7]]) for i in front]
        tick()
        v_new = [uw[i][:, :HEAD_W] - wq[j][:c] for j, i in enumerate(front)]
        os_ = [_dot(os_lhs[i], v_new[j]) for j, i in enumerate(front)]
        tick()
        for j, i in enumerate(front):
            outs[i] = wq[j][c:] + os_[j][:c]
            states[probs[i][7]] = states[probs[i][7]] * jnp.exp(pre[i][3]) + os_[j][c:]
    return outs, states


def _conv_silu_pass(x_ref, cw_ref, xp_scr, dst_scr, t, l2, scale):
    nrb = t // ROW_BLOCK
    pad = SUBLANES
    half = SHORT_CONV // 2

    def cp(r, carry):
        r0 = pl.multiple_of(r * ROW_BLOCK, ROW_BLOCK)
        xp_scr[pl.ds(r0 + pad, ROW_BLOCK), :] = x_ref[pl.ds(r0, ROW_BLOCK), :].astype(F32)
        return carry

    lax.fori_loop(0, nrb, cp, 0)
    cw = cw_ref[...]

    def body(r, carry):
        r0 = pl.multiple_of(r * ROW_BLOCK, ROW_BLOCK)
        win = xp_scr[pl.ds(r0, ROW_BLOCK + 2 * pad), :]
        y = None
        for j in range(SHORT_CONV):
            lo = pad - half + j
            term = win[lo:lo + ROW_BLOCK, :] * cw[j:j + 1, :]
            y = term if y is None else y + term
        y = _silu(y)
        if l2:
            y = y * (lax.rsqrt(jnp.sum(y * y, axis=-1, keepdims=True) + NORM_EPS) * scale)
        dst_scr[pl.ds(r0, ROW_BLOCK), :] = y
        return carry

    lax.fori_loop(0, nrb, body, 0)


def _hgrn_diag(q, kf, v, b, reverse, c):
    nb = c // SUBLANES
    q3, k3, v3, b3 = (a.reshape(nb, SUBLANES, HEAD_W) for a in (q, kf, v, b))
    sub = lax.broadcasted_iota(jnp.int32, (nb, SUBLANES, HEAD_W), 1)
    o3 = jnp.zeros((nb, SUBLANES, HEAD_W), F32)
    for j in range(SUBLANES):
        mask = (sub <= j) if reverse else (sub >= j)
        e = jnp.exp(jnp.where(mask, b3 - b3[:, j:j + 1, :], 0.0))
        a = jnp.sum(jnp.where(mask, q3 * e * k3[:, j:j + 1, :], 0.0), axis=-1, keepdims=True)
        o3 = o3 + a * v3[:, j:j + 1, :]
    return o3.reshape(c, HEAD_W)


def _hgrn_chunk(q, kf, v, lf, st, reverse, c):
    ri = lax.broadcasted_iota(jnp.int32, (c, c), 0)
    ci = lax.broadcasted_iota(jnp.int32, (c, c), 1)
    incl = (ri <= ci) if reverse else (ri >= ci)
    parts = jnp.concatenate(_split_bf16(lf, 3), axis=1)
    b3 = jnp.dot(jnp.where(incl, 1.0, 0.0).astype(BF16), parts, preferred_element_type=F32)
    b = b3[:, :HEAD_W] + b3[:, HEAD_W:2 * HEAD_W] + b3[:, 2 * HEAD_W:]
    b_tot = jnp.sum(lf, axis=0, keepdims=True)
    o = _dot_nt(q * jnp.exp(b), st)
    row = lax.broadcasted_iota(jnp.int32, (c, 1), 0)
    att = jnp.zeros((c, c), F32)
    n = SUBLANES
    while n < c:
        pieces = []
        for g in range(c // (2 * n)):
            r = g * 2 * n + (n if reverse else n - 1)
            pieces.append(jnp.broadcast_to(b[r:r + 1, :], (2 * n, HEAD_W)))
        ref = pieces[0] if len(pieces) == 1 else jnp.concatenate(pieces, axis=0)
        e = jnp.exp(-jnp.abs(b - ref))
        s = int(np.log2(n))
        second = ((row >> s) & 1) == 1
        q_part = jnp.logical_not(second) if reverse else second
        qt = jnp.where(q_part, q * e, 0.0)
        kt = jnp.where(q_part, 0.0, kf * e)
        att = att + jnp.where((ri >> (s + 1)) == (ci >> (s + 1)), _dot_nt(qt, kt), 0.0)
        n *= 2
    o = o + _dot(att, v) + _hgrn_diag(q, kf, v, b, reverse, c)
    kh = kf * jnp.exp(b_tot - b)
    st_new = st * jnp.exp(b_tot) + _dot_tn(v, kh)
    return o, st_new


def _mixer_ab_kernel(*refs, t, ca, cb, layer, has_past, emit_state):
    refs = list(refs)
    (qa_ref, ka_ref, va_ref, ga_ref, gr_ref, cwq_ref, cwk_ref, cwv_ref, alog_ref, dtb_ref, nwa_ref,
     qb_ref, ib_ref, f0_ref, f1_ref, gb_ref, lb_ref, nwb_ref) = refs[:18]
    pos = 18
    sa0_ref = sb0_ref = None
    if has_past:
        sa0_ref, sb0_ref = refs[pos:pos + 2]
        pos += 2
    oa_ref, ob_ref = refs[pos:pos + 2]
    pos += 2
    sfa_ref = sfb_ref = None
    if emit_state:
        sfa_ref, sfb_ref = refs[pos:pos + 2]
        pos += 2
    xp_scr, qn_scr, kn_scr, vn_scr, af_scr, ab_scr, bf_scr, bb_scr, sa_scr, sb_scr = refs[pos:]

    h = pl.program_id(1)
    pad = SUBLANES
    xp_scr[0:pad, :] = jnp.zeros((pad, HEAD_W), F32)
    xp_scr[t + pad:t + 2 * pad, :] = jnp.zeros((pad, HEAD_W), F32)
    _conv_silu_pass(qa_ref, cwq_ref, xp_scr, qn_scr, t, True, HEAD_W ** -0.5)
    _conv_silu_pass(ka_ref, cwk_ref, xp_scr, kn_scr, t, True, 1.0)
    _conv_silu_pass(va_ref, cwv_ref, xp_scr, vn_scr, t, False, 1.0)

    if has_past:
        for d in range(2):
            sa_scr[d] = sa0_ref[0, 0, d, 0]
            sb_scr[d] = sb0_ref[0, 0, d, 0].T
    else:
        sa_scr[...] = jnp.zeros(sa_scr.shape, F32)
        sb_scr[...] = jnp.zeros(sb_scr.shape, F32)

    ones = jnp.ones((1, ca), F32)
    a_neg = [-jnp.exp(ones * alog_ref[d, h]) for d in range(2)]
    dt_b = [dtb_ref[d, h] for d in range(2)]

    lb_terms = None
    if layer > 0:
        lb_terms = []
        for d in range(2):
            lg = lb_ref[d]
            ex = jnp.exp(lg - jnp.max(lg, axis=0, keepdims=True))
            pr = ex / jnp.sum(ex, axis=0, keepdims=True)
            lb = jnp.clip(jnp.sum(pr[1:layer + 1], axis=0, keepdims=True), LB_EPS, 1.0 - LB_EPS)
            lb_terms.append((jnp.log(lb), jnp.log1p(-lb), 1.0 - lb))

    f_refs = (f0_ref, f1_ref)
    nca, ncb = t // ca, t // cb
    rounds = math.gcd(ROUNDS_A, nca)
    ratio = rounds * ca // cb

    def hgrn_gates(z, d):
        if layer == 0:
            return _log_sigmoid(z), _sigmoid(-z)
        log_lb, log1m_lb, one_m_lb = lb_terms[d]
        a2 = log1m_lb + _log_sigmoid(z)
        return jnp.maximum(log_lb, a2) + jnp.log(1.0 + jnp.exp(-jnp.abs(log_lb - a2))), one_m_lb * _sigmoid(-z)

    def step(n, carry):
        a_rows, a_probs = [], []
        for rnd in range(rounds):
            for d in range(2):
                cidx = n * rounds + rnd if d == 0 else nca - 1 - (n * rounds + rnd)
                r0 = pl.multiple_of(cidx * ca, ca)
                gr = gr_ref[0, 0, cidx]
                a_rows.append(r0)
                a_probs.append((qn_scr[pl.ds(r0, ca), :], kn_scr[pl.ds(r0, ca), :], vn_scr[pl.ds(r0, ca), :],
                                gr[d:d + 1, :], gr[2 + d:3 + d, :], a_neg[d], dt_b[d], d, d == 1))
        sa = [sa_scr[d] for d in range(2)]
        b_rows, b_in = [], []
        for j in range(ratio):
            m = n * ratio + j
            rows = [pl.multiple_of((m if d == 0 else ncb - 1 - m) * cb, cb) for d in range(2)]
            b_rows.append(rows)
            b_in.append([(f_refs[d][pl.ds(rows[d], cb), :], qb_ref[pl.ds(rows[d], cb), :],
                          ib_ref[pl.ds(rows[d], cb), :]) for d in range(2)])
        sb = [sb_scr[d] for d in range(2)]

        b_out = [[None, None] for _ in range(ratio)]
        pending = [(j, d) for j in range(ratio) for d in range(2)]

        def tick():
            if pending:
                j, d = pending.pop(0)
                z, qraw, iraw = b_in[j][d]
                lf, kf = hgrn_gates(z, d)
                b_out[j][d], sb[d] = _hgrn_chunk(_silu(qraw.astype(F32)), kf, iraw.astype(F32), lf, sb[d],
                                                 d == 1, cb)

        a_out, sa = _delta_chunks(a_probs, sa, ca, tick)
        while pending:
            tick()

        for i, prob in enumerate(a_probs):
            (af_scr if prob[7] == 0 else ab_scr)[pl.ds(a_rows[i], ca), :] = a_out[i]
        for d in range(2):
            sa_scr[d] = sa[d]
            sb_scr[d] = sb[d]
            for j in range(ratio):
                (bf_scr if d == 0 else bb_scr)[pl.ds(b_rows[j][d], cb), :] = b_out[j][d]
        return carry

    lax.fori_loop(0, nca // rounds, step, 0)

    _gated_norm_epilogue(af_scr, ab_scr, ga_ref, nwa_ref, oa_ref, t)
    _gated_norm_epilogue(bf_scr, bb_scr, gb_ref, nwb_ref, ob_ref, t)
    if emit_state:
        for d in range(2):
            sfa_ref[0, d, 0] = sa_scr[d]
            sfb_ref[0, d, 0] = sb_scr[d].T


def _mixer_ab_call(proj, proj32, gates_r, prm, past, layer, nseq, t, emit_state):
    ca, cb = min(CHUNK_A, t), min(CHUNK_B, t)
    depth = prm["lb_logits"].shape[1]
    col = lambda off: (lambda b, h: (b, off // HEAD_W + h))
    seq_in = lambda off: pl.BlockSpec((t, HEAD_W), col(off))
    conv = lambda part: pl.BlockSpec((SHORT_CONV, HEAD_W), lambda b, h: (0, part * HEADS + h))
    smem = pl.BlockSpec(memory_space=pltpu.SMEM)
    norm = pl.BlockSpec((1, HEAD_W), lambda b, h: (0, 0))
    in_specs = [seq_in(COL_QA), seq_in(COL_KA), seq_in(COL_VA), seq_in(COL_GA),
                pl.BlockSpec((1, 1, t // ca, 4, ca), lambda b, h: (b, h, 0, 0, 0)),
                conv(0), conv(1), conv(2), smem, smem, norm,
                seq_in(COL_QB), seq_in(COL_IB), seq_in(COL32_FB), seq_in(COL32_FB + HEADS * HEAD_W), seq_in(COL_GB),
                pl.BlockSpec((2, depth, HEAD_W), lambda b, h: (0, 0, h)), norm]
    args = [proj, proj, proj, proj, gates_r, prm["conv_a"][layer], prm["conv_a"][layer], prm["conv_a"][layer],
            prm["a_log"][layer], prm["dt_bias"][layer], prm["norm_a"][layer],
            proj, proj, proj32, proj32, proj, prm["lb_logits"], prm["norm_b"][layer]]
    state_in = pl.BlockSpec((1, 1, 2, 1, HEAD_W, HEAD_W), lambda b, h: (b, layer, 0, h, 0, 0))
    if past is not None:
        in_specs += [state_in, state_in]
        args += [past[0], past[1]]
    o_shape = jax.ShapeDtypeStruct((nseq * t, HEADS * HEAD_W), BF16)
    o_spec = pl.BlockSpec((t, HEAD_W), lambda b, h: (b, h))
    out_shape, out_specs = [o_shape, o_shape], [o_spec, o_spec]
    if emit_state:
        s_shape = jax.ShapeDtypeStruct((nseq, 2, HEADS, HEAD_W, HEAD_W), F32)
        s_spec = pl.BlockSpec((1, 2, 1, HEAD_W, HEAD_W), lambda b, h: (b, 0, h, 0, 0))
        out_shape += [s_shape, s_shape]
        out_specs += [s_spec, s_spec]
    seq = pltpu.VMEM((t, HEAD_W), F32)
    state = pltpu.VMEM((2, HEAD_W, HEAD_W), F32)
    return pl.pallas_call(
        functools.partial(_mixer_ab_kernel, t=t, ca=ca, cb=cb, layer=layer, has_past=past is not None,
                          emit_state=emit_state),
        out_shape=tuple(out_shape), grid=(nseq, HEADS), in_specs=in_specs, out_specs=tuple(out_specs),
        scratch_shapes=[pltpu.VMEM((t + 2 * SUBLANES, HEAD_W), F32), seq, seq, seq, seq, seq, seq, seq, state, state],
        compiler_params=_cparams(("arbitrary", "arbitrary")),
        name="mixer_ab",
    )(*args)


def _rms_head_pairs(x, w2):
    lane = lax.broadcasted_iota(jnp.int32, x.shape, 1)
    left = lane < C_HD
    sq = x * x
    s0 = jnp.sum(jnp.where(left, sq, 0.0), axis=-1, keepdims=True)
    s1 = jnp.sum(jnp.where(left, 0.0, sq), axis=-1, keepdims=True)
    ms = jnp.where(left, s0, s1) * (1.0 / C_HD)
    return x * lax.rsqrt(ms + NORM_EPS) * w2


def _rope_pairs(x, cos2, sin2):
    lane = lax.broadcasted_iota(jnp.int32, x.shape, 1)
    quarter = C_HD // 4
    swapped = jnp.where((lane & (2 * quarter - 1)) < quarter,
                        pltpu.roll(x, LANES - quarter, axis=1), pltpu.roll(x, quarter, axis=1))
    return x * cos2 + swapped * sin2


def _softmax_sink_av(scores, values, sink):
    m = sink
    for s in scores:
        m = jnp.maximum(m, jnp.max(s, axis=-1, keepdims=True))
    den = jnp.exp(sink - m)
    acc = None
    for s, v in zip(scores, values):
        p = jnp.exp(s - m)
        den = den + jnp.sum(p, axis=-1, keepdims=True)
        t = jnp.dot(p.astype(BF16), v, preferred_element_type=F32)
        acc = t if acc is None else acc + t
    return acc / den


def _attn_ctx_kernel(q_ref, k_ref, v_ref, qn_ref, kn_ref, sink_ref, o_ref, ko_ref, vo_ref, *, t):
    qw, kw = qn_ref[...], kn_ref[...]
    kn = _rms_head_pairs(k_ref[...].astype(F32), kw)
    ko_ref[0] = kn
    v = v_ref[...]
    vo_ref[0] = v.astype(F32)
    knb, vb = kn.astype(BF16), v
    scale = C_HD ** -0.5
    for pair in range(C_QHEADS // 2):
        qp = (_rms_head_pairs(q_ref[:, pair * LANES:(pair + 1) * LANES].astype(F32), qw) * scale).astype(BF16)
        outs = []
        for half in range(2):
            hq = 2 * pair + half
            hk = hq // C_GROUP
            qh = qp[:, half * C_HD:(half + 1) * C_HD]
            s = _dot_nt(qh, knb[:, hk * C_HD:(hk + 1) * C_HD])
            sink = jnp.full((1, 1), sink_ref[hq], F32)
            outs.append(_softmax_sink_av([s], [vb[:, hk * C_HD:(hk + 1) * C_HD]], sink))
        o_ref[:, pair * LANES:(pair + 1) * LANES] = jnp.concatenate(outs, axis=1).astype(o_ref.dtype)


def _attn_ctx_call(proj, q_norm2, k_norm2, sink, nseq, t):
    return pl.pallas_call(
        functools.partial(_attn_ctx_kernel, t=t),
        out_shape=(jax.ShapeDtypeStruct((nseq * t, C_QHEADS * C_HD), BF16),
                   jax.ShapeDtypeStruct((nseq, t, LANES), F32), jax.ShapeDtypeStruct((nseq, t, LANES), F32)),
        grid=(nseq,),
        in_specs=[pl.BlockSpec((t, C_QHEADS * C_HD), lambda b: (b, COL_QC // (C_QHEADS * C_HD))),
                  pl.BlockSpec((t, LANES), lambda b: (b, COL_KC // LANES)),
                  pl.BlockSpec((t, LANES), lambda b: (b, COL_VC // LANES)),
                  pl.BlockSpec((1, LANES), lambda b: (0, 0)), pl.BlockSpec((1, LANES), lambda b: (0, 0)),
                  pl.BlockSpec(memory_space=pltpu.SMEM)],
        out_specs=(pl.BlockSpec((t, C_QHEADS * C_HD), lambda b: (b, 0)),
                   pl.BlockSpec((1, t, LANES), lambda b: (b, 0, 0)), pl.BlockSpec((1, t, LANES), lambda b: (b, 0, 0))),
        compiler_params=_cparams(("arbitrary",)),
        name="attn_ctx",
    )(proj, proj, proj, q_norm2, k_norm2, sink)


def _attn_lat_kernel(q_ref, k_ref, v_ref, kc_ref, vc_ref, qn_ref, kn_ref, cos_ref, sin_ref, sink_ref, o_ref,
                     qs_scr, ks_scr, vs_scr, bias_scr, *, t, past_len):
    qw, kw = qn_ref[...], kn_ref[...]
    scale = C_HD ** -0.5
    nrb = t // ROW_BLOCK
    blk = C_BLOCK

    ks_scr[0:blk, :] = jnp.zeros((blk, LANES), BF16)
    vs_scr[0:blk, :] = jnp.zeros((blk, LANES), BF16)
    ks_scr[t + blk:t + 2 * blk, :] = jnp.zeros((blk, LANES), BF16)
    vs_scr[t + blk:t + 2 * blk, :] = jnp.zeros((blk, LANES), BF16)

    grp_rows = C_GROUP * blk

    def prep(r, carry):
        r0 = pl.multiple_of(r * ROW_BLOCK, ROW_BLOCK)
        cos2, sin2 = cos_ref[pl.ds(r0, ROW_BLOCK), :], sin_ref[pl.ds(r0, ROW_BLOCK), :]
        kn = _rope_pairs(_rms_head_pairs(k_ref[pl.ds(r0, ROW_BLOCK), :].astype(F32), kw), cos2, sin2)
        ks_scr[pl.ds(r0 + blk, ROW_BLOCK), :] = kn.astype(BF16)
        vs_scr[pl.ds(r0 + blk, ROW_BLOCK), :] = v_ref[pl.ds(r0, ROW_BLOCK), :]
        for pair in range(C_QHEADS // 2):
            qp = _rms_head_pairs(q_ref[pl.ds(r0, ROW_BLOCK), pair * LANES:(pair + 1) * LANES].astype(F32), qw)
            qp = (_rope_pairs(qp, cos2, sin2) * scale).astype(BF16)
            for half in range(2):
                hq = 2 * pair + half
                hk, g = hq // C_GROUP, hq % C_GROUP
                for sub in range(ROW_BLOCK // blk):
                    dst = pl.multiple_of((r * (ROW_BLOCK // blk) + sub) * grp_rows + g * blk, blk)
                    qs_scr[hk, pl.ds(dst, blk), :] = qp[sub * blk:(sub + 1) * blk, half * C_HD:(half + 1) * C_HD]
        return carry

    lax.fori_loop(0, nrb, prep, 0)

    nkeys = 3 * blk + past_len
    kcb = kc_ref[0, 0].astype(BF16)
    vcb = vc_ref[0, 0].astype(BF16)
    qi = lax.broadcasted_iota(jnp.int32, (grp_rows, nkeys), 0) & (blk - 1)
    kj = lax.broadcasted_iota(jnp.int32, (grp_rows, nkeys), 1)
    visible = jnp.logical_or(kj >= 3 * blk, jnp.logical_and(kj >= qi, kj <= qi + 2 * C_WINDOW))
    bias_scr[...] = jnp.where(visible, 0.0, NEG_BIG)
    head_of_row = lax.broadcasted_iota(jnp.int32, (grp_rows, 1), 0) // blk
    kcol = lax.broadcasted_iota(jnp.int32, (1, nkeys), 1)

    def qblock(n, carry):
        r0 = pl.multiple_of(n * blk, blk)
        kpos = kcol + (r0 - blk)
        in_seq = jnp.logical_or(kcol >= 3 * blk, jnp.logical_and(kpos >= 0, kpos < t))
        edge = jnp.where(in_seq, 0.0, NEG_BIG)
        kwin = ks_scr[pl.ds(r0, 3 * blk), :]
        vwin = vs_scr[pl.ds(r0, 3 * blk), :]
        outs = []
        for hk in range(C_KVHEADS):
            lo, hi = hk * C_HD, (hk + 1) * C_HD
            keys = jnp.concatenate([kwin[:, lo:hi], kcb[:, lo:hi]], axis=0)
            vals = jnp.concatenate([vwin[:, lo:hi], vcb[:, lo:hi]], axis=0)
            q_stack = qs_scr[hk, pl.ds(pl.multiple_of(n * grp_rows, grp_rows), grp_rows), :]
            s = _dot_nt(q_stack, keys) + bias_scr[...] + edge
            sink = jnp.full((grp_rows, 1), sink_ref[hk * C_GROUP], F32)
            for g in range(1, C_GROUP):
                sink = jnp.where(head_of_row == g, sink_ref[hk * C_GROUP + g], sink)
            o = _softmax_sink_av([s], [vals], sink)
            outs += [o[g * blk:(g + 1) * blk] for g in range(C_GROUP)]
        for pair in range(C_QHEADS // 2):
            o_ref[pl.ds(r0, blk), pair * LANES:(pair + 1) * LANES] = jnp.concatenate(
                outs[2 * pair:2 * pair + 2], axis=1).astype(o_ref.dtype)
        return carry

    lax.fori_loop(0, t // blk, qblock, 0)


def _attn_lat_call(proj, cache_k, cache_v, q_norm2, k_norm2, cos2, sin2, sink, layer, nseq, t):
    past_len = cache_k.shape[2]
    qw = C_QHEADS * C_HD
    return pl.pallas_call(
        functools.partial(_attn_lat_kernel, t=t, past_len=past_len),
        out_shape=jax.ShapeDtypeStruct((nseq * t, qw), BF16),
        grid=(nseq,),
        in_specs=[pl.BlockSpec((t, qw), lambda b: (b, COL_QC // qw)),
                  pl.BlockSpec((t, LANES), lambda b: (b, COL_KC // LANES)),
                  pl.BlockSpec((t, LANES), lambda b: (b, COL_VC // LANES)),
                  pl.BlockSpec((1, 1, past_len, LANES), lambda b: (b, layer, 0, 0)),
                  pl.BlockSpec((1, 1, past_len, LANES), lambda b: (b, layer, 0, 0)),
                  pl.BlockSpec((1, LANES), lambda b: (0, 0)), pl.BlockSpec((1, LANES), lambda b: (0, 0)),
                  _resident((t, LANES), lambda b: (0, 0)), _resident((t, LANES), lambda b: (0, 0)),
                  pl.BlockSpec(memory_space=pltpu.SMEM)],
        out_specs=pl.BlockSpec((t, qw), lambda b: (b, 0)),
        scratch_shapes=[pltpu.VMEM((C_KVHEADS, t * C_GROUP, C_HD), BF16),
                        pltpu.VMEM((t + 2 * C_BLOCK, LANES), BF16), pltpu.VMEM((t + 2 * C_BLOCK, LANES), BF16),
                        pltpu.VMEM((C_GROUP * C_BLOCK, 3 * C_BLOCK + past_len), F32)],
        compiler_params=_cparams(("arbitrary",)),
        name="attn_lat",
    )(proj, proj, proj, cache_k, cache_v, q_norm2, k_norm2, cos2, sin2, sink)


def _rope_tables(t):
    rows = t // GRID_W
    row = jnp.repeat(jnp.arange(rows, dtype=F32), GRID_W)
    col = jnp.tile(jnp.arange(GRID_W, dtype=F32), rows)
    nf = C_HD // 4
    inv = ROPE_THETA ** (-jnp.arange(nf, dtype=F32) / nf)
    ar, ac = row[:, None] * inv, col[:, None] * inv
    cos = jnp.concatenate([jnp.cos(ar), jnp.cos(ar), jnp.cos(ac), jnp.cos(ac)], axis=1)
    sin = jnp.concatenate([-jnp.sin(ar), jnp.sin(ar), -jnp.sin(ac), jnp.sin(ac)], axis=1)
    return jnp.tile(cos, (1, 2)), jnp.tile(sin, (1, 2))


def _merge_kernel(x_ref, g0_ref, g1_ref, g2_ref, oa_ref, ob_ref, oc_ref, mod_ref, wbr_ref, wout_ref, o_ref):
    merged = (_sigmoid(g0_ref[...].astype(F32)) * jnp.dot(oa_ref[...], wbr_ref[0], preferred_element_type=F32)
              + _sigmoid(g1_ref[...].astype(F32)) * jnp.dot(ob_ref[...], wbr_ref[1], preferred_element_type=F32)
              + _sigmoid(g2_ref[...].astype(F32)) * jnp.dot(oc_ref[...], wbr_ref[2], preferred_element_type=F32))
    res = jnp.dot(merged.astype(BF16), wout_ref[...], preferred_element_type=F32)
    o_ref[...] = x_ref[...] + mod_ref[0, 2:3, :] * res


def _merge_call(x2d, proj, oa, ob, oc, mod, w_br, w_out, tm, tiles_per_cond):
    m, d = x2d.shape
    mg = lambda r: pl.BlockSpec((tm, d), lambda i: (i, COL_MG // d + r))
    br = pl.BlockSpec((tm, BRANCH_W), lambda i: (i, 0))
    return pl.pallas_call(
        _merge_kernel,
        out_shape=jax.ShapeDtypeStruct((m, d), F32),
        grid=(m // tm,),
        in_specs=[pl.BlockSpec((tm, d), lambda i: (i, 0)), mg(0), mg(1), mg(2), br, br, br,
                  pl.BlockSpec((1, 6, d), lambda i: (i // tiles_per_cond, 0, 0)),
                  _resident((3, BRANCH_W, d), lambda i: (0, 0, 0)),
                  _resident((d, d), lambda i: (0, 0))],
        out_specs=pl.BlockSpec((tm, d), lambda i: (i, 0)),
        compiler_params=_cparams(("arbitrary",)),
        name="merge",
    )(x2d, proj, proj, proj, oa, ob, oc, mod, w_br, w_out)


FF_CHUNK = 256
HALO = BF16_ROWS


def _ffn_kernel(x_ref, xp_ref, xn_ref, mod_ref, nw_ref, wup_ref, cw_ref, wd_ref, o_ref, h_scr, act_scr, *,
                tm, seq_len):
    i = pl.program_id(0)
    nseg = max(1, tm // seq_len)
    seg = tm // nseg
    nw, sh, sc = nw_ref[...], mod_ref[0, 3:4, :], mod_ref[0, 4:5, :]

    def norm(x):
        return _rms_rows(x, nw) * (1.0 + sc) + sh

    has_prev = ((i * tm) & (seq_len - 1)) != 0
    has_next = (((i + 1) * tm) & (seq_len - 1)) != 0
    zero_halo = jnp.zeros((HALO, x_ref.shape[1]), BF16)
    for s in range(nseg):
        h_scr[s, HALO:HALO + seg, :] = norm(x_ref[s * seg:(s + 1) * seg, :]).astype(BF16)
        if s == 0:
            h_scr[s, 0:HALO, :] = (norm(xp_ref[...]) * jnp.where(has_prev, 1.0, 0.0)).astype(BF16)
        else:
            h_scr[s, 0:HALO, :] = zero_halo
        if s == nseg - 1:
            h_scr[s, HALO + seg:2 * HALO + seg, :] = (norm(xn_ref[...]) * jnp.where(has_next, 1.0, 0.0)).astype(BF16)
        else:
            h_scr[s, HALO + seg:2 * HALO + seg, :] = zero_halo

    def conv(u, cw):
        return (u[HALO - 1:HALO - 1 + seg] * cw[0:1, :] + u[HALO:HALO + seg] * cw[1:2, :]
                + u[HALO + 1:HALO + 1 + seg] * cw[2:3, :])

    for lo, hi in _col_chunks(D_FF, FF_CHUNK):
        for s in range(nseg):
            h = h_scr[s]
            a = conv(jnp.dot(h, wup_ref[:, lo:hi], preferred_element_type=F32), cw_ref[:, lo:hi])
            u = conv(jnp.dot(h, wup_ref[:, D_FF + lo:D_FF + hi], preferred_element_type=F32),
                     cw_ref[:, D_FF + lo:D_FF + hi])
            act_scr[s * seg:(s + 1) * seg, lo:hi] = (_silu(a) * u).astype(BF16)

    o_ref[...] = x_ref[...] + mod_ref[0, 5:6, :] * jnp.dot(act_scr[...], wd_ref[...], preferred_element_type=F32)


def _ffn_call(x2d, mod, nw, w_up, conv_w, w_down, tm, tiles_per_cond, seq_len):
    m, d = x2d.shape
    hb = tm // HALO
    last = m // HALO - 1
    nseg = max(1, tm // seq_len)
    seg = tm // nseg
    return pl.pallas_call(
        functools.partial(_ffn_kernel, tm=tm, seq_len=seq_len),
        out_shape=jax.ShapeDtypeStruct((m, d), F32),
        grid=(m // tm,),
        in_specs=[pl.BlockSpec((tm, d), lambda i: (i, 0)),
                  pl.BlockSpec((HALO, d), lambda i: (jnp.maximum(i * hb - 1, 0), 0)),
                  pl.BlockSpec((HALO, d), lambda i: (jnp.minimum((i + 1) * hb, last), 0)),
                  pl.BlockSpec((1, 6, d), lambda i: (i // tiles_per_cond, 0, 0)),
                  _resident((1, d), lambda i: (0, 0)),
                  _resident((d, 2 * D_FF), lambda i: (0, 0)),
                  _resident((3, 2 * D_FF), lambda i: (0, 0)),
                  _resident((D_FF, d), lambda i: (0, 0))],
        out_specs=pl.BlockSpec((tm, d), lambda i: (i, 0)),
        scratch_shapes=[pltpu.VMEM((nseg, seg + 2 * HALO, d), BF16), pltpu.VMEM((tm, D_FF), BF16)],
        compiler_params=_cparams(("arbitrary",)),
        name="ffn",
    )(x2d, x2d, x2d, mod, nw, w_up, conv_w, w_down)


def _permute_w_in(w):
    s = _SRC
    w16 = jnp.concatenate([w[:, s["mg"]:s["end"]], w[:, s["qa"]:s["beta"]], w[:, s["qb"]:s["fb"]],
                           w[:, s["gb"]:s["mg"]]], axis=1)
    n_gate = s["qb"] - s["beta"]
    w32 = jnp.concatenate([w[:, s["fb"]:s["gb"]], jnp.pad(w[:, s["beta"]:s["qb"]], ((0, 0), (0, LANES - n_gate)))],
                          axis=1)
    return w16.astype(BF16), w32.astype(BF16)


def _gate_rows(proj32, nseq, t, c):
    g = proj32[:, COL32_GATES:COL32_GATES + 4 * HEADS].reshape(nseq, t // c, c, 2, 2, HEADS)
    return jnp.transpose(g, (0, 5, 1, 3, 4, 2)).reshape(nseq, HEADS, t // c, 4, c)


def _row_tile(rows, t):
    tm = 512
    while rows % tm or (t % tm and tm % t):
        tm //= 2
    return tm


def _group_forward(x3d, mod_g, prm, past, tables):
    nseq, t, d = x3d.shape
    x = x3d.reshape(nseq * t, d)
    tm = _row_tile(nseq * t, t)
    tiles_per_cond = (nseq * t) // tm if mod_g.shape[1] == 1 else t // tm
    emit = past is None
    states_a, states_b, keys, vals = [], [], [], []
    for l in range(len(prm["w16"])):
        mod = mod_g[l]
        proj, proj32 = _in_proj_call(x, mod, prm["norm1_w"][l], prm["w16"][l], prm["w32"][l], tm, tiles_per_cond)
        gates_r = _gate_rows(proj32, nseq, t, min(CHUNK_A, t))
        res_ab = _mixer_ab_call(proj, proj32, gates_r, prm, past, l, nseq, t, emit)
        if emit:
            oc, kn, vn = _attn_ctx_call(proj, prm["q_norm2"][l], prm["k_norm2"][l], prm["sink"][l], nseq, t)
            states_a.append(res_ab[2])
            states_b.append(res_ab[3])
            keys.append(kn.reshape(nseq, t, C_KVHEADS, C_HD))
            vals.append(vn.reshape(nseq, t, C_KVHEADS, C_HD))
        else:
            oc = _attn_lat_call(proj, past[2], past[3], prm["q_norm2"][l], prm["k_norm2"][l], tables[0], tables[1],
                                prm["sink"][l], l, nseq, t)
        x = _merge_call(x, proj, res_ab[0], res_ab[1], oc, mod, prm["w_branch"][l], prm["w_out"][l], tm, tiles_per_cond)
        x = _ffn_call(x, mod, prm["norm2_w"][l], prm["w_up"][l], prm["conv_ffn"][l], prm["w_down"][l],
                      tm, tiles_per_cond, t)
    return x.reshape(nseq, t, d), states_a, states_b, keys, vals


def kernel(x_prompt, x_sample, state_delta, state_hgrn, cache_k, cache_v, c, c_ctx, ada_w, ada_b, norm1_w, w_in, conv_a, a_log, dt_bias, norm_a, lb_logits, norm_b, q_norm, k_norm, sink, w_branch, w_out, norm2_w, w_up, conv_ffn, w_down):
    depth = w_in.shape[0]
    d = x_prompt.shape[-1]
    n_lat = c.shape[0]

    cond = jnp.concatenate([c_ctx[None, :], c], axis=0)
    rows = -(-cond.shape[0] // SUBLANES) * SUBLANES
    cond = jnp.pad(cond, ((0, rows - cond.shape[0]), (0, 0)))
    mod_all = _mod_call(cond, ada_w, ada_b).reshape(depth, rows, 6, d)

    perm = [_permute_w_in(w_in[l]) for l in range(depth)]
    prm = dict(
        w16=[p[0] for p in perm], w32=[p[1] for p in perm],
        norm1_w=norm1_w.reshape(depth, 1, d), norm2_w=norm2_w.reshape(depth, 1, d),
        conv_a=conv_a, a_log=a_log, dt_bias=dt_bias, norm_a=norm_a.reshape(depth, 1, HEAD_W),
        lb_logits=lb_logits, norm_b=norm_b.reshape(depth, 1, HEAD_W),
        q_norm2=jnp.tile(q_norm, (1, 2)).reshape(depth, 1, LANES), k_norm2=jnp.tile(k_norm, (1, 2)).reshape(depth, 1, LANES),
        sink=sink, w_branch=w_branch.astype(BF16), w_out=w_out.astype(BF16),
        w_up=w_up.astype(BF16), conv_ffn=conv_ffn, w_down=w_down.astype(BF16))

    y_prompt, st_a, st_b, keys, vals = _group_forward(x_prompt, mod_all[:, 0:1], prm, None, None)

    past_len = cache_k.shape[2]
    past = (state_delta, state_hgrn,
            cache_k.reshape(cache_k.shape[0], depth, past_len, C_KVHEADS * C_HD),
            cache_v.reshape(cache_v.shape[0], depth, past_len, C_KVHEADS * C_HD))
    y_sample, _, _, _, _ = _group_forward(x_sample, mod_all[:, 1:1 + n_lat], prm, past, _rope_tables(x_sample.shape[1]))

    return (y_prompt, y_sample, jnp.stack(st_a, axis=1), jnp.stack(st_b, axis=1),
            jnp.stack(keys, axis=1), jnp.stack(vals, axis=1))
```

```python
import functools
import math

import numpy as np
import jax
import jax.numpy as jnp
from jax import lax
from jax.experimental import pallas as pl
from jax.experimental.pallas import tpu as pltpu

F32 = jnp.float32
BF16 = jnp.bfloat16

D_MODEL = 1024
NORM_EPS = 1e-6
LB_EPS = 1e-6
NEG_BIG = -1e30
LOG2E = 1.4426950408889634
GRID_W = 64
ROPE_THETA = 10000.0

HEADS = 4
HEAD_W = 128
SHORT_CONV = 5
C_QHEADS = 8
C_KVHEADS = 2
C_GROUP = C_QHEADS // C_KVHEADS
C_HD = 64
C_WINDOW = 128
C_BLOCK = 128
BRANCH_W = 512
D_FF = 2816

LANES = 128
SUBLANES = 8
BF16_ROWS = 16
VMEM_LIMIT = 56 * 1024 * 1024

COL_MG = 0
COL_QA = 3072
COL_KA = 3584
COL_VA = 4096
COL_GA = 4608
COL_QB = 5120
COL_IB = 5632
COL_GB = 6144
COL_QC = 6656
COL_KC = 7168
COL_VC = 7296
PROJ_W = 7424
COL32_FB = 0
COL32_GATES = 1024
PROJ32_W = 1152
_SRC = dict(qa=0, ka=512, va=1024, ga=1536, beta=2048, alpha=2056, qb=2064, ib=2576, fb=3088, gb=4112,
            qc=4624, kc=5136, vc=5264, mg=5392, end=8464)

CHUNK_A = 128
ROUNDS_A = 8
CHUNK_B = 64
TRI_BASE = 16
TRI_MERGE = 2
MM_TILE = 768


def _cparams(sem):
    return pltpu.CompilerParams(dimension_semantics=sem, vmem_limit_bytes=VMEM_LIMIT)


def _resident(shape, index_map):
    return pl.BlockSpec(shape, index_map, pipeline_mode=pl.Buffered(1))


def _dot(a, b):
    return jnp.dot(a.astype(BF16), b.astype(BF16), preferred_element_type=F32)


def _dot_nt(a, b):
    return lax.dot_general(a.astype(BF16), b.astype(BF16), (((1,), (1,)), ((), ())), preferred_element_type=F32)


def _dot_tn(a, b):
    return lax.dot_general(a.astype(BF16), b.astype(BF16), (((0,), (0,)), ((), ())), preferred_element_type=F32)


def _sigmoid(x):
    return 1.0 / (1.0 + jnp.exp(-x))


def _silu(x):
    return x * _sigmoid(x)


def _softplus(x):
    return jnp.maximum(x, 0.0) + jnp.log(1.0 + jnp.exp(-jnp.abs(x)))


def _rms_rows(x, w):
    ms = jnp.mean(x * x, axis=-1, keepdims=True)
    return x * lax.rsqrt(ms + NORM_EPS) * w


def _split_bf16(x, n):
    parts, r = [], x
    for _ in range(n):
        p = r.astype(BF16)
        parts.append(p)
        r = r - p.astype(F32)
    return parts


def _col_chunks(width, step):
    return [(lo, min(lo + step, width)) for lo in range(0, width, step)]


def _mod_kernel(c_ref, w_ref, b_ref, o_ref):
    c = c_ref[...]
    o_ref[0] = _dot(_silu(c), w_ref[0]) + b_ref[0]


def _mod_call(cond, ada_w, ada_b):
    depth, d, n = ada_w.shape
    rows = cond.shape[0]
    tn = 768
    return pl.pallas_call(
        _mod_kernel,
        out_shape=jax.ShapeDtypeStruct((depth, rows, n), F32),
        grid=(depth, n // tn),
        in_specs=[pl.BlockSpec((rows, d), lambda l, j: (0, 0)),
                  pl.BlockSpec((1, d, tn), lambda l, j: (l, 0, j)),
                  pl.BlockSpec((1, 1, tn), lambda l, j: (l, 0, j))],
        out_specs=pl.BlockSpec((1, rows, tn), lambda l, j: (l, 0, j)),
        compiler_params=_cparams(("arbitrary", "arbitrary")),
        name="mod",
    )(cond, ada_w, ada_b.reshape(depth, 1, n))


def _in_proj_kernel(x_ref, mod_ref, nw_ref, w16_ref, w32_ref, o16_ref, o32_ref):
    h = _rms_rows(x_ref[...], nw_ref[...]) * (1.0 + mod_ref[0, 1:2, :]) + mod_ref[0, 0:1, :]
    hb = h.astype(BF16)
    for lo, hi in _col_chunks(PROJ_W, MM_TILE):
        o16_ref[:, lo:hi] = jnp.dot(hb, w16_ref[:, lo:hi], preferred_element_type=F32).astype(BF16)
    for lo, hi in _col_chunks(PROJ32_W, MM_TILE):
        o32_ref[:, lo:hi] = jnp.dot(hb, w32_ref[:, lo:hi], preferred_element_type=F32)


def _in_proj_call(x2d, mod, nw, w16, w32, tm, tiles_per_cond):
    m, d = x2d.shape
    return pl.pallas_call(
        _in_proj_kernel,
        out_shape=(jax.ShapeDtypeStruct((m, PROJ_W), BF16), jax.ShapeDtypeStruct((m, PROJ32_W), F32)),
        grid=(m // tm,),
        in_specs=[pl.BlockSpec((tm, d), lambda i: (i, 0)),
                  pl.BlockSpec((1, 6, d), lambda i: (i // tiles_per_cond, 0, 0)),
                  _resident((1, d), lambda i: (0, 0)),
                  _resident((d, PROJ_W), lambda i: (0, 0)),
                  _resident((d, PROJ32_W), lambda i: (0, 0))],
        out_specs=(pl.BlockSpec((tm, PROJ_W), lambda i: (i, 0)),
                   pl.BlockSpec((tm, PROJ32_W), lambda i: (i, 0))),
        compiler_params=_cparams(("arbitrary",)),
        name="in_proj",
    )(x2d, mod, nw, w16, w32)


ROW_BLOCK = 256


def _gated_norm_epilogue(of_scr, ob_scr, gate_ref, nw_ref, o_ref, t):
    nw = nw_ref[...]

    def body(r, carry):
        r0 = pl.multiple_of(r * ROW_BLOCK, ROW_BLOCK)
        o = of_scr[pl.ds(r0, ROW_BLOCK), :] + ob_scr[pl.ds(r0, ROW_BLOCK), :]
        y = _rms_rows(o, nw) * _silu(gate_ref[pl.ds(r0, ROW_BLOCK), :].astype(F32))
        o_ref[pl.ds(r0, ROW_BLOCK), :] = y.astype(o_ref.dtype)
        return carry

    lax.fori_loop(0, t // ROW_BLOCK, body, 0)


def _tri_inverse(mats, ri, ci, c, tick):
    n = range(len(mats))
    shift = int(np.log2(TRI_BASE))
    base = (ri >> shift) == (ci >> shift)
    mb = [jnp.where(base, m, 0.0) for m in mats]
    y = [-m for m in mb]
    p = [_dot(m, m) for m in mb]
    tick()
    for _ in range(shift - 2):
        yp = [_dot(jnp.concatenate([y[i], p[i]], axis=0), p[i]) for i in n]
        tick()
        y = [y[i] + p[i] + yp[i][:c] for i in n]
        p = [r[c:] for r in yp]
    yp = [_dot(y[i], p[i]) for i in n]
    tick()
    y = [y[i] + p[i] + yp[i] for i in n]
    total = int(np.log2(c))
    while shift < total:
        factors = min(TRI_MERGE, total - shift)
        inner = (ri >> shift) == (ci >> shift)
        outer = (ri >> (shift + factors)) == (ci >> (shift + factors))
        between = jnp.logical_and(outer, jnp.logical_not(inner))
        cm = [jnp.where(between, m, 0.0) for m in mats]
        w = [_dot(y[i], cm[i]) for i in n]
        tick()
        w = [cm[i] + w[i] for i in n]
        if factors == 1:
            wy = [_dot(w[i], y[i]) for i in n]
            tick()
            y = [y[i] - w[i] - wy[i] for i in n]
        else:
            r = [_dot(w[i], jnp.concatenate([w[i], y[i]], axis=1)) for i in n]
            tick()
            p, y = [x[:, :c] for x in r], [y[i] - w[i] - r[i][:, c:] for i in n]
            for f in range(1, factors):
                if f + 1 < factors:
                    r = [_dot(p[i], jnp.concatenate([p[i], y[i]], axis=1)) for i in n]
                    tick()
                    p, y = [x[:, :c] for x in r], [y[i] + p[i] + r[i][:, c:] for i in n]
                else:
                    py = [_dot(p[i], y[i]) for i in n]
                    tick()
                    y = [y[i] + p[i] + py[i] for i in n]
        shift += factors
    eye = jnp.where(ri == ci, 1.0, 0.0)
    return [eye + v for v in y]


def _delta_chunks(probs, states, c, tick):
    n = range(len(probs))
    ri = lax.broadcasted_iota(jnp.int32, (c, c), 0)
    ci = lax.broadcasted_iota(jnp.int32, (c, c), 1)
    eye = ri == ci
    pre = []
    for q, k, v, beta_row, alpha_row, a_neg, dt_b, chain, reverse in probs:
        incl, strict = (ri <= ci, ri < ci) if reverse else (ri >= ci, ri > ci)
        beta_r = _sigmoid(beta_row)
        g_r = a_neg * _softplus(alpha_row + dt_b)
        gc_col = jnp.sum(jnp.where(incl, jnp.broadcast_to(g_r, (c, c)), 0.0), axis=1, keepdims=True)
        beta_col = jnp.sum(jnp.where(eye, jnp.broadcast_to(beta_r, (c, c)), 0.0), axis=1, keepdims=True)
        gc_row = jnp.sum(jnp.where(eye, jnp.broadcast_to(gc_col, (c, c)), 0.0), axis=0, keepdims=True)
        g_tot = jnp.sum(g_r, axis=1, keepdims=True)
        decay = jnp.where(incl, jnp.exp(jnp.where(incl, gc_col - gc_row, 0.0)), 0.0)
        pre.append((strict, gc_col, beta_col, g_tot, decay, jnp.exp(gc_col)))

    kb = [pr[1].astype(BF16) for pr in probs]
    kq = [_dot_nt(jnp.concatenate([kb[i], probs[i][0].astype(BF16)], axis=0), kb[i]) for i in n]
    tick()
    t_inv = _tri_inverse([jnp.where(pre[i][0], kq[i][:c] * pre[i][2] * pre[i][4], 0.0) for i in n], ri, ci, c, tick)
    uw = [_dot(t_inv[i], jnp.concatenate([probs[i][2] * pre[i][2], probs[i][1] * (pre[i][2] * pre[i][5])], axis=1))
          for i in n]
    tick()
    wq_lhs = [jnp.concatenate([uw[i][:, HEAD_W:], probs[i][0] * pre[i][5]], axis=0).astype(BF16) for i in n]
    os_lhs = [jnp.concatenate([kq[i][c:] * pre[i][4], (probs[i][1] * jnp.exp(pre[i][3] - pre[i][1])).T],
                              axis=0).astype(BF16) for i in n]
    states = list(states)
    outs = [None] * len(probs)
    todo = list(n)
    while todo:
        front, seen = [], set()
        for i in todo:
            if probs[i][7] not in seen:
                seen.add(probs[i][7])
                front.append(i)
        todo = [i for i in todo if i not in front]
        wq = [_dot(wq_lhs[i], states[probs[i][7]]) for i in front]
        tick()
        v_new = [uw[i][:, :HEAD_W] - wq[j][:c] for j, i in enumerate(front)]
        os_ = [_dot(os_lhs[i], v_new[j]) for j, i in enumerate(front)]
        tick()
        for j, i in enumerate(front):
            outs[i] = wq[j][c:] + os_[j][:c]
            states[probs[i][7]] = states[probs[i][7]] * jnp.exp(pre[i][3]) + os_[j][c:]
    return outs, states


def _conv_silu_pass(x_ref, cw_ref, xp_scr, dst_scr, t, l2, scale):
    nrb = t // ROW_BLOCK
    pad = SUBLANES
    half = SHORT_CONV // 2

    def cp(r, carry):
        r0 = pl.multiple_of(r * ROW_BLOCK, ROW_BLOCK)
        xp_scr[pl.ds(r0 + pad, ROW_BLOCK), :] = x_ref[pl.ds(r0, ROW_BLOCK), :].astype(F32)
        return carry

    lax.fori_loop(0, nrb, cp, 0)
    cw = cw_ref[...]

    def body(r, carry):
        r0 = pl.multiple_of(r * ROW_BLOCK, ROW_BLOCK)
        win = xp_scr[pl.ds(r0, ROW_BLOCK + 2 * pad), :]
        y = None
        for j in range(SHORT_CONV):
            lo = pad - half + j
            term = win[lo:lo + ROW_BLOCK, :] * cw[j:j + 1, :]
            y = term if y is None else y + term
        y = _silu(y)
        if l2:
            y = y * (lax.rsqrt(jnp.sum(y * y, axis=-1, keepdims=True) + NORM_EPS) * scale)
        dst_scr[pl.ds(r0, ROW_BLOCK), :] = y
        return carry

    lax.fori_loop(0, nrb, body, 0)


def _hgrn_diag(q, kf, v, b, reverse, c):
    nb = c // SUBLANES
    q3, k3, v3, b3 = (a.reshape(nb, SUBLANES, HEAD_W) for a in (q, kf, v, b))
    sub = lax.broadcasted_iota(jnp.int32, (nb, SUBLANES, HEAD_W), 1)
    o3 = jnp.zeros((nb, SUBLANES, HEAD_W), F32)
    for j in range(SUBLANES):
        mask = (sub <= j) if reverse else (sub >= j)
        e = jnp.exp2(jnp.where(mask, b3 - b3[:, j:j + 1, :], NEG_BIG))
        a = jnp.sum(q3 * e * k3[:, j:j + 1, :], axis=-1, keepdims=True)
        o3 = o3 + a * v3[:, j:j + 1, :]
    return o3.reshape(c, HEAD_W)


def _hgrn_chunk(q, kf, v, lf, st, reverse, c):
    ri = lax.broadcasted_iota(jnp.int32, (c, c), 0)
    ci = lax.broadcasted_iota(jnp.int32, (c, c), 1)
    incl = (ri <= ci) if reverse else (ri >= ci)
    parts = jnp.concatenate(_split_bf16(lf, 3), axis=1)
    b3 = jnp.dot(jnp.where(incl, 1.0, 0.0).astype(BF16), parts, preferred_element_type=F32)
    b = b3[:, :HEAD_W] + b3[:, HEAD_W:2 * HEAD_W] + b3[:, 2 * HEAD_W:]
    b_tot = jnp.sum(lf, axis=0, keepdims=True)
    o = _dot_nt(q * jnp.exp2(b), st)
    row = lax.broadcasted_iota(jnp.int32, (c, 1), 0)
    att = jnp.zeros((c, c), F32)
    n = SUBLANES
    while n < c:
        pieces = []
        for g in range(c // (2 * n)):
            r = g * 2 * n + (n if reverse else n - 1)
            pieces.append(jnp.broadcast_to(b[r:r + 1, :], (2 * n, HEAD_W)))
        ref = pieces[0] if len(pieces) == 1 else jnp.concatenate(pieces, axis=0)
        e = jnp.exp2(-jnp.abs(b - ref))
        s = int(np.log2(n))
        second = ((row >> s) & 1) == 1
        q_part = jnp.logical_not(second) if reverse else second
        qt = jnp.where(q_part, q * e, 0.0)
        kt = jnp.where(q_part, 0.0, kf * e)
        att = att + jnp.where((ri >> (s + 1)) == (ci >> (s + 1)), _dot_nt(qt, kt), 0.0)
        n *= 2
    o = o + _dot(att, v) + _hgrn_diag(q, kf, v, b, reverse, c)
    kh = kf * jnp.exp2(b_tot - b)
    st_new = st * jnp.exp2(b_tot) + _dot_tn(v, kh)
    return o, st_new


def _mixer_ab_kernel(*refs, t, ca, cb, layer, has_past, emit_state):
    refs = list(refs)
    (qa_ref, ka_ref, va_ref, ga_ref, gr_ref, cwq_ref, cwk_ref, cwv_ref, alog_ref, dtb_ref, nwa_ref,
     qb_ref, ib_ref, f0_ref, f1_ref, gb_ref, lb_ref, nwb_ref) = refs[:18]
    pos = 18
    sa0_ref = sb0_ref = None
    if has_past:
        sa0_ref, sb0_ref = refs[pos:pos + 2]
        pos += 2
    oa_ref, ob_ref = refs[pos:pos + 2]
    pos += 2
    sfa_ref = sfb_ref = None
    if emit_state:
        sfa_ref, sfb_ref = refs[pos:pos + 2]
        pos += 2
    xp_scr, qn_scr, kn_scr, vn_scr, af_scr, ab_scr, bf_scr, bb_scr, sa_scr, sb_scr = refs[pos:]

    h = pl.program_id(1)
    pad = SUBLANES
    xp_scr[0:pad, :] = jnp.zeros((pad, HEAD_W), F32)
    xp_scr[t + pad:t + 2 * pad, :] = jnp.zeros((pad, HEAD_W), F32)
    _conv_silu_pass(qa_ref, cwq_ref, xp_scr, qn_scr, t, True, HEAD_W ** -0.5)
    _conv_silu_pass(ka_ref, cwk_ref, xp_scr, kn_scr, t, True, 1.0)
    _conv_silu_pass(va_ref, cwv_ref, xp_scr, vn_scr, t, False, 1.0)

    if has_past:
        for d in range(2):
            sa_scr[d] = sa0_ref[0, 0, d, 0]
            sb_scr[d] = sb0_ref[0, 0, d, 0].T
    else:
        sa_scr[...] = jnp.zeros(sa_scr.shape, F32)
        sb_scr[...] = jnp.zeros(sb_scr.shape, F32)

    ones = jnp.ones((1, ca), F32)
    a_neg = [-jnp.exp(ones * alog_ref[d, h]) for d in range(2)]
    dt_b = [dtb_ref[d, h] for d in range(2)]

    lb_terms = None
    if layer > 0:
        lb_terms = []
        for d in range(2):
            lg = lb_ref[d]
            ex = jnp.exp(lg - jnp.max(lg, axis=0, keepdims=True))
            pr = ex / jnp.sum(ex, axis=0, keepdims=True)
            lb = jnp.clip(jnp.sum(pr[1:layer + 1], axis=0, keepdims=True), LB_EPS, 1.0 - LB_EPS)
            lb_terms.append((jnp.log(lb) * LOG2E, jnp.log1p(-lb) * LOG2E, 1.0 - lb))

    f_refs = (f0_ref, f1_ref)
    nca, ncb = t // ca, t // cb
    rounds = math.gcd(ROUNDS_A, nca)
    ratio = rounds * ca // cb

    def hgrn_gates(z, d):
        z2 = z * LOG2E
        y = jnp.exp2(-jnp.abs(z2))
        one_y = 1.0 + y
        l2_sig = jnp.minimum(z2, 0.0) - jnp.log(one_y) * LOG2E
        sig_neg = jnp.where(z >= 0.0, y, 1.0) / one_y
        if layer == 0:
            return l2_sig, sig_neg
        l2_lb, l2_1m_lb, one_m_lb = lb_terms[d]
        a2 = l2_1m_lb + l2_sig
        return (jnp.maximum(l2_lb, a2) + jnp.log(1.0 + jnp.exp2(-jnp.abs(l2_lb - a2))) * LOG2E,
                one_m_lb * sig_neg)

    def step(n, carry):
        a_rows, a_probs = [], []
        for rnd in range(rounds):
            for d in range(2):
                cidx = n * rounds + rnd if d == 0 else nca - 1 - (n * rounds + rnd)
                r0 = pl.multiple_of(cidx * ca, ca)
                gr = gr_ref[0, 0, cidx]
                a_rows.append(r0)
                a_probs.append((qn_scr[pl.ds(r0, ca), :], kn_scr[pl.ds(r0, ca), :], vn_scr[pl.ds(r0, ca), :],
                                gr[d:d + 1, :], gr[2 + d:3 + d, :], a_neg[d], dt_b[d], d, d == 1))
        sa = [sa_scr[d] for d in range(2)]
        b_rows, b_in = [], []
        for j in range(ratio):
            m = n * ratio + j
            rows = [pl.multiple_of((m if d == 0 else ncb - 1 - m) * cb, cb) for d in range(2)]
            b_rows.append(rows)
            b_in.append([(f_refs[d][pl.ds(rows[d], cb), :], qb_ref[pl.ds(rows[d], cb), :],
                          ib_ref[pl.ds(rows[d], cb), :]) for d in range(2)])
        sb = [sb_scr[d] for d in range(2)]

        b_out = [[None, None] for _ in range(ratio)]
        pending = [(j, d) for j in range(ratio) for d in range(2)]

        def tick():
            if pending:
                j, d = pending.pop(0)
                z, qraw, iraw = b_in[j][d]
                lf, kf = hgrn_gates(z, d)
                b_out[j][d], sb[d] = _hgrn_chunk(_silu(qraw.astype(F32)), kf, iraw.astype(F32), lf, sb[d],
                                                 d == 1, cb)

        a_out, sa = _delta_chunks(a_probs, sa, ca, tick)
        while pending:
            tick()

        for i, prob in enumerate(a_probs):
            (af_scr if prob[7] == 0 else ab_scr)[pl.ds(a_rows[i], ca), :] = a_out[i]
        for d in range(2):
            sa_scr[d] = sa[d]
            sb_scr[d] = sb[d]
            for j in range(ratio):
                (bf_scr if d == 0 else bb_scr)[pl.ds(b_rows[j][d], cb), :] = b_out[j][d]
        return carry

    lax.fori_loop(0, nca // rounds, step, 0)

    _gated_norm_epilogue(af_scr, ab_scr, ga_ref, nwa_ref, oa_ref, t)
    _gated_norm_epilogue(bf_scr, bb_scr, gb_ref, nwb_ref, ob_ref, t)
    if emit_state:
        for d in range(2):
            sfa_ref[0, d, 0] = sa_scr[d]
            sfb_ref[0, d, 0] = sb_scr[d].T


def _mixer_ab_call(proj, proj32, gates_r, prm, past, layer, nseq, t, emit_state):
    ca, cb = min(CHUNK_A, t), min(CHUNK_B, t)
    depth = prm["lb_logits"].shape[1]
    col = lambda off: (lambda b, h: (b, off // HEAD_W + h))
    seq_in = lambda off: pl.BlockSpec((t, HEAD_W), col(off))
    conv = lambda part: pl.BlockSpec((SHORT_CONV, HEAD_W), lambda b, h: (0, part * HEADS + h))
    smem = pl.BlockSpec(memory_space=pltpu.SMEM)
    norm = pl.BlockSpec((1, HEAD_W), lambda b, h: (0, 0))
    in_specs = [seq_in(COL_QA), seq_in(COL_KA), seq_in(COL_VA), seq_in(COL_GA),
                pl.BlockSpec((1, 1, t // ca, 4, ca), lambda b, h: (b, h, 0, 0, 0)),
                conv(0), conv(1), conv(2), smem, smem, norm,
                seq_in(COL_QB), seq_in(COL_IB), seq_in(COL32_FB), seq_in(COL32_FB + HEADS * HEAD_W), seq_in(COL_GB),
                pl.BlockSpec((2, depth, HEAD_W), lambda b, h: (0, 0, h)), norm]
    args = [proj, proj, proj, proj, gates_r, prm["conv_a"][layer], prm["conv_a"][layer], prm["conv_a"][layer],
            prm["a_log"][layer], prm["dt_bias"][layer], prm["norm_a"][layer],
            proj, proj, proj32, proj32, proj, prm["lb_logits"], prm["norm_b"][layer]]
    state_in = pl.BlockSpec((1, 1, 2, 1, HEAD_W, HEAD_W), lambda b, h: (b, layer, 0, h, 0, 0))
    if past is not None:
        in_specs += [state_in, state_in]
        args += [past[0], past[1]]
    o_shape = jax.ShapeDtypeStruct((nseq * t, HEADS * HEAD_W), BF16)
    o_spec = pl.BlockSpec((t, HEAD_W), lambda b, h: (b, h))
    out_shape, out_specs = [o_shape, o_shape], [o_spec, o_spec]
    if emit_state:
        s_shape = jax.ShapeDtypeStruct((nseq, 2, HEADS, HEAD_W, HEAD_W), F32)
        s_spec = pl.BlockSpec((1, 2, 1, HEAD_W, HEAD_W), lambda b, h: (b, 0, h, 0, 0))
        out_shape += [s_shape, s_shape]
        out_specs += [s_spec, s_spec]
    seq = pltpu.VMEM((t, HEAD_W), F32)
    state = pltpu.VMEM((2, HEAD_W, HEAD_W), F32)
    return pl.pallas_call(
        functools.partial(_mixer_ab_kernel, t=t, ca=ca, cb=cb, layer=layer, has_past=past is not None,
                          emit_state=emit_state),
        out_shape=tuple(out_shape), grid=(nseq, HEADS), in_specs=in_specs, out_specs=tuple(out_specs),
        scratch_shapes=[pltpu.VMEM((t + 2 * SUBLANES, HEAD_W), F32), seq, seq, seq, seq, seq, seq, seq, state, state],
        compiler_params=_cparams(("arbitrary", "arbitrary")),
        name="mixer_ab",
    )(*args)


def _rms_head_pairs(x, w2):
    lane = lax.broadcasted_iota(jnp.int32, x.shape, 1)
    left = lane < C_HD
    sq = x * x
    s0 = jnp.sum(jnp.where(left, sq, 0.0), axis=-1, keepdims=True)
    s1 = jnp.sum(jnp.where(left, 0.0, sq), axis=-1, keepdims=True)
    ms = jnp.where(left, s0, s1) * (1.0 / C_HD)
    return x * lax.rsqrt(ms + NORM_EPS) * w2


def _rope_pairs(x, cos2, sin2):
    lane = lax.broadcasted_iota(jnp.int32, x.shape, 1)
    quarter = C_HD // 4
    swapped = jnp.where((lane & (2 * quarter - 1)) < quarter,
                        pltpu.roll(x, LANES - quarter, axis=1), pltpu.roll(x, quarter, axis=1))
    return x * cos2 + swapped * sin2


def _softmax_sink_av(scores, values, sink):
    m = sink
    for s in scores:
        m = jnp.maximum(m, jnp.max(s, axis=-1, keepdims=True))
    den = jnp.exp2(sink - m)
    acc = None
    for s, v in zip(scores, values):
        p = jnp.exp2(s - m)
        den = den + jnp.sum(p, axis=-1, keepdims=True)
        t = jnp.dot(p.astype(BF16), v, preferred_element_type=F32)
        acc = t if acc is None else acc + t
    return acc / den


def _attn_ctx_kernel(q_ref, k_ref, v_ref, qn_ref, kn_ref, sink_ref, o_ref, ko_ref, vo_ref, *, t):
    qw, kw = qn_ref[...], kn_ref[...]
    kn = _rms_head_pairs(k_ref[...].astype(F32), kw)
    ko_ref[0] = kn
    v = v_ref[...]
    vo_ref[0] = v.astype(F32)
    knb, vb = kn.astype(BF16), v
    scale = C_HD ** -0.5 * LOG2E
    for pair in range(C_QHEADS // 2):
        qp = (_rms_head_pairs(q_ref[:, pair * LANES:(pair + 1) * LANES].astype(F32), qw) * scale).astype(BF16)
        outs = []
        for half in range(2):
            hq = 2 * pair + half
            hk = hq // C_GROUP
            qh = qp[:, half * C_HD:(half + 1) * C_HD]
            s = _dot_nt(qh, knb[:, hk * C_HD:(hk + 1) * C_HD])
            sink = jnp.full((1, 1), sink_ref[hq] * LOG2E, F32)
            outs.append(_softmax_sink_av([s], [vb[:, hk * C_HD:(hk + 1) * C_HD]], sink))
        o_ref[:, pair * LANES:(pair + 1) * LANES] = jnp.concatenate(outs, axis=1).astype(o_ref.dtype)


def _attn_ctx_call(proj, q_norm2, k_norm2, sink, nseq, t):
    return pl.pallas_call(
        functools.partial(_attn_ctx_kernel, t=t),
        out_shape=(jax.ShapeDtypeStruct((nseq * t, C_QHEADS * C_HD), BF16),
                   jax.ShapeDtypeStruct((nseq, t, LANES), F32), jax.ShapeDtypeStruct((nseq, t, LANES), F32)),
        grid=(nseq,),
        in_specs=[pl.BlockSpec((t, C_QHEADS * C_HD), lambda b: (b, COL_QC // (C_QHEADS * C_HD))),
                  pl.BlockSpec((t, LANES), lambda b: (b, COL_KC // LANES)),
                  pl.BlockSpec((t, LANES), lambda b: (b, COL_VC // LANES)),
                  pl.BlockSpec((1, LANES), lambda b: (0, 0)), pl.BlockSpec((1, LANES), lambda b: (0, 0)),
                  pl.BlockSpec(memory_space=pltpu.SMEM)],
        out_specs=(pl.BlockSpec((t, C_QHEADS * C_HD), lambda b: (b, 0)),
                   pl.BlockSpec((1, t, LANES), lambda b: (b, 0, 0)), pl.BlockSpec((1, t, LANES), lambda b: (b, 0, 0))),
        compiler_params=_cparams(("arbitrary",)),
        name="attn_ctx",
    )(proj, proj, proj, q_norm2, k_norm2, sink)


def _attn_lat_kernel(q_ref, k_ref, v_ref, kc_ref, vc_ref, qn_ref, kn_ref, cos_ref, sin_ref, sink_ref, o_ref,
                     qs_scr, ks_scr, vs_scr, bias_scr, *, t, past_len):
    qw, kw = qn_ref[...], kn_ref[...]
    scale = C_HD ** -0.5 * LOG2E
    nrb = t // ROW_BLOCK
    blk = C_BLOCK

    ks_scr[0:blk, :] = jnp.zeros((blk, LANES), BF16)
    vs_scr[0:blk, :] = jnp.zeros((blk, LANES), BF16)
    ks_scr[t + blk:t + 2 * blk, :] = jnp.zeros((blk, LANES), BF16)
    vs_scr[t + blk:t + 2 * blk, :] = jnp.zeros((blk, LANES), BF16)

    grp_rows = C_GROUP * blk

    def prep(r, carry):
        r0 = pl.multiple_of(r * ROW_BLOCK, ROW_BLOCK)
        cos2, sin2 = cos_ref[pl.ds(r0, ROW_BLOCK), :], sin_ref[pl.ds(r0, ROW_BLOCK), :]
        kn = _rope_pairs(_rms_head_pairs(k_ref[pl.ds(r0, ROW_BLOCK), :].astype(F32), kw), cos2, sin2)
        ks_scr[pl.ds(r0 + blk, ROW_BLOCK), :] = kn.astype(BF16)
        vs_scr[pl.ds(r0 + blk, ROW_BLOCK), :] = v_ref[pl.ds(r0, ROW_BLOCK), :]
        for pair in range(C_QHEADS // 2):
            qp = _rms_head_pairs(q_ref[pl.ds(r0, ROW_BLOCK), pair * LANES:(pair + 1) * LANES].astype(F32), qw)
            qp = (_rope_pairs(qp, cos2, sin2) * scale).astype(BF16)
            for half in range(2):
                hq = 2 * pair + half
                hk, g = hq // C_GROUP, hq % C_GROUP
                for sub in range(ROW_BLOCK // blk):
                    dst = pl.multiple_of((r * (ROW_BLOCK // blk) + sub) * grp_rows + g * blk, blk)
                    qs_scr[hk, pl.ds(dst, blk), :] = qp[sub * blk:(sub + 1) * blk, half * C_HD:(half + 1) * C_HD]
        return carry

    lax.fori_loop(0, nrb, prep, 0)

    nkeys = 3 * blk + past_len
    kcb = kc_ref[0, 0].astype(BF16)
    vcb = vc_ref[0, 0].astype(BF16)
    qi = lax.broadcasted_iota(jnp.int32, (grp_rows, nkeys), 0) & (blk - 1)
    kj = lax.broadcasted_iota(jnp.int32, (grp_rows, nkeys), 1)
    visible = jnp.logical_or(kj >= 3 * blk, jnp.logical_and(kj >= qi, kj <= qi + 2 * C_WINDOW))
    bias_scr[...] = jnp.where(visible, 0.0, NEG_BIG)
    head_of_row = lax.broadcasted_iota(jnp.int32, (grp_rows, 1), 0) // blk
    kcol = lax.broadcasted_iota(jnp.int32, (1, nkeys), 1)

    def qblock(n, carry):
        r0 = pl.multiple_of(n * blk, blk)
        kpos = kcol + (r0 - blk)
        in_seq = jnp.logical_or(kcol >= 3 * blk, jnp.logical_and(kpos >= 0, kpos < t))
        edge = jnp.where(in_seq, 0.0, NEG_BIG)
        kwin = ks_scr[pl.ds(r0, 3 * blk), :]
        vwin = vs_scr[pl.ds(r0, 3 * blk), :]
        scores, vals, sinks = [], [], []
        for hk in range(C_KVHEADS):
            lo, hi = hk * C_HD, (hk + 1) * C_HD
            keys = jnp.concatenate([kwin[:, lo:hi], kcb[:, lo:hi]], axis=0)
            vals.append(jnp.concatenate([vwin[:, lo:hi], vcb[:, lo:hi]], axis=0))
            q_stack = qs_scr[hk, pl.ds(pl.multiple_of(n * grp_rows, grp_rows), grp_rows), :]
            scores.append(_dot_nt(q_stack, keys))
            sink = jnp.full((grp_rows, 1), sink_ref[hk * C_GROUP], F32)
            for g in range(1, C_GROUP):
                sink = jnp.where(head_of_row == g, sink_ref[hk * C_GROUP + g], sink)
            sinks.append(sink * LOG2E)
        outs = []
        for hk in range(C_KVHEADS):
            o = _softmax_sink_av([scores[hk] + bias_scr[...] + edge], [vals[hk]], sinks[hk])
            outs += [o[g * blk:(g + 1) * blk] for g in range(C_GROUP)]
        for pair in range(C_QHEADS // 2):
            o_ref[pl.ds(r0, blk), pair * LANES:(pair + 1) * LANES] = jnp.concatenate(
                outs[2 * pair:2 * pair + 2], axis=1).astype(o_ref.dtype)
        return carry

    lax.fori_loop(0, t // blk, qblock, 0)


def _attn_lat_call(proj, cache_k, cache_v, q_norm2, k_norm2, cos2, sin2, sink, layer, nseq, t):
    past_len = cache_k.shape[2]
    qw = C_QHEADS * C_HD
    return pl.pallas_call(
        functools.partial(_attn_lat_kernel, t=t, past_len=past_len),
        out_shape=jax.ShapeDtypeStruct((nseq * t, qw), BF16),
        grid=(nseq,),
        in_specs=[pl.BlockSpec((t, qw), lambda b: (b, COL_QC // qw)),
                  pl.BlockSpec((t, LANES), lambda b: (b, COL_KC // LANES)),
                  pl.BlockSpec((t, LANES), lambda b: (b, COL_VC // LANES)),
                  pl.BlockSpec((1, 1, past_len, LANES), lambda b: (b, layer, 0, 0)),
                  pl.BlockSpec((1, 1, past_len, LANES), lambda b: (b, layer, 0, 0)),
                  pl.BlockSpec((1, LANES), lambda b: (0, 0)), pl.BlockSpec((1, LANES), lambda b: (0, 0)),
                  _resident((t, LANES), lambda b: (0, 0)), _resident((t, LANES), lambda b: (0, 0)),
                  pl.BlockSpec(memory_space=pltpu.SMEM)],
        out_specs=pl.BlockSpec((t, qw), lambda b: (b, 0)),
        scratch_shapes=[pltpu.VMEM((C_KVHEADS, t * C_GROUP, C_HD), BF16),
                        pltpu.VMEM((t + 2 * C_BLOCK, LANES), BF16), pltpu.VMEM((t + 2 * C_BLOCK, LANES), BF16),
                        pltpu.VMEM((C_GROUP * C_BLOCK, 3 * C_BLOCK + past_len), F32)],
        compiler_params=_cparams(("arbitrary",)),
        name="attn_lat",
    )(proj, proj, proj, cache_k, cache_v, q_norm2, k_norm2, cos2, sin2, sink)


def _rope_tables(t):
    rows = t // GRID_W
    row = jnp.repeat(jnp.arange(rows, dtype=F32), GRID_W)
    col = jnp.tile(jnp.arange(GRID_W, dtype=F32), rows)
    nf = C_HD // 4
    inv = ROPE_THETA ** (-jnp.arange(nf, dtype=F32) / nf)
    ar, ac = row[:, None] * inv, col[:, None] * inv
    cos = jnp.concatenate([jnp.cos(ar), jnp.cos(ar), jnp.cos(ac), jnp.cos(ac)], axis=1)
    sin = jnp.concatenate([-jnp.sin(ar), jnp.sin(ar), -jnp.sin(ac), jnp.sin(ac)], axis=1)
    return jnp.tile(cos, (1, 2)), jnp.tile(sin, (1, 2))


def _merge_kernel(x_ref, g0_ref, g1_ref, g2_ref, oa_ref, ob_ref, oc_ref, mod_ref, wbr_ref, wout_ref, o_ref):
    merged = (_sigmoid(g0_ref[...].astype(F32)) * jnp.dot(oa_ref[...], wbr_ref[0], preferred_element_type=F32)
              + _sigmoid(g1_ref[...].astype(F32)) * jnp.dot(ob_ref[...], wbr_ref[1], preferred_element_type=F32)
              + _sigmoid(g2_ref[...].astype(F32)) * jnp.dot(oc_ref[...], wbr_ref[2], preferred_element_type=F32))
    res = jnp.dot(merged.astype(BF16), wout_ref[...], preferred_element_type=F32)
    o_ref[...] = x_ref[...] + mod_ref[0, 2:3, :] * res


def _merge_call(x2d, proj, oa, ob, oc, mod, w_br, w_out, tm, tiles_per_cond):
    m, d = x2d.shape
    mg = lambda r: pl.BlockSpec((tm, d), lambda i: (i, COL_MG // d + r))
    br = pl.BlockSpec((tm, BRANCH_W), lambda i: (i, 0))
    return pl.pallas_call(
        _merge_kernel,
        out_shape=jax.ShapeDtypeStruct((m, d), F32),
        grid=(m // tm,),
        in_specs=[pl.BlockSpec((tm, d), lambda i: (i, 0)), mg(0), mg(1), mg(2), br, br, br,
                  pl.BlockSpec((1, 6, d), lambda i: (i // tiles_per_cond, 0, 0)),
                  _resident((3, BRANCH_W, d), lambda i: (0, 0, 0)),
                  _resident((d, d), lambda i: (0, 0))],
        out_specs=pl.BlockSpec((tm, d), lambda i: (i, 0)),
        compiler_params=_cparams(("arbitrary",)),
        name="merge",
    )(x2d, proj, proj, proj, oa, ob, oc, mod, w_br, w_out)


FF_CHUNK = 256
HALO = BF16_ROWS


def _ffn_kernel(x_ref, xp_ref, xn_ref, mod_ref, nw_ref, wup_ref, cw_ref, wd_ref, o_ref, h_scr, act_scr, *,
                tm, seq_len):
    i = pl.program_id(0)
    nseg = max(1, tm // seq_len)
    seg = tm // nseg
    nw, sh, sc = nw_ref[...], mod_ref[0, 3:4, :], mod_ref[0, 4:5, :]

    def norm(x):
        return _rms_rows(x, nw) * (1.0 + sc) + sh

    has_prev = ((i * tm) & (seq_len - 1)) != 0
    has_next = (((i + 1) * tm) & (seq_len - 1)) != 0
    zero_halo = jnp.zeros((HALO, x_ref.shape[1]), BF16)
    for s in range(nseg):
        h_scr[s, HALO:HALO + seg, :] = norm(x_ref[s * seg:(s + 1) * seg, :]).astype(BF16)
        if s == 0:
            h_scr[s, 0:HALO, :] = (norm(xp_ref[...]) * jnp.where(has_prev, 1.0, 0.0)).astype(BF16)
        else:
            h_scr[s, 0:HALO, :] = zero_halo
        if s == nseg - 1:
            h_scr[s, HALO + seg:2 * HALO + seg, :] = (norm(xn_ref[...]) * jnp.where(has_next, 1.0, 0.0)).astype(BF16)
        else:
            h_scr[s, HALO + seg:2 * HALO + seg, :] = zero_halo

    def conv(u, cw):
        return (u[HALO - 1:HALO - 1 + seg] * cw[0:1, :] + u[HALO:HALO + seg] * cw[1:2, :]
                + u[HALO + 1:HALO + 1 + seg] * cw[2:3, :])

    for lo, hi in _col_chunks(D_FF, FF_CHUNK):
        for s in range(nseg):
            h = h_scr[s]
            a = conv(jnp.dot(h, wup_ref[:, lo:hi], preferred_element_type=F32), cw_ref[:, lo:hi])
            u = conv(jnp.dot(h, wup_ref[:, D_FF + lo:D_FF + hi], preferred_element_type=F32),
                     cw_ref[:, D_FF + lo:D_FF + hi])
            act_scr[s * seg:(s + 1) * seg, lo:hi] = (_silu(a) * u).astype(BF16)

    o_ref[...] = x_ref[...] + mod_ref[0, 5:6, :] * jnp.dot(act_scr[...], wd_ref[...], preferred_element_type=F32)


def _ffn_call(x2d, mod, nw, w_up, conv_w, w_down, tm, tiles_per_cond, seq_len):
    m, d = x2d.shape
    hb = tm // HALO
    last = m // HALO - 1
    nseg = max(1, tm // seq_len)
    seg = tm // nseg
    return pl.pallas_call(
        functools.partial(_ffn_kernel, tm=tm, seq_len=seq_len),
        out_shape=jax.ShapeDtypeStruct((m, d), F32),
        grid=(m // tm,),
        in_specs=[pl.BlockSpec((tm, d), lambda i: (i, 0)),
                  pl.BlockSpec((HALO, d), lambda i: (jnp.maximum(i * hb - 1, 0), 0)),
                  pl.BlockSpec((HALO, d), lambda i: (jnp.minimum((i + 1) * hb, last), 0)),
                  pl.BlockSpec((1, 6, d), lambda i: (i // tiles_per_cond, 0, 0)),
                  _resident((1, d), lambda i: (0, 0)),
                  _resident((d, 2 * D_FF), lambda i: (0, 0)),
                  _resident((3, 2 * D_FF), lambda i: (0, 0)),
                  _resident((D_FF, d), lambda i: (0, 0))],
        out_specs=pl.BlockSpec((tm, d), lambda i: (i, 0)),
        scratch_shapes=[pltpu.VMEM((nseg, seg + 2 * HALO, d), BF16), pltpu.VMEM((tm, D_FF), BF16)],
        compiler_params=_cparams(("arbitrary",)),
        name="ffn",
    )(x2d, x2d, x2d, mod, nw, w_up, conv_w, w_down)


def _permute_w_in(w):
    s = _SRC
    w16 = jnp.concatenate([w[:, s["mg"]:s["end"]], w[:, s["qa"]:s["beta"]], w[:, s["qb"]:s["fb"]],
                           w[:, s["gb"]:s["mg"]]], axis=1)
    n_gate = s["qb"] - s["beta"]
    w32 = jnp.concatenate([w[:, s["fb"]:s["gb"]], jnp.pad(w[:, s["beta"]:s["qb"]], ((0, 0), (0, LANES - n_gate)))],
                          axis=1)
    return w16.astype(BF16), w32.astype(BF16)


def _gate_rows(proj32, nseq, t, c):
    g = proj32[:, COL32_GATES:COL32_GATES + 4 * HEADS].reshape(nseq, t // c, c, 2, 2, HEADS)
    return jnp.transpose(g, (0, 5, 1, 3, 4, 2)).reshape(nseq, HEADS, t // c, 4, c)


def _row_tile(rows, t):
    tm = 512
    while rows % tm or (t % tm and tm % t):
        tm //= 2
    return tm


def _group_forward(x3d, mod_g, prm, past, tables):
    nseq, t, d = x3d.shape
    x = x3d.reshape(nseq * t, d)
    tm = _row_tile(nseq * t, t)
    tiles_per_cond = (nseq * t) // tm if mod_g.shape[1] == 1 else t // tm
    emit = past is None
    states_a, states_b, keys, vals = [], [], [], []
    for l in range(len(prm["w16"])):
        mod = mod_g[l]
        proj, proj32 = _in_proj_call(x, mod, prm["norm1_w"][l], prm["w16"][l], prm["w32"][l], tm, tiles_per_cond)
        gates_r = _gate_rows(proj32, nseq, t, min(CHUNK_A, t))
        res_ab = _mixer_ab_call(proj, proj32, gates_r, prm, past, l, nseq, t, emit)
        if emit:
            oc, kn, vn = _attn_ctx_call(proj, prm["q_norm2"][l], prm["k_norm2"][l], prm["sink"][l], nseq, t)
            states_a.append(res_ab[2])
            states_b.append(res_ab[3])
            keys.append(kn.reshape(nseq, t, C_KVHEADS, C_HD))
            vals.append(vn.reshape(nseq, t, C_KVHEADS, C_HD))
        else:
            oc = _attn_lat_call(proj, past[2], past[3], prm["q_norm2"][l], prm["k_norm2"][l], tables[0], tables[1],
                                prm["sink"][l], l, nseq, t)
        x = _merge_call(x, proj, res_ab[0], res_ab[1], oc, mod, prm["w_branch"][l], prm["w_out"][l], tm, tiles_per_cond)
        x = _ffn_call(x, mod, prm["norm2_w"][l], prm["w_up"][l], prm["conv_ffn"][l], prm["w_down"][l],
                      tm, tiles_per_cond, t)
    return x.reshape(nseq, t, d), states_a, states_b, keys, vals


def kernel(x_prompt, x_sample, state_delta, state_hgrn, cache_k, cache_v, c, c_ctx, ada_w, ada_b, norm1_w, w_in, conv_a, a_log, dt_bias, norm_a, lb_logits, norm_b, q_norm, k_norm, sink, w_branch, w_out, norm2_w, w_up, conv_ffn, w_down):
    depth = w_in.shape[0]
    d = x_prompt.shape[-1]
    n_lat = c.shape[0]

    cond = jnp.concatenate([c_ctx[None, :], c], axis=0)
    rows = -(-cond.shape[0] // SUBLANES) * SUBLANES
    cond = jnp.pad(cond, ((0, rows - cond.shape[0]), (0, 0)))
    mod_all = _mod_call(cond, ada_w, ada_b).reshape(depth, rows, 6, d)

    perm = [_permute_w_in(w_in[l]) for l in range(depth)]
    prm = dict(
        w16=[p[0] for p in perm], w32=[p[1] for p in perm],
        norm1_w=norm1_w.reshape(depth, 1, d), norm2_w=norm2_w.reshape(depth, 1, d),
        conv_a=conv_a, a_log=a_log, dt_bias=dt_bias, norm_a=norm_a.reshape(depth, 1, HEAD_W),
        lb_logits=lb_logits, norm_b=norm_b.reshape(depth, 1, HEAD_W),
        q_norm2=jnp.tile(q_norm, (1, 2)).reshape(depth, 1, LANES), k_norm2=jnp.tile(k_norm, (1, 2)).reshape(depth, 1, LANES),
        sink=sink, w_branch=w_branch.astype(BF16), w_out=w_out.astype(BF16),
        w_up=w_up.astype(BF16), conv_ffn=conv_ffn, w_down=w_down.astype(BF16))

    y_prompt, st_a, st_b, keys, vals = _group_forward(x_prompt, mod_all[:, 0:1], prm, None, None)

    past_len = cache_k.shape[2]
    past = (state_delta, state_hgrn,
            cache_k.reshape(cache_k.shape[0], depth, past_len, C_KVHEADS * C_HD),
            cache_v.reshape(cache_v.shape[0], depth, past_len, C_KVHEADS * C_HD))
    y_sample, _, _, _, _ = _group_forward(x_sample, mod_all[:, 1:1 + n_lat], prm, past, _rope_tables(x_sample.shape[1]))

    return (y_prompt, y_sample, jnp.stack(st_a, axis=1), jnp.stack(st_b, axis=1),
            jnp.stack(keys, axis=1), jnp.stack(vals, axis=1))
```

```python
import functools
import math

import numpy as np
import jax
import jax.numpy as jnp
from jax import lax
from jax.experimental import pallas as pl
from jax.experimental.pallas import tpu as pltpu

F32 = jnp.float32
BF16 = jnp.bfloat16

D_MODEL = 1024
NORM_EPS = 1e-6
LB_EPS = 1e-6
NEG_BIG = -1e30
LOG2E = 1.4426950408889634
GRID_W = 64
ROPE_THETA = 10000.0

HEADS = 4
HEAD_W = 128
SHORT_CONV = 5
C_QHEADS = 8
C_KVHEADS = 2
C_GROUP = C_QHEADS // C_KVHEADS
C_HD = 64
C_WINDOW = 128
C_BLOCK = 128
BRANCH_W = 512
D_FF = 2816

LANES = 128
SUBLANES = 8
BF16_ROWS = 16
VMEM_LIMIT = 56 * 1024 * 1024

COL_MG = 0
COL_QA = 3072
COL_KA = 3584
COL_VA = 4096
COL_GA = 4608
COL_QB = 5120
COL_IB = 5632
COL_GB = 6144
COL_QC = 6656
COL_KC = 7168
COL_VC = 7296
PROJ_W = 7424
COL32_FB = 0
COL32_GATES = 1024
PROJ32_W = 1152
_SRC = dict(qa=0, ka=512, va=1024, ga=1536, beta=2048, alpha=2056, qb=2064, ib=2576, fb=3088, gb=4112,
            qc=4624, kc=5136, vc=5264, mg=5392, end=8464)

CHUNK_A = 128
ROUNDS_A = 8
CHUNK_B = 64
TRI_BASE = 16
TRI_MERGE = 2
MM_TILE = 768


def _cparams(sem):
    return pltpu.CompilerParams(dimension_semantics=sem, vmem_limit_bytes=VMEM_LIMIT)


def _resident(shape, index_map):
    return pl.BlockSpec(shape, index_map, pipeline_mode=pl.Buffered(1))


def _layer_block(shape, layer):
    return _resident((None,) + tuple(shape), lambda *_: (layer,) + (0,) * len(shape))


def _mod_block(d, layer, cond):
    return pl.BlockSpec((None, 1, 6, d), lambda i: (layer, cond(i), 0, 0))


def _dot(a, b):
    return jnp.dot(a.astype(BF16), b.astype(BF16), preferred_element_type=F32)


def _dot_nt(a, b):
    return lax.dot_general(a.astype(BF16), b.astype(BF16), (((1,), (1,)), ((), ())), preferred_element_type=F32)


def _dot_tn(a, b):
    return lax.dot_general(a.astype(BF16), b.astype(BF16), (((0,), (0,)), ((), ())), preferred_element_type=F32)


def _sigmoid(x):
    return 1.0 / (1.0 + jnp.exp(-x))


def _silu(x):
    return x * _sigmoid(x)


def _softplus(x):
    return jnp.maximum(x, 0.0) + jnp.log(1.0 + jnp.exp(-jnp.abs(x)))


def _rms_rows(x, w):
    ms = jnp.mean(x * x, axis=-1, keepdims=True)
    return x * lax.rsqrt(ms + NORM_EPS) * w


def _split_bf16(x, n):
    parts, r = [], x
    for _ in range(n):
        p = r.astype(BF16)
        parts.append(p)
        r = r - p.astype(F32)
    return parts


def _col_chunks(width, step):
    return [(lo, min(lo + step, width)) for lo in range(0, width, step)]


def _mod_kernel(c_ref, w_ref, b_ref, o_ref):
    c = c_ref[...]
    o_ref[0] = _dot(_silu(c), w_ref[0]) + b_ref[0]


def _mod_call(cond, ada_w, ada_b):
    depth, d, n = ada_w.shape
    rows = cond.shape[0]
    tn = 768
    return pl.pallas_call(
        _mod_kernel,
        out_shape=jax.ShapeDtypeStruct((depth, rows, n), F32),
        grid=(depth, n // tn),
        in_specs=[pl.BlockSpec((rows, d), lambda l, j: (0, 0)),
                  pl.BlockSpec((1, d, tn), lambda l, j: (l, 0, j)),
                  pl.BlockSpec((1, 1, tn), lambda l, j: (l, 0, j))],
        out_specs=pl.BlockSpec((1, rows, tn), lambda l, j: (l, 0, j)),
        compiler_params=_cparams(("arbitrary", "arbitrary")),
        name="mod",
    )(cond, ada_w, ada_b.reshape(depth, 1, n))


def _in_proj_kernel(x_ref, mod_ref, nw_ref, w16_ref, w32_ref, o16_ref, o32_ref):
    h = _rms_rows(x_ref[...], nw_ref[...]) * (1.0 + mod_ref[0, 1:2, :]) + mod_ref[0, 0:1, :]
    hb = h.astype(BF16)
    for lo, hi in _col_chunks(PROJ_W, MM_TILE):
        o16_ref[:, lo:hi] = jnp.dot(hb, w16_ref[:, lo:hi], preferred_element_type=F32).astype(BF16)
    for lo, hi in _col_chunks(PROJ32_W, MM_TILE):
        o32_ref[:, lo:hi] = jnp.dot(hb, w32_ref[:, lo:hi], preferred_element_type=F32)


def _in_proj_call(x2d, prm, layer, cond, tm):
    m, d = x2d.shape
    return pl.pallas_call(
        _in_proj_kernel,
        out_shape=(jax.ShapeDtypeStruct((m, PROJ_W), BF16), jax.ShapeDtypeStruct((m, PROJ32_W), F32)),
        grid=(m // tm,),
        in_specs=[pl.BlockSpec((tm, d), lambda i: (i, 0)), _mod_block(d, layer, cond),
                  _layer_block((1, d), layer), _layer_block((d, PROJ_W), layer), _layer_block((d, PROJ32_W), layer)],
        out_specs=(pl.BlockSpec((tm, PROJ_W), lambda i: (i, 0)),
                   pl.BlockSpec((tm, PROJ32_W), lambda i: (i, 0))),
        compiler_params=_cparams(("arbitrary",)),
        name="in_proj",
    )(x2d, prm["mod"], prm["norm1_w"], prm["w16"], prm["w32"])


ROW_BLOCK = 256


def _gated_norm_epilogue(of_scr, ob_scr, gate_ref, nw_ref, o_ref, t):
    nw = nw_ref[...]

    def body(r, carry):
        r0 = pl.multiple_of(r * ROW_BLOCK, ROW_BLOCK)
        o = of_scr[pl.ds(r0, ROW_BLOCK), :] + ob_scr[pl.ds(r0, ROW_BLOCK), :]
        y = _rms_rows(o, nw) * _silu(gate_ref[pl.ds(r0, ROW_BLOCK), :].astype(F32))
        o_ref[pl.ds(r0, ROW_BLOCK), :] = y.astype(o_ref.dtype)
        return carry

    lax.fori_loop(0, t // ROW_BLOCK, body, 0)


def _tri_inverse(mats, ri, ci, c, tick):
    n = range(len(mats))
    shift = int(np.log2(TRI_BASE))
    base = (ri >> shift) == (ci >> shift)
    mb = [jnp.where(base, m, 0.0) for m in mats]
    y = [-m for m in mb]
    p = [_dot(m, m) for m in mb]
    tick()
    for _ in range(shift - 2):
        yp = [_dot(jnp.concatenate([y[i], p[i]], axis=0), p[i]) for i in n]
        tick()
        y = [y[i] + p[i] + yp[i][:c] for i in n]
        p = [r[c:] for r in yp]
    yp = [_dot(y[i], p[i]) for i in n]
    tick()
    y = [y[i] + p[i] + yp[i] for i in n]
    total = int(np.log2(c))
    while shift < total:
        factors = min(TRI_MERGE, total - shift)
        inner = (ri >> shift) == (ci >> shift)
        outer = (ri >> (shift + factors)) == (ci >> (shift + factors))
        between = jnp.logical_and(outer, jnp.logical_not(inner))
        cm = [jnp.where(between, m, 0.0) for m in mats]
        w = [_dot(y[i], cm[i]) for i in n]
        tick()
        w = [cm[i] + w[i] for i in n]
        if factors == 1:
            wy = [_dot(w[i], y[i]) for i in n]
            tick()
            y = [y[i] - w[i] - wy[i] for i in n]
        else:
            r = [_dot(w[i], jnp.concatenate([w[i], y[i]], axis=1)) for i in n]
            tick()
            p, y = [x[:, :c] for x in r], [y[i] - w[i] - r[i][:, c:] for i in n]
            for f in range(1, factors):
                if f + 1 < factors:
                    r = [_dot(p[i], jnp.concatenate([p[i], y[i]], axis=1)) for i in n]
                    tick()
                    p, y = [x[:, :c] for x in r], [y[i] + p[i] + r[i][:, c:] for i in n]
                else:
                    py = [_dot(p[i], y[i]) for i in n]
                    tick()
                    y = [y[i] + p[i] + py[i] for i in n]
        shift += factors
    eye = jnp.where(ri == ci, 1.0, 0.0)
    return [eye + v for v in y]


def _delta_chunks(probs, states, c, tick):
    n = range(len(probs))
    ri = lax.broadcasted_iota(jnp.int32, (c, c), 0)
    ci = lax.broadcasted_iota(jnp.int32, (c, c), 1)
    eye = ri == ci
    pre = []
    for q, k, v, beta_row, alpha_row, a_neg, dt_b, chain, reverse in probs:
        incl, strict = (ri <= ci, ri < ci) if reverse else (ri >= ci, ri > ci)
        beta_r = _sigmoid(beta_row)
        g_r = a_neg * _softplus(alpha_row + dt_b)
        gc_col = jnp.sum(jnp.where(incl, jnp.broadcast_to(g_r, (c, c)), 0.0), axis=1, keepdims=True)
        beta_col = jnp.sum(jnp.where(eye, jnp.broadcast_to(beta_r, (c, c)), 0.0), axis=1, keepdims=True)
        gc_row = jnp.sum(jnp.where(eye, jnp.broadcast_to(gc_col, (c, c)), 0.0), axis=0, keepdims=True)
        g_tot = jnp.sum(g_r, axis=1, keepdims=True)
        decay = jnp.where(incl, jnp.exp(jnp.where(incl, gc_col - gc_row, 0.0)), 0.0)
        pre.append((strict, gc_col, beta_col, g_tot, decay, jnp.exp(gc_col)))

    kb = [pr[1].astype(BF16) for pr in probs]
    kq = [_dot_nt(jnp.concatenate([kb[i], probs[i][0].astype(BF16)], axis=0), kb[i]) for i in n]
    tick()
    t_inv = _tri_inverse([jnp.where(pre[i][0], kq[i][:c] * pre[i][2] * pre[i][4], 0.0) for i in n], ri, ci, c, tick)
    uw = [_dot(t_inv[i], jnp.concatenate([probs[i][2] * pre[i][2], probs[i][1] * (pre[i][2] * pre[i][5])], axis=1))
          for i in n]
    tick()
    wq_lhs = [jnp.concatenate([uw[i][:, HEAD_W:], probs[i][0] * pre[i][5]], axis=0).astype(BF16) for i in n]
    os_lhs = [jnp.concatenate([kq[i][c:] * pre[i][4], (probs[i][1] * jnp.exp(pre[i][3] - pre[i][1])).T],
                              axis=0).astype(BF16) for i in n]
    states = list(states)
    outs = [None] * len(probs)
    todo = list(n)
    while todo:
        front, seen = [], set()
        for i in todo:
            if probs[i][7] not in seen:
                seen.add(probs[i][7])
                front.append(i)
        todo = [i for i in todo if i not in front]
        wq = [_dot(wq_lhs[i], states[probs[i][7]]) for i in front]
        tick()
        v_new = [uw[i][:, :HEAD_W] - wq[j][:c] for j, i in enumerate(front)]
        os_ = [_dot(os_lhs[i], v_new[j]) for j, i in enumerate(front)]
        tick()
        for j, i in enumerate(front):
            outs[i] = wq[j][c:] + os_[j][:c]
            states[probs[i][7]] = states[probs[i][7]] * jnp.exp(pre[i][3]) + os_[j][c:]
    return outs, states


def _conv_silu_pass(x_ref, cw_ref, xp_scr, dst_scr, t, l2, scale):
    nrb = t // ROW_BLOCK
    pad = SUBLANES
    half = SHORT_CONV // 2

    def cp(r, carry):
        r0 = pl.multiple_of(r * ROW_BLOCK, ROW_BLOCK)
        xp_scr[pl.ds(r0 + pad, ROW_BLOCK), :] = x_ref[pl.ds(r0, ROW_BLOCK), :].astype(F32)
        return carry

    lax.fori_loop(0, nrb, cp, 0)
    cw = cw_ref[...]

    def body(r, carry):
        r0 = pl.multiple_of(r * ROW_BLOCK, ROW_BLOCK)
        y = None
        for j in range(SHORT_CONV):
            term = xp_scr[pl.ds(r0 + (pad - half + j), ROW_BLOCK), :] * cw[j:j + 1, :]
            y = term if y is None else y + term
        y = _silu(y)
        if l2:
            y = y * (lax.rsqrt(jnp.sum(y * y, axis=-1, keepdims=True) + NORM_EPS) * scale)
        dst_scr[pl.ds(r0, ROW_BLOCK), :] = y
        return carry

    lax.fori_loop(0, nrb, body, 0)


def _hgrn_diag(q, kf, v, b, reverse, c):
    nb = c // SUBLANES
    q3, k3, v3, b3 = (a.reshape(nb, SUBLANES, HEAD_W) for a in (q, kf, v, b))
    sub = lax.broadcasted_iota(jnp.int32, (nb, SUBLANES, HEAD_W), 1)
    o3 = jnp.zeros((nb, SUBLANES, HEAD_W), F32)
    for j in range(SUBLANES):
        mask = (sub <= j) if reverse else (sub >= j)
        e = jnp.exp2(jnp.where(mask, b3 - b3[:, j:j + 1, :], NEG_BIG))
        a = jnp.sum(q3 * e * k3[:, j:j + 1, :], axis=-1, keepdims=True)
        o3 = o3 + a * v3[:, j:j + 1, :]
    return o3.reshape(c, HEAD_W)


def _hgrn_chunk(q, kf, v, lf, st, reverse, c):
    ri = lax.broadcasted_iota(jnp.int32, (c, c), 0)
    ci = lax.broadcasted_iota(jnp.int32, (c, c), 1)
    incl = (ri <= ci) if reverse else (ri >= ci)
    parts = jnp.concatenate(_split_bf16(lf, 3), axis=1)
    b3 = jnp.dot(jnp.where(incl, 1.0, 0.0).astype(BF16), parts, preferred_element_type=F32)
    b = b3[:, :HEAD_W] + b3[:, HEAD_W:2 * HEAD_W] + b3[:, 2 * HEAD_W:]
    b_tot = jnp.sum(lf, axis=0, keepdims=True)
    o = _dot_nt(q * jnp.exp2(b), st)
    row = lax.broadcasted_iota(jnp.int32, (c, 1), 0)
    att = jnp.zeros((c, c), F32)
    n = SUBLANES
    while n < c:
        pieces = []
        for g in range(c // (2 * n)):
            r = g * 2 * n + (n if reverse else n - 1)
            pieces.append(jnp.broadcast_to(b[r:r + 1, :], (2 * n, HEAD_W)))
        ref = pieces[0] if len(pieces) == 1 else jnp.concatenate(pieces, axis=0)
        e = jnp.exp2(-jnp.abs(b - ref))
        s = int(np.log2(n))
        second = ((row >> s) & 1) == 1
        q_part = jnp.logical_not(second) if reverse else second
        qt = jnp.where(q_part, q * e, 0.0)
        kt = jnp.where(q_part, 0.0, kf * e)
        att = att + jnp.where((ri >> (s + 1)) == (ci >> (s + 1)), _dot_nt(qt, kt), 0.0)
        n *= 2
    o = o + _dot(att, v) + _hgrn_diag(q, kf, v, b, reverse, c)
    kh = kf * jnp.exp2(b_tot - b)
    st_new = st * jnp.exp2(b_tot) + _dot_tn(v, kh)
    return o, st_new


def _mixer_ab_kernel(*refs, t, ca, cb, layer, has_past, emit_state):
    refs = list(refs)
    (qa_ref, ka_ref, va_ref, ga_ref, gr_ref, cwq_ref, cwk_ref, cwv_ref, alog_ref, dtb_ref, nwa_ref,
     qb_ref, ib_ref, f0_ref, f1_ref, gb_ref, lb_ref, nwb_ref) = refs[:18]
    pos = 18
    sa0_ref = sb0_ref = None
    if has_past:
        sa0_ref, sb0_ref = refs[pos:pos + 2]
        pos += 2
    oa_ref, ob_ref = refs[pos:pos + 2]
    pos += 2
    sfa_ref = sfb_ref = None
    if emit_state:
        sfa_ref, sfb_ref = refs[pos:pos + 2]
        pos += 2
    xp_scr, qn_scr, kn_scr, vn_scr, af_scr, ab_scr, bf_scr, bb_scr, sa_scr, sb_scr = refs[pos:]

    h = pl.program_id(1)
    pad = SUBLANES
    xp_scr[0:pad, :] = jnp.zeros((pad, HEAD_W), F32)
    xp_scr[t + pad:t + 2 * pad, :] = jnp.zeros((pad, HEAD_W), F32)
    _conv_silu_pass(qa_ref, cwq_ref, xp_scr, qn_scr, t, True, HEAD_W ** -0.5)
    _conv_silu_pass(ka_ref, cwk_ref, xp_scr, kn_scr, t, True, 1.0)
    _conv_silu_pass(va_ref, cwv_ref, xp_scr, vn_scr, t, False, 1.0)

    if has_past:
        for d in range(2):
            sa_scr[d] = sa0_ref[0, 0, d, 0]
            sb_scr[d] = sb0_ref[0, 0, d, 0].T
    else:
        sa_scr[...] = jnp.zeros(sa_scr.shape, F32)
        sb_scr[...] = jnp.zeros(sb_scr.shape, F32)

    ones = jnp.ones((1, ca), F32)
    a_neg = [-jnp.exp(ones * alog_ref[layer, d, h]) for d in range(2)]
    dt_b = [dtb_ref[layer, d, h] for d in range(2)]

    lb_terms = None
    if layer > 0:
        lb_terms = []
        for d in range(2):
            lg = lb_ref[d]
            ex = jnp.exp(lg - jnp.max(lg, axis=0, keepdims=True))
            pr = ex / jnp.sum(ex, axis=0, keepdims=True)
            lb = jnp.clip(jnp.sum(pr[1:layer + 1], axis=0, keepdims=True), LB_EPS, 1.0 - LB_EPS)
            lb_terms.append((jnp.log(lb) * LOG2E, jnp.log1p(-lb) * LOG2E, 1.0 - lb))

    f_refs = (f0_ref, f1_ref)
    nca, ncb = t // ca, t // cb
    rounds = math.gcd(ROUNDS_A, nca)
    ratio = rounds * ca // cb

    def hgrn_gates(z, d):
        z2 = z * LOG2E
        y = jnp.exp2(-jnp.abs(z2))
        one_y = 1.0 + y
        l2_sig = jnp.minimum(z2, 0.0) - jnp.log(one_y) * LOG2E
        sig_neg = jnp.where(z >= 0.0, y, 1.0) / one_y
        if layer == 0:
            return l2_sig, sig_neg
        l2_lb, l2_1m_lb, one_m_lb = lb_terms[d]
        a2 = l2_1m_lb + l2_sig
        return (jnp.maximum(l2_lb, a2) + jnp.log(1.0 + jnp.exp2(-jnp.abs(l2_lb - a2))) * LOG2E,
                one_m_lb * sig_neg)

    def step(n, carry):
        a_rows, a_probs = [], []
        for rnd in range(rounds):
            for d in range(2):
                cidx = n * rounds + rnd if d == 0 else nca - 1 - (n * rounds + rnd)
                r0 = pl.multiple_of(cidx * ca, ca)
                gr = gr_ref[0, 0, cidx]
                a_rows.append(r0)
                a_probs.append((qn_scr[pl.ds(r0, ca), :], kn_scr[pl.ds(r0, ca), :], vn_scr[pl.ds(r0, ca), :],
                                gr[d:d + 1, :], gr[2 + d:3 + d, :], a_neg[d], dt_b[d], d, d == 1))
        sa = [sa_scr[d] for d in range(2)]
        b_rows, b_in = [], []
        for j in range(ratio):
            m = n * ratio + j
            rows = [pl.multiple_of((m if d == 0 else ncb - 1 - m) * cb, cb) for d in range(2)]
            b_rows.append(rows)
            b_in.append([(f_refs[d][pl.ds(rows[d], cb), :], qb_ref[pl.ds(rows[d], cb), :],
                          ib_ref[pl.ds(rows[d], cb), :]) for d in range(2)])
        sb = [sb_scr[d] for d in range(2)]

        b_out = [[None, None] for _ in range(ratio)]
        pending = [(j, d) for j in range(ratio) for d in range(2)]

        def tick():
            if pending:
                j, d = pending.pop(0)
                z, qraw, iraw = b_in[j][d]
                lf, kf = hgrn_gates(z, d)
                b_out[j][d], sb[d] = _hgrn_chunk(_silu(qraw.astype(F32)), kf, iraw.astype(F32), lf, sb[d],
                                                 d == 1, cb)

        a_out, sa = _delta_chunks(a_probs, sa, ca, tick)
        while pending:
            tick()

        for i, prob in enumerate(a_probs):
            (af_scr if prob[7] == 0 else ab_scr)[pl.ds(a_rows[i], ca), :] = a_out[i]
        for d in range(2):
            sa_scr[d] = sa[d]
            sb_scr[d] = sb[d]
            for j in range(ratio):
                (bf_scr if d == 0 else bb_scr)[pl.ds(b_rows[j][d], cb), :] = b_out[j][d]
        return carry

    lax.fori_loop(0, nca // rounds, step, 0)

    _gated_norm_epilogue(af_scr, ab_scr, ga_ref, nwa_ref, oa_ref, t)
    _gated_norm_epilogue(bf_scr, bb_scr, gb_ref, nwb_ref, ob_ref, t)
    if emit_state:
        for d in range(2):
            sfa_ref[0, d, 0] = sa_scr[d]
            sfb_ref[0, d, 0] = sb_scr[d].T


def _mixer_ab_call(proj, proj32, gates_r, prm, past, layer, nseq, t, emit_state):
    ca, cb = min(CHUNK_A, t), min(CHUNK_B, t)
    depth = prm["lb_logits"].shape[1]
    col = lambda off: (lambda b, h: (b, off // HEAD_W + h))
    seq_in = lambda off: pl.BlockSpec((t, HEAD_W), col(off))
    conv = lambda part: pl.BlockSpec((None, SHORT_CONV, HEAD_W), lambda b, h: (layer, 0, part * HEADS + h))
    smem = pl.BlockSpec(memory_space=pltpu.SMEM)
    norm = _layer_block((1, HEAD_W), layer)
    in_specs = [seq_in(COL_QA), seq_in(COL_KA), seq_in(COL_VA), seq_in(COL_GA),
                pl.BlockSpec((1, 1, t // ca, 4, ca), lambda b, h: (b, h, 0, 0, 0)),
                conv(0), conv(1), conv(2), smem, smem, norm,
                seq_in(COL_QB), seq_in(COL_IB), seq_in(COL32_FB), seq_in(COL32_FB + HEADS * HEAD_W), seq_in(COL_GB),
                pl.BlockSpec((2, depth, HEAD_W), lambda b, h: (0, 0, h)), norm]
    args = [proj, proj, proj, proj, gates_r, prm["conv_a"], prm["conv_a"], prm["conv_a"],
            prm["a_log"], prm["dt_bias"], prm["norm_a"],
            proj, proj, proj32, proj32, proj, prm["lb_logits"], prm["norm_b"]]
    state_in = pl.BlockSpec((1, 1, 2, 1, HEAD_W, HEAD_W), lambda b, h: (b, layer, 0, h, 0, 0))
    if past is not None:
        in_specs += [state_in, state_in]
        args += [past[0], past[1]]
    o_shape = jax.ShapeDtypeStruct((nseq * t, HEADS * HEAD_W), BF16)
    o_spec = pl.BlockSpec((t, HEAD_W), lambda b, h: (b, h))
    out_shape, out_specs = [o_shape, o_shape], [o_spec, o_spec]
    if emit_state:
        s_shape = jax.ShapeDtypeStruct((nseq, 2, HEADS, HEAD_W, HEAD_W), F32)
        s_spec = pl.BlockSpec((1, 2, 1, HEAD_W, HEAD_W), lambda b, h: (b, 0, h, 0, 0))
        out_shape += [s_shape, s_shape]
        out_specs += [s_spec, s_spec]
    seq = pltpu.VMEM((t, HEAD_W), F32)
    state = pltpu.VMEM((2, HEAD_W, HEAD_W), F32)
    return pl.pallas_call(
        functools.partial(_mixer_ab_kernel, t=t, ca=ca, cb=cb, layer=layer, has_past=past is not None,
                          emit_state=emit_state),
        out_shape=tuple(out_shape), grid=(nseq, HEADS), in_specs=in_specs, out_specs=tuple(out_specs),
        scratch_shapes=[pltpu.VMEM((t + 2 * SUBLANES, HEAD_W), F32), seq, seq, seq, seq, seq, seq, seq, state, state],
        compiler_params=_cparams(("arbitrary", "arbitrary")),
        name="mixer_ab",
    )(*args)


def _rms_head_pairs(x, w2):
    lane = lax.broadcasted_iota(jnp.int32, x.shape, 1)
    left = lane < C_HD
    sq = x * x
    s0 = jnp.sum(jnp.where(left, sq, 0.0), axis=-1, keepdims=True)
    s1 = jnp.sum(jnp.where(left, 0.0, sq), axis=-1, keepdims=True)
    ms = jnp.where(left, s0, s1) * (1.0 / C_HD)
    return x * lax.rsqrt(ms + NORM_EPS) * w2


def _rope_pairs(x, cos2, sin2):
    lane = lax.broadcasted_iota(jnp.int32, x.shape, 1)
    quarter = C_HD // 4
    swapped = jnp.where((lane & (2 * quarter - 1)) < quarter,
                        pltpu.roll(x, LANES - quarter, axis=1), pltpu.roll(x, quarter, axis=1))
    return x * cos2 + swapped * sin2


def _softmax_sink_av(scores, values, sink):
    m = sink
    for s in scores:
        m = jnp.maximum(m, jnp.max(s, axis=-1, keepdims=True))
    den = jnp.exp2(sink - m)
    acc = None
    for s, v in zip(scores, values):
        p = jnp.exp2(s - m)
        den = den + jnp.sum(p, axis=-1, keepdims=True)
        t = jnp.dot(p.astype(BF16), v, preferred_element_type=F32)
        acc = t if acc is None else acc + t
    return acc / den


def _attn_ctx_kernel(q_ref, k_ref, v_ref, qn_ref, kn_ref, sink_ref, o_ref, ko_ref, vo_ref, *, t, layer):
    qw, kw = qn_ref[...], kn_ref[...]
    kn = _rms_head_pairs(k_ref[...].astype(F32), kw)
    ko_ref[0] = kn
    v = v_ref[...]
    vo_ref[0] = v.astype(F32)
    knb, vb = kn.astype(BF16), v
    scale = C_HD ** -0.5 * LOG2E
    for pair in range(C_QHEADS // 2):
        qp = (_rms_head_pairs(q_ref[:, pair * LANES:(pair + 1) * LANES].astype(F32), qw) * scale).astype(BF16)
        outs = []
        for half in range(2):
            hq = 2 * pair + half
            hk = hq // C_GROUP
            qh = qp[:, half * C_HD:(half + 1) * C_HD]
            s = _dot_nt(qh, knb[:, hk * C_HD:(hk + 1) * C_HD])
            sink = jnp.full((1, 1), sink_ref[layer, hq] * LOG2E, F32)
            outs.append(_softmax_sink_av([s], [vb[:, hk * C_HD:(hk + 1) * C_HD]], sink))
        o_ref[:, pair * LANES:(pair + 1) * LANES] = jnp.concatenate(outs, axis=1).astype(o_ref.dtype)


def _attn_ctx_call(proj, prm, layer, nseq, t):
    return pl.pallas_call(
        functools.partial(_attn_ctx_kernel, t=t, layer=layer),
        out_shape=(jax.ShapeDtypeStruct((nseq * t, C_QHEADS * C_HD), BF16),
                   jax.ShapeDtypeStruct((nseq, t, LANES), F32), jax.ShapeDtypeStruct((nseq, t, LANES), F32)),
        grid=(nseq,),
        in_specs=[pl.BlockSpec((t, C_QHEADS * C_HD), lambda b: (b, COL_QC // (C_QHEADS * C_HD))),
                  pl.BlockSpec((t, LANES), lambda b: (b, COL_KC // LANES)),
                  pl.BlockSpec((t, LANES), lambda b: (b, COL_VC // LANES)),
                  _layer_block((1, LANES), layer), _layer_block((1, LANES), layer),
                  pl.BlockSpec(memory_space=pltpu.SMEM)],
        out_specs=(pl.BlockSpec((t, C_QHEADS * C_HD), lambda b: (b, 0)),
                   pl.BlockSpec((1, t, LANES), lambda b: (b, 0, 0)), pl.BlockSpec((1, t, LANES), lambda b: (b, 0, 0))),
        compiler_params=_cparams(("arbitrary",)),
        name="attn_ctx",
    )(proj, proj, proj, prm["q_norm2"], prm["k_norm2"], prm["sink"])


def _attn_lat_kernel(q_ref, k_ref, v_ref, kc_ref, vc_ref, qn_ref, kn_ref, cos_ref, sin_ref, sink_ref, o_ref,
                     qs_scr, ks_scr, vs_scr, bias_scr, *, t, past_len, layer):
    qw, kw = qn_ref[...], kn_ref[...]
    scale = C_HD ** -0.5 * LOG2E
    nrb = t // ROW_BLOCK
    blk = C_BLOCK

    ks_scr[0:blk, :] = jnp.zeros((blk, LANES), BF16)
    vs_scr[0:blk, :] = jnp.zeros((blk, LANES), BF16)
    ks_scr[t + blk:t + 2 * blk, :] = jnp.zeros((blk, LANES), BF16)
    vs_scr[t + blk:t + 2 * blk, :] = jnp.zeros((blk, LANES), BF16)

    grp_rows = C_GROUP * blk

    def prep(r, carry):
        r0 = pl.multiple_of(r * ROW_BLOCK, ROW_BLOCK)
        cos2, sin2 = cos_ref[pl.ds(r0, ROW_BLOCK), :], sin_ref[pl.ds(r0, ROW_BLOCK), :]
        kn = _rope_pairs(_rms_head_pairs(k_ref[pl.ds(r0, ROW_BLOCK), :].astype(F32), kw), cos2, sin2)
        ks_scr[pl.ds(r0 + blk, ROW_BLOCK), :] = kn.astype(BF16)
        vs_scr[pl.ds(r0 + blk, ROW_BLOCK), :] = v_ref[pl.ds(r0, ROW_BLOCK), :]
        for pair in range(C_QHEADS // 2):
            qp = _rms_head_pairs(q_ref[pl.ds(r0, ROW_BLOCK), pair * LANES:(pair + 1) * LANES].astype(F32), qw)
            qp = (_rope_pairs(qp, cos2, sin2) * scale).astype(BF16)
            for half in range(2):
                hq = 2 * pair + half
                hk, g = hq // C_GROUP, hq % C_GROUP
                for sub in range(ROW_BLOCK // blk):
                    dst = pl.multiple_of((r * (ROW_BLOCK // blk) + sub) * grp_rows + g * blk, blk)
                    qs_scr[hk, pl.ds(dst, blk), :] = qp[sub * blk:(sub + 1) * blk, half * C_HD:(half + 1) * C_HD]
        return carry

    lax.fori_loop(0, nrb, prep, 0)

    nkeys = 3 * blk + past_len
    kcb = kc_ref[0, 0].astype(BF16)
    vcb = vc_ref[0, 0].astype(BF16)
    qi = lax.broadcasted_iota(jnp.int32, (grp_rows, nkeys), 0) & (blk - 1)
    kj = lax.broadcasted_iota(jnp.int32, (grp_rows, nkeys), 1)
    visible = jnp.logical_or(kj >= 3 * blk, jnp.logical_and(kj >= qi, kj <= qi + 2 * C_WINDOW))
    bias_scr[...] = jnp.where(visible, 0.0, NEG_BIG)
    head_of_row = lax.broadcasted_iota(jnp.int32, (grp_rows, 1), 0) // blk
    kcol = lax.broadcasted_iota(jnp.int32, (1, nkeys), 1)

    def qblock(n, carry):
        r0 = pl.multiple_of(n * blk, blk)
        kpos = kcol + (r0 - blk)
        in_seq = jnp.logical_or(kcol >= 3 * blk, jnp.logical_and(kpos >= 0, kpos < t))
        edge = jnp.where(in_seq, 0.0, NEG_BIG)
        kwin = ks_scr[pl.ds(r0, 3 * blk), :]
        vwin = vs_scr[pl.ds(r0, 3 * blk), :]
        scores, vals, sinks = [], [], []
        for hk in range(C_KVHEADS):
            lo, hi = hk * C_HD, (hk + 1) * C_HD
            keys = jnp.concatenate([kwin[:, lo:hi], kcb[:, lo:hi]], axis=0)
            vals.append(jnp.concatenate([vwin[:, lo:hi], vcb[:, lo:hi]], axis=0))
            q_stack = qs_scr[hk, pl.ds(pl.multiple_of(n * grp_rows, grp_rows), grp_rows), :]
            scores.append(_dot_nt(q_stack, keys))
            sink = jnp.full((grp_rows, 1), sink_ref[layer, hk * C_GROUP], F32)
            for g in range(1, C_GROUP):
                sink = jnp.where(head_of_row == g, sink_ref[layer, hk * C_GROUP + g], sink)
            sinks.append(sink * LOG2E)
        outs = []
        for hk in range(C_KVHEADS):
            o = _softmax_sink_av([scores[hk] + bias_scr[...] + edge], [vals[hk]], sinks[hk])
            outs += [o[g * blk:(g + 1) * blk] for g in range(C_GROUP)]
        for pair in range(C_QHEADS // 2):
            o_ref[pl.ds(r0, blk), pair * LANES:(pair + 1) * LANES] = jnp.concatenate(
                outs[2 * pair:2 * pair + 2], axis=1).astype(o_ref.dtype)
        return carry

    lax.fori_loop(0, t // blk, qblock, 0)


def _attn_lat_call(proj, cache_k, cache_v, prm, cos2, sin2, layer, nseq, t):
    past_len = cache_k.shape[2]
    qw = C_QHEADS * C_HD
    return pl.pallas_call(
        functools.partial(_attn_lat_kernel, t=t, past_len=past_len, layer=layer),
        out_shape=jax.ShapeDtypeStruct((nseq * t, qw), BF16),
        grid=(nseq,),
        in_specs=[pl.BlockSpec((t, qw), lambda b: (b, COL_QC // qw)),
                  pl.BlockSpec((t, LANES), lambda b: (b, COL_KC // LANES)),
                  pl.BlockSpec((t, LANES), lambda b: (b, COL_VC // LANES)),
                  pl.BlockSpec((1, 1, past_len, LANES), lambda b: (b, layer, 0, 0)),
                  pl.BlockSpec((1, 1, past_len, LANES), lambda b: (b, layer, 0, 0)),
                  _layer_block((1, LANES), layer), _layer_block((1, LANES), layer),
                  _resident((t, LANES), lambda b: (0, 0)), _resident((t, LANES), lambda b: (0, 0)),
                  pl.BlockSpec(memory_space=pltpu.SMEM)],
        out_specs=pl.BlockSpec((t, qw), lambda b: (b, 0)),
        scratch_shapes=[pltpu.VMEM((C_KVHEADS, t * C_GROUP, C_HD), BF16),
                        pltpu.VMEM((t + 2 * C_BLOCK, LANES), BF16), pltpu.VMEM((t + 2 * C_BLOCK, LANES), BF16),
                        pltpu.VMEM((C_GROUP * C_BLOCK, 3 * C_BLOCK + past_len), F32)],
        compiler_params=_cparams(("arbitrary",)),
        name="attn_lat",
    )(proj, proj, proj, cache_k, cache_v, prm["q_norm2"], prm["k_norm2"], cos2, sin2, prm["sink"])


def _rope_tables(t):
    rows = t // GRID_W
    row = jnp.repeat(jnp.arange(rows, dtype=F32), GRID_W)
    col = jnp.tile(jnp.arange(GRID_W, dtype=F32), rows)
    nf = C_HD // 4
    inv = ROPE_THETA ** (-jnp.arange(nf, dtype=F32) / nf)
    ar, ac = row[:, None] * inv, col[:, None] * inv
    cos = jnp.concatenate([jnp.cos(ar), jnp.cos(ar), jnp.cos(ac), jnp.cos(ac)], axis=1)
    sin = jnp.concatenate([-jnp.sin(ar), jnp.sin(ar), -jnp.sin(ac), jnp.sin(ac)], axis=1)
    return jnp.tile(cos, (1, 2)), jnp.tile(sin, (1, 2))


def _merge_kernel(x_ref, g0_ref, g1_ref, g2_ref, oa_ref, ob_ref, oc_ref, mod_ref, wbr_ref, wout_ref, o_ref):
    merged = (_sigmoid(g0_ref[...].astype(F32)) * jnp.dot(oa_ref[...], wbr_ref[0], preferred_element_type=F32)
              + _sigmoid(g1_ref[...].astype(F32)) * jnp.dot(ob_ref[...], wbr_ref[1], preferred_element_type=F32)
              + _sigmoid(g2_ref[...].astype(F32)) * jnp.dot(oc_ref[...], wbr_ref[2], preferred_element_type=F32))
    res = jnp.dot(merged.astype(BF16), wout_ref[...], preferred_element_type=F32)
    o_ref[...] = x_ref[...] + mod_ref[0, 2:3, :] * res


def _merge_call(x2d, proj, oa, ob, oc, prm, layer, cond, tm):
    m, d = x2d.shape
    mg = lambda r: pl.BlockSpec((tm, d), lambda i: (i, COL_MG // d + r))
    br = pl.BlockSpec((tm, BRANCH_W), lambda i: (i, 0))
    return pl.pallas_call(
        _merge_kernel,
        out_shape=jax.ShapeDtypeStruct((m, d), F32),
        grid=(m // tm,),
        in_specs=[pl.BlockSpec((tm, d), lambda i: (i, 0)), mg(0), mg(1), mg(2), br, br, br,
                  _mod_block(d, layer, cond), _layer_block((3, BRANCH_W, d), layer), _layer_block((d, d), layer)],
        out_specs=pl.BlockSpec((tm, d), lambda i: (i, 0)),
        compiler_params=_cparams(("arbitrary",)),
        name="merge",
    )(x2d, proj, proj, proj, oa, ob, oc, prm["mod"], prm["w_branch"], prm["w_out"])


FF_CHUNK = 256
HALO = BF16_ROWS


def _ffn_kernel(x_ref, xp_ref, xn_ref, mod_ref, nw_ref, wup_ref, cw_ref, wd_ref, o_ref, h_scr, act_scr, *,
                tm, seq_len):
    i = pl.program_id(0)
    nseg = max(1, tm // seq_len)
    seg = tm // nseg
    nw, sh, sc = nw_ref[...], mod_ref[0, 3:4, :], mod_ref[0, 4:5, :]

    def norm(x):
        return _rms_rows(x, nw) * (1.0 + sc) + sh

    has_prev = ((i * tm) & (seq_len - 1)) != 0
    has_next = (((i + 1) * tm) & (seq_len - 1)) != 0
    zero_halo = jnp.zeros((HALO, x_ref.shape[1]), BF16)
    for s in range(nseg):
        h_scr[s, HALO:HALO + seg, :] = norm(x_ref[s * seg:(s + 1) * seg, :]).astype(BF16)
        if s == 0:
            h_scr[s, 0:HALO, :] = (norm(xp_ref[...]) * jnp.where(has_prev, 1.0, 0.0)).astype(BF16)
        else:
            h_scr[s, 0:HALO, :] = zero_halo
        if s == nseg - 1:
            h_scr[s, HALO + seg:2 * HALO + seg, :] = (norm(xn_ref[...]) * jnp.where(has_next, 1.0, 0.0)).astype(BF16)
        else:
            h_scr[s, HALO + seg:2 * HALO + seg, :] = zero_halo

    def conv(u, cw):
        return (u[HALO - 1:HALO - 1 + seg] * cw[0:1, :] + u[HALO:HALO + seg] * cw[1:2, :]
                + u[HALO + 1:HALO + 1 + seg] * cw[2:3, :])

    for lo, hi in _col_chunks(D_FF, FF_CHUNK):
        for s in range(nseg):
            h = h_scr[s]
            a = conv(jnp.dot(h, wup_ref[:, lo:hi], preferred_element_type=F32), cw_ref[:, lo:hi])
            u = conv(jnp.dot(h, wup_ref[:, D_FF + lo:D_FF + hi], preferred_element_type=F32),
                     cw_ref[:, D_FF + lo:D_FF + hi])
            act_scr[s * seg:(s + 1) * seg, lo:hi] = (_silu(a) * u).astype(BF16)

    o_ref[...] = x_ref[...] + mod_ref[0, 5:6, :] * jnp.dot(act_scr[...], wd_ref[...], preferred_element_type=F32)


def _ffn_call(x2d, prm, layer, cond, tm, seq_len):
    m, d = x2d.shape
    hb = tm // HALO
    last = m // HALO - 1
    nseg = max(1, tm // seq_len)
    seg = tm // nseg
    return pl.pallas_call(
        functools.partial(_ffn_kernel, tm=tm, seq_len=seq_len),
        out_shape=jax.ShapeDtypeStruct((m, d), F32),
        grid=(m // tm,),
        in_specs=[pl.BlockSpec((tm, d), lambda i: (i, 0)),
                  pl.BlockSpec((HALO, d), lambda i: (jnp.maximum(i * hb - 1, 0), 0)),
                  pl.BlockSpec((HALO, d), lambda i: (jnp.minimum((i + 1) * hb, last), 0)),
                  _mod_block(d, layer, cond), _layer_block((1, d), layer), _layer_block((d, 2 * D_FF), layer),
                  _layer_block((3, 2 * D_FF), layer), _layer_block((D_FF, d), layer)],
        out_specs=pl.BlockSpec((tm, d), lambda i: (i, 0)),
        scratch_shapes=[pltpu.VMEM((nseg, seg + 2 * HALO, d), BF16), pltpu.VMEM((tm, D_FF), BF16)],
        compiler_params=_cparams(("arbitrary",)),
        name="ffn",
    )(x2d, x2d, x2d, prm["mod"], prm["norm2_w"], prm["w_up"], prm["conv_ffn"], prm["w_down"])


def _permute_w_in(w):
    s = _SRC
    w16 = jnp.concatenate([w[..., s["mg"]:s["end"]], w[..., s["qa"]:s["beta"]], w[..., s["qb"]:s["fb"]],
                           w[..., s["gb"]:s["mg"]]], axis=-1)
    n_gate = s["qb"] - s["beta"]
    gates = jnp.pad(w[..., s["beta"]:s["qb"]], ((0, 0), (0, 0), (0, LANES - n_gate)))
    w32 = jnp.concatenate([w[..., s["fb"]:s["gb"]], gates], axis=-1)
    return w16.astype(BF16), w32.astype(BF16)


def _gate_rows(proj32, nseq, t, c):
    g = proj32[:, COL32_GATES:COL32_GATES + 4 * HEADS].reshape(nseq, t // c, c, 2, 2, HEADS)
    return jnp.transpose(g, (0, 5, 1, 3, 4, 2)).reshape(nseq, HEADS, t // c, 4, c)


def _row_tile(rows, t):
    tm = 512
    while rows % tm or (t % tm and tm % t):
        tm //= 2
    return tm


def _group_forward(x3d, first_cond, shared_cond, prm, past, tables):
    nseq, t, d = x3d.shape
    x = x3d.reshape(nseq * t, d)
    tm = _row_tile(nseq * t, t)
    tiles_per_seq = max(1, t // tm)
    cond = (lambda i: first_cond) if shared_cond else (lambda i: first_cond + i // tiles_per_seq)
    emit = past is None
    states_a, states_b, keys, vals = [], [], [], []
    for l in range(prm["w16"].shape[0]):
        proj, proj32 = _in_proj_call(x, prm, l, cond, tm)
        gates_r = _gate_rows(proj32, nseq, t, min(CHUNK_A, t))
        res_ab = _mixer_ab_call(proj, proj32, gates_r, prm, past, l, nseq, t, emit)
        if emit:
            oc, kn, vn = _attn_ctx_call(proj, prm, l, nseq, t)
            states_a.append(res_ab[2])
            states_b.append(res_ab[3])
            keys.append(kn.reshape(nseq, t, C_KVHEADS, C_HD))
            vals.append(vn.reshape(nseq, t, C_KVHEADS, C_HD))
        else:
            oc = _attn_lat_call(proj, past[2], past[3], prm, tables[0], tables[1], l, nseq, t)
        x = _merge_call(x, proj, res_ab[0], res_ab[1], oc, prm, l, cond, tm)
        x = _ffn_call(x, prm, l, cond, tm, t)
    return x.reshape(nseq, t, d), states_a, states_b, keys, vals


def kernel(x_prompt, x_sample, state_delta, state_hgrn, cache_k, cache_v, c, c_ctx, ada_w, ada_b, norm1_w, w_in, conv_a, a_log, dt_bias, norm_a, lb_logits, norm_b, q_norm, k_norm, sink, w_branch, w_out, norm2_w, w_up, conv_ffn, w_down):
    depth = w_in.shape[0]
    d = x_prompt.shape[-1]

    cond = jnp.concatenate([c_ctx[None, :], c], axis=0)
    rows = -(-cond.shape[0] // SUBLANES) * SUBLANES
    cond = jnp.pad(cond, ((0, rows - cond.shape[0]), (0, 0)))
    mod_all = _mod_call(cond, ada_w, ada_b).reshape(depth, rows, 6, d)

    w16, w32 = _permute_w_in(w_in)
    prm = dict(
        mod=mod_all, w16=w16, w32=w32,
        norm1_w=norm1_w.reshape(depth, 1, d), norm2_w=norm2_w.reshape(depth, 1, d),
        conv_a=conv_a, a_log=a_log, dt_bias=dt_bias, norm_a=norm_a.reshape(depth, 1, HEAD_W),
        lb_logits=lb_logits, norm_b=norm_b.reshape(depth, 1, HEAD_W),
        q_norm2=jnp.tile(q_norm, (1, 2)).reshape(depth, 1, LANES), k_norm2=jnp.tile(k_norm, (1, 2)).reshape(depth, 1, LANES),
        sink=sink, w_branch=w_branch.astype(BF16), w_out=w_out.astype(BF16),
        w_up=w_up.astype(BF16), conv_ffn=conv_ffn, w_down=w_down.astype(BF16))

    y_prompt, st_a, st_b, keys, vals = _group_forward(x_prompt, 0, True, prm, None, None)

    past_len = cache_k.shape[2]
    past = (state_delta, state_hgrn,
            cache_k.reshape(cache_k.shape[0], depth, past_len, C_KVHEADS * C_HD),
            cache_v.reshape(cache_v.shape[0], depth, past_len, C_KVHEADS * C_HD))
    y_sample, _, _, _, _ = _group_forward(x_sample, 1, False, prm, past, _rope_tables(x_sample.shape[1]))

    return (y_prompt, y_sample, jnp.stack(st_a, axis=1), jnp.stack(st_b, axis=1),
            jnp.stack(keys, axis=1), jnp.stack(vals, axis=1))
```

```python
import functools
import math

import numpy as np
import jax
import jax.numpy as jnp
from jax import lax
from jax.experimental import pallas as pl
from jax.experimental.pallas import tpu as pltpu

F32 = jnp.float32
BF16 = jnp.bfloat16

D_MODEL = 1024
NORM_EPS = 1e-6
LB_EPS = 1e-6
NEG_BIG = -1e30
LOG2E = 1.4426950408889634
GRID_W = 64
ROPE_THETA = 10000.0

HEADS = 4
HEAD_W = 128
SHORT_CONV = 5
C_QHEADS = 8
C_KVHEADS = 2
C_GROUP = C_QHEADS // C_KVHEADS
C_HD = 64
C_WINDOW = 128
C_BLOCK = 128
BRANCH_W = 512
D_FF = 2816

LANES = 128
SUBLANES = 8
BF16_ROWS = 16
VMEM_LIMIT = 56 * 1024 * 1024

COL_MG = 0
COL_QA = 3072
COL_KA = 3584
COL_VA = 4096
COL_GA = 4608
COL_QB = 5120
COL_IB = 5632
COL_GB = 6144
COL_QC = 6656
COL_KC = 7168
COL_VC = 7296
PROJ_W = 7424
COL32_FB = 0
COL32_GATES = 1024
PROJ32_W = 1152
_SRC = dict(qa=0, ka=512, va=1024, ga=1536, beta=2048, alpha=2056, qb=2064, ib=2576, fb=3088, gb=4112,
            qc=4624, kc=5136, vc=5264, mg=5392, end=8464)

CHUNK_A = 128
ROUNDS_A = 8
CHUNK_B = 64
TRI_BASE = 16
TRI_MERGE = 2
MM_TILE = 768


def _cparams(sem):
    return pltpu.CompilerParams(dimension_semantics=sem, vmem_limit_bytes=VMEM_LIMIT)


def _resident(shape, index_map):
    return pl.BlockSpec(shape, index_map, pipeline_mode=pl.Buffered(1))


def _layer_block(shape, layer):
    return _resident((None,) + tuple(shape), lambda *_: (layer,) + (0,) * len(shape))


def _mod_block(d, layer, cond):
    return pl.BlockSpec((None, 1, 6, d), lambda i: (layer, cond(i), 0, 0))


def _dot(a, b):
    return jnp.dot(a.astype(BF16), b.astype(BF16), preferred_element_type=F32)


def _dot_nt(a, b):
    return lax.dot_general(a.astype(BF16), b.astype(BF16), (((1,), (1,)), ((), ())), preferred_element_type=F32)


def _dot_tn(a, b):
    return lax.dot_general(a.astype(BF16), b.astype(BF16), (((0,), (0,)), ((), ())), preferred_element_type=F32)


def _sigmoid(x):
    return 1.0 / (1.0 + jnp.exp(-x))


def _silu(x):
    return x * _sigmoid(x)


def _softplus(x):
    return jnp.maximum(x, 0.0) + jnp.log(1.0 + jnp.exp(-jnp.abs(x)))


def _rms_rows(x, w):
    ms = jnp.mean(x * x, axis=-1, keepdims=True)
    return x * lax.rsqrt(ms + NORM_EPS) * w


def _split_bf16(x, n):
    parts, r = [], x
    for _ in range(n):
        p = r.astype(BF16)
        parts.append(p)
        r = r - p.astype(F32)
    return parts


def _col_chunks(width, step):
    return [(lo, min(lo + step, width)) for lo in range(0, width, step)]


def _mod_kernel(c_ref, w_ref, b_ref, o_ref):
    c = c_ref[...]
    o_ref[0] = _dot(_silu(c), w_ref[0]) + b_ref[0]


def _mod_call(cond, ada_w, ada_b):
    depth, d, n = ada_w.shape
    rows = cond.shape[0]
    tn = 768
    return pl.pallas_call(
        _mod_kernel,
        out_shape=jax.ShapeDtypeStruct((depth, rows, n), F32),
        grid=(depth, n // tn),
        in_specs=[pl.BlockSpec((rows, d), lambda l, j: (0, 0)),
                  pl.BlockSpec((1, d, tn), lambda l, j: (l, 0, j)),
                  pl.BlockSpec((1, 1, tn), lambda l, j: (l, 0, j))],
        out_specs=pl.BlockSpec((1, rows, tn), lambda l, j: (l, 0, j)),
        compiler_params=_cparams(("arbitrary", "arbitrary")),
        name="mod",
    )(cond, ada_w, ada_b.reshape(depth, 1, n))


def _in_proj_kernel(x_ref, mod_ref, nw_ref, w16_ref, w32_ref, o16_ref, o32_ref):
    h = _rms_rows(x_ref[...], nw_ref[...]) * (1.0 + mod_ref[0, 1:2, :]) + mod_ref[0, 0:1, :]
    hb = h.astype(BF16)
    for lo, hi in _col_chunks(PROJ_W, MM_TILE):
        o16_ref[:, lo:hi] = jnp.dot(hb, w16_ref[:, lo:hi], preferred_element_type=F32).astype(BF16)
    for lo, hi in _col_chunks(PROJ32_W, MM_TILE):
        o32_ref[:, lo:hi] = jnp.dot(hb, w32_ref[:, lo:hi], preferred_element_type=F32)


def _in_proj_call(x2d, prm, layer, cond, tm):
    m, d = x2d.shape
    return pl.pallas_call(
        _in_proj_kernel,
        out_shape=(jax.ShapeDtypeStruct((m, PROJ_W), BF16), jax.ShapeDtypeStruct((m, PROJ32_W), F32)),
        grid=(m // tm,),
        in_specs=[pl.BlockSpec((tm, d), lambda i: (i, 0)), _mod_block(d, layer, cond),
                  _layer_block((1, d), layer), _layer_block((d, PROJ_W), layer), _layer_block((d, PROJ32_W), layer)],
        out_specs=(pl.BlockSpec((tm, PROJ_W), lambda i: (i, 0)),
                   pl.BlockSpec((tm, PROJ32_W), lambda i: (i, 0))),
        compiler_params=_cparams(("arbitrary",)),
        name="in_proj",
    )(x2d, prm["mod"], prm["norm1_w"], prm["w16"], prm["w32"])


ROW_BLOCK = 256


def _gated_norm_epilogue(of_scr, ob_scr, gate_ref, nw_ref, o_ref, t, cols):
    nw = nw_ref[...]

    def body(r, carry):
        r0 = pl.multiple_of(r * ROW_BLOCK, ROW_BLOCK)
        o = of_scr[pl.ds(r0, ROW_BLOCK), :] + ob_scr[pl.ds(r0, ROW_BLOCK), :]
        y = _rms_rows(o, nw) * _silu(gate_ref[pl.ds(r0, ROW_BLOCK), cols].astype(F32))
        o_ref[pl.ds(r0, ROW_BLOCK), cols] = y.astype(o_ref.dtype)
        return carry

    lax.fori_loop(0, t // ROW_BLOCK, body, 0)


def _tri_inverse(mats, ri, ci, c, tick):
    n = range(len(mats))
    shift = int(np.log2(TRI_BASE))
    base = (ri >> shift) == (ci >> shift)
    mb = [jnp.where(base, m, 0.0) for m in mats]
    y = [-m for m in mb]
    p = [_dot(m, m) for m in mb]
    tick()
    for _ in range(shift - 2):
        yp = [_dot(jnp.concatenate([y[i], p[i]], axis=0), p[i]) for i in n]
        tick()
        y = [y[i] + p[i] + yp[i][:c] for i in n]
        p = [r[c:] for r in yp]
    yp = [_dot(y[i], p[i]) for i in n]
    tick()
    y = [y[i] + p[i] + yp[i] for i in n]
    total = int(np.log2(c))
    while shift < total:
        factors = min(TRI_MERGE, total - shift)
        inner = (ri >> shift) == (ci >> shift)
        outer = (ri >> (shift + factors)) == (ci >> (shift + factors))
        between = jnp.logical_and(outer, jnp.logical_not(inner))
        cm = [jnp.where(between, m, 0.0) for m in mats]
        w = [_dot(y[i], cm[i]) for i in n]
        tick()
        w = [cm[i] + w[i] for i in n]
        if factors == 1:
            wy = [_dot(w[i], y[i]) for i in n]
            tick()
            y = [y[i] - w[i] - wy[i] for i in n]
        else:
            r = [_dot(w[i], jnp.concatenate([w[i], y[i]], axis=1)) for i in n]
            tick()
            p, y = [x[:, :c] for x in r], [y[i] - w[i] - r[i][:, c:] for i in n]
            for f in range(1, factors):
                if f + 1 < factors:
                    r = [_dot(p[i], jnp.concatenate([p[i], y[i]], axis=1)) for i in n]
                    tick()
                    p, y = [x[:, :c] for x in r], [y[i] + p[i] + r[i][:, c:] for i in n]
                else:
                    py = [_dot(p[i], y[i]) for i in n]
                    tick()
                    y = [y[i] + p[i] + py[i] for i in n]
        shift += factors
    eye = jnp.where(ri == ci, 1.0, 0.0)
    return [eye + v for v in y]


def _delta_chunks(probs, states, c, tick):
    n = range(len(probs))
    ri = lax.broadcasted_iota(jnp.int32, (c, c), 0)
    ci = lax.broadcasted_iota(jnp.int32, (c, c), 1)
    eye = ri == ci
    pre = []
    for q, k, v, beta_row, alpha_row, a_neg, dt_b, chain, reverse in probs:
        incl, strict = (ri <= ci, ri < ci) if reverse else (ri >= ci, ri > ci)
        beta_r = _sigmoid(beta_row)
        g_r = a_neg * _softplus(alpha_row + dt_b)
        gc_col = jnp.sum(jnp.where(incl, jnp.broadcast_to(g_r, (c, c)), 0.0), axis=1, keepdims=True)
        beta_col = jnp.sum(jnp.where(eye, jnp.broadcast_to(beta_r, (c, c)), 0.0), axis=1, keepdims=True)
        gc_row = jnp.sum(jnp.where(eye, jnp.broadcast_to(gc_col, (c, c)), 0.0), axis=0, keepdims=True)
        g_tot = jnp.sum(g_r, axis=1, keepdims=True)
        decay = jnp.where(incl, jnp.exp(jnp.where(incl, gc_col - gc_row, 0.0)), 0.0)
        pre.append((strict, gc_col, beta_col, g_tot, decay, jnp.exp(gc_col)))

    kb = [pr[1].astype(BF16) for pr in probs]
    kq = [_dot_nt(jnp.concatenate([kb[i], probs[i][0].astype(BF16)], axis=0), kb[i]) for i in n]
    tick()
    t_inv = _tri_inverse([jnp.where(pre[i][0], kq[i][:c] * pre[i][2] * pre[i][4], 0.0) for i in n], ri, ci, c, tick)
    uw = [_dot(t_inv[i], jnp.concatenate([probs[i][2] * pre[i][2], probs[i][1] * (pre[i][2] * pre[i][5])], axis=1))
          for i in n]
    tick()
    wq_lhs = [jnp.concatenate([uw[i][:, HEAD_W:], probs[i][0] * pre[i][5]], axis=0).astype(BF16) for i in n]
    os_lhs = [jnp.concatenate([kq[i][c:] * pre[i][4], (probs[i][1] * jnp.exp(pre[i][3] - pre[i][1])).T],
                              axis=0).astype(BF16) for i in n]
    states = list(states)
    outs = [None] * len(probs)
    todo = list(n)
    while todo:
        front, seen = [], set()
        for i in todo:
            if probs[i][7] not in seen:
                seen.add(probs[i][7])
                front.append(i)
        todo = [i for i in todo if i not in front]
        wq = [_dot(wq_lhs[i], states[probs[i][7]]) for i in front]
        tick()
        v_new = [uw[i][:, :HEAD_W] - wq[j][:c] for j, i in enumerate(front)]
        os_ = [_dot(os_lhs[i], v_new[j]) for j, i in enumerate(front)]
        tick()
        for j, i in enumerate(front):
            outs[i] = wq[j][c:] + os_[j][:c]
            states[probs[i][7]] = states[probs[i][7]] * jnp.exp(pre[i][3]) + os_[j][c:]
    return outs, states


def _conv_silu_pass(x_ref, cw_ref, xp_scr, dst_scr, t, l2, scale, cols):
    nrb = t // ROW_BLOCK
    pad = SUBLANES
    half = SHORT_CONV // 2

    def cp(r, carry):
        r0 = pl.multiple_of(r * ROW_BLOCK, ROW_BLOCK)
        xp_scr[pl.ds(r0 + pad, ROW_BLOCK), :] = x_ref[pl.ds(r0, ROW_BLOCK), cols].astype(F32)
        return carry

    lax.fori_loop(0, nrb, cp, 0)
    cw = cw_ref[:, cols]

    def body(r, carry):
        r0 = pl.multiple_of(r * ROW_BLOCK, ROW_BLOCK)
        y = None
        for j in range(SHORT_CONV):
            term = xp_scr[pl.ds(r0 + (pad - half + j), ROW_BLOCK), :] * cw[j:j + 1, :]
            y = term if y is None else y + term
        y = _silu(y)
        if l2:
            y = y * (lax.rsqrt(jnp.sum(y * y, axis=-1, keepdims=True) + NORM_EPS) * scale)
        dst_scr[pl.ds(r0, ROW_BLOCK), :] = y
        return carry

    lax.fori_loop(0, nrb, body, 0)


def _hgrn_diag(q, kf, v, b, reverse, c):
    nb = c // SUBLANES
    q3, k3, v3, b3 = (a.reshape(nb, SUBLANES, HEAD_W) for a in (q, kf, v, b))
    sub = lax.broadcasted_iota(jnp.int32, (nb, SUBLANES, HEAD_W), 1)
    o3 = jnp.zeros((nb, SUBLANES, HEAD_W), F32)
    for j in range(SUBLANES):
        mask = (sub <= j) if reverse else (sub >= j)
        e = jnp.exp2(jnp.where(mask, b3 - b3[:, j:j + 1, :], NEG_BIG))
        a = jnp.sum(q3 * e * k3[:, j:j + 1, :], axis=-1, keepdims=True)
        o3 = o3 + a * v3[:, j:j + 1, :]
    return o3.reshape(c, HEAD_W)


def _hgrn_chunk(q, kf, v, lf, st, reverse, c):
    ri = lax.broadcasted_iota(jnp.int32, (c, c), 0)
    ci = lax.broadcasted_iota(jnp.int32, (c, c), 1)
    incl = (ri <= ci) if reverse else (ri >= ci)
    parts = jnp.concatenate(_split_bf16(lf, 3), axis=1)
    b3 = jnp.dot(jnp.where(incl, 1.0, 0.0).astype(BF16), parts, preferred_element_type=F32)
    b = b3[:, :HEAD_W] + b3[:, HEAD_W:2 * HEAD_W] + b3[:, 2 * HEAD_W:]
    b_tot = jnp.sum(lf, axis=0, keepdims=True)
    o = _dot_nt(q * jnp.exp2(b), st)
    row = lax.broadcasted_iota(jnp.int32, (c, 1), 0)
    att = jnp.zeros((c, c), F32)
    n = SUBLANES
    while n < c:
        pieces = []
        for g in range(c // (2 * n)):
            r = g * 2 * n + (n if reverse else n - 1)
            pieces.append(jnp.broadcast_to(b[r:r + 1, :], (2 * n, HEAD_W)))
        ref = pieces[0] if len(pieces) == 1 else jnp.concatenate(pieces, axis=0)
        e = jnp.exp2(-jnp.abs(b - ref))
        s = int(np.log2(n))
        second = ((row >> s) & 1) == 1
        q_part = jnp.logical_not(second) if reverse else second
        qt = jnp.where(q_part, q * e, 0.0)
        kt = jnp.where(q_part, 0.0, kf * e)
        att = att + jnp.where((ri >> (s + 1)) == (ci >> (s + 1)), _dot_nt(qt, kt), 0.0)
        n *= 2
    o = o + _dot(att, v) + _hgrn_diag(q, kf, v, b, reverse, c)
    kh = kf * jnp.exp2(b_tot - b)
    st_new = st * jnp.exp2(b_tot) + _dot_tn(v, kh)
    return o, st_new


def _mixer_ab_kernel(*refs, t, ca, cb, layer, hps, has_past, emit_state):
    refs = list(refs)
    (qa_ref, ka_ref, va_ref, ga_ref, gr_ref, cwq_ref, cwk_ref, cwv_ref, alog_ref, dtb_ref, nwa_ref,
     qb_ref, ib_ref, f0_ref, f1_ref, gb_ref, lb_ref, nwb_ref) = refs[:18]
    pos = 18
    sa0_ref = sb0_ref = None
    if has_past:
        sa0_ref, sb0_ref = refs[pos:pos + 2]
        pos += 2
    oa_ref, ob_ref = refs[pos:pos + 2]
    pos += 2
    sfa_ref = sfb_ref = None
    if emit_state:
        sfa_ref, sfb_ref = refs[pos:pos + 2]
        pos += 2
    xp_scr, qn_scr, kn_scr, vn_scr, af_scr, ab_scr, bf_scr, bb_scr, sa_scr, sb_scr = refs[pos:]

    heads = range(hps)
    cols = [slice(hh * HEAD_W, (hh + 1) * HEAD_W) for hh in heads]
    first_head = pl.program_id(1) * hps
    pad = SUBLANES
    xp_scr[0:pad, :] = jnp.zeros((pad, HEAD_W), F32)
    xp_scr[t + pad:t + 2 * pad, :] = jnp.zeros((pad, HEAD_W), F32)
    for hh in heads:
        _conv_silu_pass(qa_ref, cwq_ref, xp_scr, qn_scr.at[hh], t, True, HEAD_W ** -0.5, cols[hh])
        _conv_silu_pass(ka_ref, cwk_ref, xp_scr, kn_scr.at[hh], t, True, 1.0, cols[hh])
        _conv_silu_pass(va_ref, cwv_ref, xp_scr, vn_scr.at[hh], t, False, 1.0, cols[hh])

    if has_past:
        for hh in heads:
            for d in range(2):
                sa_scr[2 * hh + d] = sa0_ref[0, 0, d, hh]
                sb_scr[2 * hh + d] = sb0_ref[0, 0, d, hh].T
    else:
        sa_scr[...] = jnp.zeros(sa_scr.shape, F32)
        sb_scr[...] = jnp.zeros(sb_scr.shape, F32)

    ones = jnp.ones((1, ca), F32)
    a_neg = [[-jnp.exp(ones * alog_ref[layer, d, first_head + hh]) for d in range(2)] for hh in heads]
    dt_b = [[dtb_ref[layer, d, first_head + hh] for d in range(2)] for hh in heads]

    lb_terms = None
    if layer > 0:
        lb_terms = []
        for hh in heads:
            terms = []
            for d in range(2):
                lg = lb_ref[d][:, cols[hh]]
                ex = jnp.exp(lg - jnp.max(lg, axis=0, keepdims=True))
                pr = ex / jnp.sum(ex, axis=0, keepdims=True)
                lb = jnp.clip(jnp.sum(pr[1:layer + 1], axis=0, keepdims=True), LB_EPS, 1.0 - LB_EPS)
                terms.append((jnp.log(lb) * LOG2E, jnp.log1p(-lb) * LOG2E, 1.0 - lb))
            lb_terms.append(terms)

    f_refs = (f0_ref, f1_ref)
    nca, ncb = t // ca, t // cb
    rounds = math.gcd(ROUNDS_A // hps, nca)
    ratio = rounds * ca // cb

    def hgrn_gates(z, hh, d):
        z2 = z * LOG2E
        y = jnp.exp2(-jnp.abs(z2))
        one_y = 1.0 + y
        l2_sig = jnp.minimum(z2, 0.0) - jnp.log(one_y) * LOG2E
        sig_neg = jnp.where(z >= 0.0, y, 1.0) / one_y
        if layer == 0:
            return l2_sig, sig_neg
        l2_lb, l2_1m_lb, one_m_lb = lb_terms[hh][d]
        a2 = l2_1m_lb + l2_sig
        return (jnp.maximum(l2_lb, a2) + jnp.log(1.0 + jnp.exp2(-jnp.abs(l2_lb - a2))) * LOG2E,
                one_m_lb * sig_neg)

    def step(n, carry):
        a_dst, a_probs = [], []
        for rnd in range(rounds):
            for hh in heads:
                for d in range(2):
                    cidx = n * rounds + rnd if d == 0 else nca - 1 - (n * rounds + rnd)
                    r0 = pl.multiple_of(cidx * ca, ca)
                    gr = gr_ref[0, hh, cidx]
                    a_dst.append((af_scr if d == 0 else ab_scr, hh, r0))
                    a_probs.append((qn_scr[hh, pl.ds(r0, ca), :], kn_scr[hh, pl.ds(r0, ca), :],
                                    vn_scr[hh, pl.ds(r0, ca), :], gr[d:d + 1, :], gr[2 + d:3 + d, :],
                                    a_neg[hh][d], dt_b[hh][d], 2 * hh + d, d == 1))
        sa = [sa_scr[chain] for chain in range(2 * hps)]
        b_jobs = []
        for j in range(ratio):
            m = n * ratio + j
            for hh in heads:
                for d in range(2):
                    r0 = pl.multiple_of((m if d == 0 else ncb - 1 - m) * cb, cb)
                    b_jobs.append((hh, d, r0, f_refs[d][pl.ds(r0, cb), cols[hh]], qb_ref[pl.ds(r0, cb), cols[hh]],
                                   ib_ref[pl.ds(r0, cb), cols[hh]]))
        sb = [sb_scr[chain] for chain in range(2 * hps)]

        b_out = []
        pending = list(b_jobs)

        def tick():
            if pending:
                hh, d, _, z, qraw, iraw = pending.pop(0)
                lf, kf = hgrn_gates(z, hh, d)
                o, sb[2 * hh + d] = _hgrn_chunk(_silu(qraw.astype(F32)), kf, iraw.astype(F32), lf, sb[2 * hh + d],
                                                d == 1, cb)
                b_out.append(o)

        a_out, sa = _delta_chunks(a_probs, sa, ca, tick)
        while pending:
            tick()

        for (dst, hh, r0), o in zip(a_dst, a_out):
            dst[hh, pl.ds(r0, ca), :] = o
        for (hh, d, r0, _, _, _), o in zip(b_jobs, b_out):
            (bf_scr if d == 0 else bb_scr)[hh, pl.ds(r0, cb), :] = o
        for chain in range(2 * hps):
            sa_scr[chain] = sa[chain]
            sb_scr[chain] = sb[chain]
        return carry

    lax.fori_loop(0, nca // rounds, step, 0)

    for hh in heads:
        _gated_norm_epilogue(af_scr.at[hh], ab_scr.at[hh], ga_ref, nwa_ref, oa_ref, t, cols[hh])
        _gated_norm_epilogue(bf_scr.at[hh], bb_scr.at[hh], gb_ref, nwb_ref, ob_ref, t, cols[hh])
    if emit_state:
        for hh in heads:
            for d in range(2):
                sfa_ref[0, d, hh] = sa_scr[2 * hh + d]
                sfb_ref[0, d, hh] = sb_scr[2 * hh + d].T


def _mixer_ab_call(proj, proj32, gates_r, prm, past, layer, nseq, t, emit_state):
    ca, cb = min(CHUNK_A, t), min(CHUNK_B, t)
    depth = prm["lb_logits"].shape[1]
    per_head = t * HEAD_W * (7 * 4 + 2 * (7 * 2 + 2 * 4) + 2 * 2 * 2)
    hps = 2 if 2 * per_head <= VMEM_LIMIT // 2 else 1
    slab = hps * HEAD_W
    col = lambda off: (lambda b, h: (b, off // slab + h))
    seq_in = lambda off: pl.BlockSpec((t, slab), col(off))
    conv = lambda part: pl.BlockSpec((None, SHORT_CONV, slab), lambda b, h: (layer, 0, part * (HEADS // hps) + h))
    smem = pl.BlockSpec(memory_space=pltpu.SMEM)
    norm = _layer_block((1, HEAD_W), layer)
    in_specs = [seq_in(COL_QA), seq_in(COL_KA), seq_in(COL_VA), seq_in(COL_GA),
                pl.BlockSpec((1, hps, t // ca, 4, ca), lambda b, h: (b, h, 0, 0, 0)),
                conv(0), conv(1), conv(2), smem, smem, norm,
                seq_in(COL_QB), seq_in(COL_IB), seq_in(COL32_FB), seq_in(COL32_FB + HEADS * HEAD_W), seq_in(COL_GB),
                pl.BlockSpec((2, depth, slab), lambda b, h: (0, 0, h)), norm]
    args = [proj, proj, proj, proj, gates_r, prm["conv_a"], prm["conv_a"], prm["conv_a"],
            prm["a_log"], prm["dt_bias"], prm["norm_a"],
            proj, proj, proj32, proj32, proj, prm["lb_logits"], prm["norm_b"]]
    state_in = pl.BlockSpec((1, 1, 2, hps, HEAD_W, HEAD_W), lambda b, h: (b, layer, 0, h, 0, 0))
    if past is not None:
        in_specs += [state_in, state_in]
        args += [past[0], past[1]]
    o_shape = jax.ShapeDtypeStruct((nseq * t, HEADS * HEAD_W), BF16)
    o_spec = pl.BlockSpec((t, slab), lambda b, h: (b, h))
    out_shape, out_specs = [o_shape, o_shape], [o_spec, o_spec]
    if emit_state:
        s_shape = jax.ShapeDtypeStruct((nseq, 2, HEADS, HEAD_W, HEAD_W), F32)
        s_spec = pl.BlockSpec((1, 2, hps, HEAD_W, HEAD_W), lambda b, h: (b, 0, h, 0, 0))
        out_shape += [s_shape, s_shape]
        out_specs += [s_spec, s_spec]
    seq = pltpu.VMEM((hps, t, HEAD_W), F32)
    state = pltpu.VMEM((2 * hps, HEAD_W, HEAD_W), F32)
    return pl.pallas_call(
        functools.partial(_mixer_ab_kernel, t=t, ca=ca, cb=cb, layer=layer, hps=hps, has_past=past is not None,
                          emit_state=emit_state),
        out_shape=tuple(out_shape), grid=(nseq, HEADS // hps), in_specs=in_specs, out_specs=tuple(out_specs),
        scratch_shapes=[pltpu.VMEM((t + 2 * SUBLANES, HEAD_W), F32), seq, seq, seq, seq, seq, seq, seq, state, state],
        compiler_params=_cparams(("arbitrary", "arbitrary")),
        name="mixer_ab",
    )(*args)


def _rms_head_pairs(x, w2):
    lane = lax.broadcasted_iota(jnp.int32, x.shape, 1)
    left = lane < C_HD
    sq = x * x
    s0 = jnp.sum(jnp.where(left, sq, 0.0), axis=-1, keepdims=True)
    s1 = jnp.sum(jnp.where(left, 0.0, sq), axis=-1, keepdims=True)
    ms = jnp.where(left, s0, s1) * (1.0 / C_HD)
    return x * lax.rsqrt(ms + NORM_EPS) * w2


def _rope_pairs(x, cos2, sin2):
    lane = lax.broadcasted_iota(jnp.int32, x.shape, 1)
    quarter = C_HD // 4
    swapped = jnp.where((lane & (2 * quarter - 1)) < quarter,
                        pltpu.roll(x, LANES - quarter, axis=1), pltpu.roll(x, quarter, axis=1))
    return x * cos2 + swapped * sin2


def _softmax_sink_av(scores, values, sink):
    m = sink
    for s in scores:
        m = jnp.maximum(m, jnp.max(s, axis=-1, keepdims=True))
    den = jnp.exp2(sink - m)
    acc = None
    for s, v in zip(scores, values):
        p = jnp.exp2(s - m)
        den = den + jnp.sum(p, axis=-1, keepdims=True)
        t = jnp.dot(p.astype(BF16), v, preferred_element_type=F32)
        acc = t if acc is None else acc + t
    return acc / den


def _attn_ctx_kernel(q_ref, k_ref, v_ref, qn_ref, kn_ref, sink_ref, o_ref, ko_ref, vo_ref, *, t, layer):
    qw, kw = qn_ref[...], kn_ref[...]
    kn = _rms_head_pairs(k_ref[...].astype(F32), kw)
    ko_ref[0] = kn
    v = v_ref[...]
    vo_ref[0] = v.astype(F32)
    knb, vb = kn.astype(BF16), v
    scale = C_HD ** -0.5 * LOG2E
    for pair in range(C_QHEADS // 2):
        qp = (_rms_head_pairs(q_ref[:, pair * LANES:(pair + 1) * LANES].astype(F32), qw) * scale).astype(BF16)
        outs = []
        for half in range(2):
            hq = 2 * pair + half
            hk = hq // C_GROUP
            qh = qp[:, half * C_HD:(half + 1) * C_HD]
            s = _dot_nt(qh, knb[:, hk * C_HD:(hk + 1) * C_HD])
            sink = jnp.full((1, 1), sink_ref[layer, hq] * LOG2E, F32)
            outs.append(_softmax_sink_av([s], [vb[:, hk * C_HD:(hk + 1) * C_HD]], sink))
        o_ref[:, pair * LANES:(pair + 1) * LANES] = jnp.concatenate(outs, axis=1).astype(o_ref.dtype)


def _attn_ctx_call(proj, prm, layer, nseq, t):
    return pl.pallas_call(
        functools.partial(_attn_ctx_kernel, t=t, layer=layer),
        out_shape=(jax.ShapeDtypeStruct((nseq * t, C_QHEADS * C_HD), BF16),
                   jax.ShapeDtypeStruct((nseq, t, LANES), F32), jax.ShapeDtypeStruct((nseq, t, LANES), F32)),
        grid=(nseq,),
        in_specs=[pl.BlockSpec((t, C_QHEADS * C_HD), lambda b: (b, COL_QC // (C_QHEADS * C_HD))),
                  pl.BlockSpec((t, LANES), lambda b: (b, COL_KC // LANES)),
                  pl.BlockSpec((t, LANES), lambda b: (b, COL_VC // LANES)),
                  _layer_block((1, LANES), layer), _layer_block((1, LANES), layer),
                  pl.BlockSpec(memory_space=pltpu.SMEM)],
        out_specs=(pl.BlockSpec((t, C_QHEADS * C_HD), lambda b: (b, 0)),
                   pl.BlockSpec((1, t, LANES), lambda b: (b, 0, 0)), pl.BlockSpec((1, t, LANES), lambda b: (b, 0, 0))),
        compiler_params=_cparams(("arbitrary",)),
        name="attn_ctx",
    )(proj, proj, proj, prm["q_norm2"], prm["k_norm2"], prm["sink"])


def _attn_lat_kernel(q_ref, k_ref, v_ref, kc_ref, vc_ref, qn_ref, kn_ref, cos_ref, sin_ref, sink_ref, o_ref,
                     qs_scr, ks_scr, vs_scr, bias_scr, *, t, past_len, layer):
    qw, kw = qn_ref[...], kn_ref[...]
    scale = C_HD ** -0.5 * LOG2E
    nrb = t // ROW_BLOCK
    blk = C_BLOCK

    ks_scr[0:blk, :] = jnp.zeros((blk, LANES), BF16)
    vs_scr[0:blk, :] = jnp.zeros((blk, LANES), BF16)
    ks_scr[t + blk:t + 2 * blk, :] = jnp.zeros((blk, LANES), BF16)
    vs_scr[t + blk:t + 2 * blk, :] = jnp.zeros((blk, LANES), BF16)

    grp_rows = C_GROUP * blk

    def prep(r, carry):
        r0 = pl.multiple_of(r * ROW_BLOCK, ROW_BLOCK)
        cos2, sin2 = cos_ref[pl.ds(r0, ROW_BLOCK), :], sin_ref[pl.ds(r0, ROW_BLOCK), :]
        kn = _rope_pairs(_rms_head_pairs(k_ref[pl.ds(r0, ROW_BLOCK), :].astype(F32), kw), cos2, sin2)
        ks_scr[pl.ds(r0 + blk, ROW_BLOCK), :] = kn.astype(BF16)
        vs_scr[pl.ds(r0 + blk, ROW_BLOCK), :] = v_ref[pl.ds(r0, ROW_BLOCK), :]
        for pair in range(C_QHEADS // 2):
            qp = _rms_head_pairs(q_ref[pl.ds(r0, ROW_BLOCK), pair * LANES:(pair + 1) * LANES].astype(F32), qw)
            qp = (_rope_pairs(qp, cos2, sin2) * scale).astype(BF16)
            for half in range(2):
                hq = 2 * pair + half
                hk, g = hq // C_GROUP, hq % C_GROUP
                for sub in range(ROW_BLOCK // blk):
                    dst = pl.multiple_of((r * (ROW_BLOCK // blk) + sub) * grp_rows + g * blk, blk)
                    qs_scr[hk, pl.ds(dst, blk), :] = qp[sub * blk:(sub + 1) * blk, half * C_HD:(half + 1) * C_HD]
        return carry

    lax.fori_loop(0, nrb, prep, 0)

    nkeys = 3 * blk + past_len
    kcb = kc_ref[0, 0].astype(BF16)
    vcb = vc_ref[0, 0].astype(BF16)
    qi = lax.broadcasted_iota(jnp.int32, (grp_rows, nkeys), 0) & (blk - 1)
    kj = lax.broadcasted_iota(jnp.int32, (grp_rows, nkeys), 1)
    visible = jnp.logical_or(kj >= 3 * blk, jnp.logical_and(kj >= qi, kj <= qi + 2 * C_WINDOW))
    bias_scr[...] = jnp.where(visible, 0.0, NEG_BIG)
    head_of_row = lax.broadcasted_iota(jnp.int32, (grp_rows, 1), 0) // blk
    kcol = lax.broadcasted_iota(jnp.int32, (1, nkeys), 1)

    def qblock(n, carry):
        r0 = pl.multiple_of(n * blk, blk)
        kpos = kcol + (r0 - blk)
        in_seq = jnp.logical_or(kcol >= 3 * blk, jnp.logical_and(kpos >= 0, kpos < t))
        edge = jnp.where(in_seq, 0.0, NEG_BIG)
        kwin = ks_scr[pl.ds(r0, 3 * blk), :]
        vwin = vs_scr[pl.ds(r0, 3 * blk), :]
        scores, vals, sinks = [], [], []
        for hk in range(C_KVHEADS):
            lo, hi = hk * C_HD, (hk + 1) * C_HD
            keys = jnp.concatenate([kwin[:, lo:hi], kcb[:, lo:hi]], axis=0)
            vals.append(jnp.concatenate([vwin[:, lo:hi], vcb[:, lo:hi]], axis=0))
            q_stack = qs_scr[hk, pl.ds(pl.multiple_of(n * grp_rows, grp_rows), grp_rows), :]
            scores.append(_dot_nt(q_stack, keys))
            sink = jnp.full((grp_rows, 1), sink_ref[layer, hk * C_GROUP], F32)
            for g in range(1, C_GROUP):
                sink = jnp.where(head_of_row == g, sink_ref[layer, hk * C_GROUP + g], sink)
            sinks.append(sink * LOG2E)
        outs = []
        for hk in range(C_KVHEADS):
            o = _softmax_sink_av([scores[hk] + bias_scr[...] + edge], [vals[hk]], sinks[hk])
            outs += [o[g * blk:(g + 1) * blk] for g in range(C_GROUP)]
        for pair in range(C_QHEADS // 2):
            o_ref[pl.ds(r0, blk), pair * LANES:(pair + 1) * LANES] = jnp.concatenate(
                outs[2 * pair:2 * pair + 2], axis=1).astype(o_ref.dtype)
        return carry

    lax.fori_loop(0, t // blk, qblock, 0)


def _attn_lat_call(proj, cache_k, cache_v, prm, cos2, sin2, layer, nseq, t):
    past_len = cache_k.shape[2]
    qw = C_QHEADS * C_HD
    return pl.pallas_call(
        functools.partial(_attn_lat_kernel, t=t, past_len=past_len, layer=layer),
        out_shape=jax.ShapeDtypeStruct((nseq * t, qw), BF16),
        grid=(nseq,),
        in_specs=[pl.BlockSpec((t, qw), lambda b: (b, COL_QC // qw)),
                  pl.BlockSpec((t, LANES), lambda b: (b, COL_KC // LANES)),
                  pl.BlockSpec((t, LANES), lambda b: (b, COL_VC // LANES)),
                  pl.BlockSpec((1, 1, past_len, LANES), lambda b: (b, layer, 0, 0)),
                  pl.BlockSpec((1, 1, past_len, LANES), lambda b: (b, layer, 0, 0)),
                  _layer_block((1, LANES), layer), _layer_block((1, LANES), layer),
                  _resident((t, LANES), lambda b: (0, 0)), _resident((t, LANES), lambda b: (0, 0)),
                  pl.BlockSpec(memory_space=pltpu.SMEM)],
        out_specs=pl.BlockSpec((t, qw), lambda b: (b, 0)),
        scratch_shapes=[pltpu.VMEM((C_KVHEADS, t * C_GROUP, C_HD), BF16),
                        pltpu.VMEM((t + 2 * C_BLOCK, LANES), BF16), pltpu.VMEM((t + 2 * C_BLOCK, LANES), BF16),
                        pltpu.VMEM((C_GROUP * C_BLOCK, 3 * C_BLOCK + past_len), F32)],
        compiler_params=_cparams(("arbitrary",)),
        name="attn_lat",
    )(proj, proj, proj, cache_k, cache_v, prm["q_norm2"], prm["k_norm2"], cos2, sin2, prm["sink"])


def _rope_tables(t):
    rows = t // GRID_W
    row = jnp.repeat(jnp.arange(rows, dtype=F32), GRID_W)
    col = jnp.tile(jnp.arange(GRID_W, dtype=F32), rows)
    nf = C_HD // 4
    inv = ROPE_THETA ** (-jnp.arange(nf, dtype=F32) / nf)
    ar, ac = row[:, None] * inv, col[:, None] * inv
    cos = jnp.concatenate([jnp.cos(ar), jnp.cos(ar), jnp.cos(ac), jnp.cos(ac)], axis=1)
    sin = jnp.concatenate([-jnp.sin(ar), jnp.sin(ar), -jnp.sin(ac), jnp.sin(ac)], axis=1)
    return jnp.tile(cos, (1, 2)), jnp.tile(sin, (1, 2))


def _merge_kernel(x_ref, g0_ref, g1_ref, g2_ref, oa_ref, ob_ref, oc_ref, mod_ref, wbr_ref, wout_ref, o_ref):
    merged = (_sigmoid(g0_ref[...].astype(F32)) * jnp.dot(oa_ref[...], wbr_ref[0], preferred_element_type=F32)
              + _sigmoid(g1_ref[...].astype(F32)) * jnp.dot(ob_ref[...], wbr_ref[1], preferred_element_type=F32)
              + _sigmoid(g2_ref[...].astype(F32)) * jnp.dot(oc_ref[...], wbr_ref[2], preferred_element_type=F32))
    res = jnp.dot(merged.astype(BF16), wout_ref[...], preferred_element_type=F32)
    o_ref[...] = x_ref[...] + mod_ref[0, 2:3, :] * res


def _merge_call(x2d, proj, oa, ob, oc, prm, layer, cond, tm):
    m, d = x2d.shape
    mg = lambda r: pl.BlockSpec((tm, d), lambda i: (i, COL_MG // d + r))
    br = pl.BlockSpec((tm, BRANCH_W), lambda i: (i, 0))
    return pl.pallas_call(
        _merge_kernel,
        out_shape=jax.ShapeDtypeStruct((m, d), F32),
        grid=(m // tm,),
        in_specs=[pl.BlockSpec((tm, d), lambda i: (i, 0)), mg(0), mg(1), mg(2), br, br, br,
                  _mod_block(d, layer, cond), _layer_block((3, BRANCH_W, d), layer), _layer_block((d, d), layer)],
        out_specs=pl.BlockSpec((tm, d), lambda i: (i, 0)),
        compiler_params=_cparams(("arbitrary",)),
        name="merge",
    )(x2d, proj, proj, proj, oa, ob, oc, prm["mod"], prm["w_branch"], prm["w_out"])


FF_CHUNK = 256
HALO = BF16_ROWS


def _ffn_kernel(x_ref, xp_ref, xn_ref, mod_ref, nw_ref, wup_ref, cw_ref, wd_ref, o_ref, h_scr, act_scr, *,
                tm, seq_len):
    i = pl.program_id(0)
    nseg = max(1, tm // seq_len)
    seg = tm // nseg
    nw, sh, sc = nw_ref[...], mod_ref[0, 3:4, :], mod_ref[0, 4:5, :]

    def norm(x):
        return _rms_rows(x, nw) * (1.0 + sc) + sh

    has_prev = ((i * tm) & (seq_len - 1)) != 0
    has_next = (((i + 1) * tm) & (seq_len - 1)) != 0
    zero_halo = jnp.zeros((HALO, x_ref.shape[1]), BF16)
    for s in range(nseg):
        h_scr[s, HALO:HALO + seg, :] = norm(x_ref[s * seg:(s + 1) * seg, :]).astype(BF16)
        if s == 0:
            h_scr[s, 0:HALO, :] = (norm(xp_ref[...]) * jnp.where(has_prev, 1.0, 0.0)).astype(BF16)
        else:
            h_scr[s, 0:HALO, :] = zero_halo
        if s == nseg - 1:
            h_scr[s, HALO + seg:2 * HALO + seg, :] = (norm(xn_ref[...]) * jnp.where(has_next, 1.0, 0.0)).astype(BF16)
        else:
            h_scr[s, HALO + seg:2 * HALO + seg, :] = zero_halo

    def conv(u, cw):
        return (u[HALO - 1:HALO - 1 + seg] * cw[0:1, :] + u[HALO:HALO + seg] * cw[1:2, :]
                + u[HALO + 1:HALO + 1 + seg] * cw[2:3, :])

    for lo, hi in _col_chunks(D_FF, FF_CHUNK):
        for s in range(nseg):
            h = h_scr[s]
            a = conv(jnp.dot(h, wup_ref[:, lo:hi], preferred_element_type=F32), cw_ref[:, lo:hi])
            u = conv(jnp.dot(h, wup_ref[:, D_FF + lo:D_FF + hi], preferred_element_type=F32),
                     cw_ref[:, D_FF + lo:D_FF + hi])
            act_scr[s * seg:(s + 1) * seg, lo:hi] = (_silu(a) * u).astype(BF16)

    o_ref[...] = x_ref[...] + mod_ref[0, 5:6, :] * jnp.dot(act_scr[...], wd_ref[...], preferred_element_type=F32)


def _ffn_call(x2d, prm, layer, cond, tm, seq_len):
    m, d = x2d.shape
    hb = tm // HALO
    last = m // HALO - 1
    nseg = max(1, tm // seq_len)
    seg = tm // nseg
    return pl.pallas_call(
        functools.partial(_ffn_kernel, tm=tm, seq_len=seq_len),
        out_shape=jax.ShapeDtypeStruct((m, d), F32),
        grid=(m // tm,),
        in_specs=[pl.BlockSpec((tm, d), lambda i: (i, 0)),
                  pl.BlockSpec((HALO, d), lambda i: (jnp.maximum(i * hb - 1, 0), 0)),
                  pl.BlockSpec((HALO, d), lambda i: (jnp.minimum((i + 1) * hb, last), 0)),
                  _mod_block(d, layer, cond), _layer_block((1, d), layer), _layer_block((d, 2 * D_FF), layer),
                  _layer_block((3, 2 * D_FF), layer), _layer_block((D_FF, d), layer)],
        out_specs=pl.BlockSpec((tm, d), lambda i: (i, 0)),
        scratch_shapes=[pltpu.VMEM((nseg, seg + 2 * HALO, d), BF16), pltpu.VMEM((tm, D_FF), BF16)],
        compiler_params=_cparams(("arbitrary",)),
        name="ffn",
    )(x2d, x2d, x2d, prm["mod"], prm["norm2_w"], prm["w_up"], prm["conv_ffn"], prm["w_down"])


def _permute_w_in(w):
    s = _SRC
    w16 = jnp.concatenate([w[..., s["mg"]:s["end"]], w[..., s["qa"]:s["beta"]], w[..., s["qb"]:s["fb"]],
                           w[..., s["gb"]:s["mg"]]], axis=-1)
    n_gate = s["qb"] - s["beta"]
    gates = jnp.pad(w[..., s["beta"]:s["qb"]], ((0, 0), (0, 0), (0, LANES - n_gate)))
    w32 = jnp.concatenate([w[..., s["fb"]:s["gb"]], gates], axis=-1)
    return w16.astype(BF16), w32.astype(BF16)


def _gate_rows(proj32, nseq, t, c):
    g = proj32[:, COL32_GATES:COL32_GATES + 4 * HEADS].reshape(nseq, t // c, c, 2, 2, HEADS)
    return jnp.transpose(g, (0, 5, 1, 3, 4, 2)).reshape(nseq, HEADS, t // c, 4, c)


def _row_tile(rows, t):
    tm = 512
    while rows % tm or (t % tm and tm % t):
        tm //= 2
    return tm


def _group_forward(x3d, first_cond, shared_cond, prm, past, tables):
    nseq, t, d = x3d.shape
    x = x3d.reshape(nseq * t, d)
    tm = _row_tile(nseq * t, t)
    tiles_per_seq = max(1, t // tm)
    cond = (lambda i: first_cond) if shared_cond else (lambda i: first_cond + i // tiles_per_seq)
    emit = past is None
    states_a, states_b, keys, vals = [], [], [], []
    for l in range(prm["w16"].shape[0]):
        proj, proj32 = _in_proj_call(x, prm, l, cond, tm)
        gates_r = _gate_rows(proj32, nseq, t, min(CHUNK_A, t))
        res_ab = _mixer_ab_call(proj, proj32, gates_r, prm, past, l, nseq, t, emit)
        if emit:
            oc, kn, vn = _attn_ctx_call(proj, prm, l, nseq, t)
            states_a.append(res_ab[2])
            states_b.append(res_ab[3])
            keys.append(kn.reshape(nseq, t, C_KVHEADS, C_HD))
            vals.append(vn.reshape(nseq, t, C_KVHEADS, C_HD))
        else:
            oc = _attn_lat_call(proj, past[2], past[3], prm, tables[0], tables[1], l, nseq, t)
        x = _merge_call(x, proj, res_ab[0], res_ab[1], oc, prm, l, cond, tm)
        x = _ffn_call(x, prm, l, cond, tm, t)
    return x.reshape(nseq, t, d), states_a, states_b, keys, vals


def kernel(x_prompt, x_sample, state_delta, state_hgrn, cache_k, cache_v, c, c_ctx, ada_w, ada_b, norm1_w, w_in, conv_a, a_log, dt_bias, norm_a, lb_logits, norm_b, q_norm, k_norm, sink, w_branch, w_out, norm2_w, w_up, conv_ffn, w_down):
    depth = w_in.shape[0]
    d = x_prompt.shape[-1]

    cond = jnp.concatenate([c_ctx[None, :], c], axis=0)
    rows = -(-cond.shape[0] // SUBLANES) * SUBLANES
    cond = jnp.pad(cond, ((0, rows - cond.shape[0]), (0, 0)))
    mod_all = _mod_call(cond, ada_w, ada_b).reshape(depth, rows, 6, d)

    w16, w32 = _permute_w_in(w_in)
    prm = dict(
        mod=mod_all, w16=w16, w32=w32,
        norm1_w=norm1_w.reshape(depth, 1, d), norm2_w=norm2_w.reshape(depth, 1, d),
        conv_a=conv_a, a_log=a_log, dt_bias=dt_bias, norm_a=norm_a.reshape(depth, 1, HEAD_W),
        lb_logits=lb_logits, norm_b=norm_b.reshape(depth, 1, HEAD_W),
        q_norm2=jnp.tile(q_norm, (1, 2)).reshape(depth, 1, LANES), k_norm2=jnp.tile(k_norm, (1, 2)).reshape(depth, 1, LANES),
        sink=sink, w_branch=w_branch.astype(BF16), w_out=w_out.astype(BF16),
        w_up=w_up.astype(BF16), conv_ffn=conv_ffn, w_down=w_down.astype(BF16))

    y_prompt, st_a, st_b, keys, vals = _group_forward(x_prompt, 0, True, prm, None, None)

    past_len = cache_k.shape[2]
    past = (state_delta, state_hgrn,
            cache_k.reshape(cache_k.shape[0], depth, past_len, C_KVHEADS * C_HD),
            cache_v.reshape(cache_v.shape[0], depth, past_len, C_KVHEADS * C_HD))
    y_sample, _, _, _, _ = _group_forward(x_sample, 1, False, prm, past, _rope_tables(x_sample.shape[1]))

    return (y_prompt, y_sample, jnp.stack(st_a, axis=1), jnp.stack(st_b, axis=1),
            jnp.stack(keys, axis=1), jnp.stack(vals, axis=1))
```

```python
import functools
import math

import numpy as np
import jax
import jax.numpy as jnp
from jax import lax
from jax.experimental import pallas as pl
from jax.experimental.pallas import tpu as pltpu

F32 = jnp.float32
BF16 = jnp.bfloat16

D_MODEL = 1024
NORM_EPS = 1e-6
LB_EPS = 1e-6
NEG_BIG = -1e30
LOG2E = 1.4426950408889634
GRID_W = 64
ROPE_THETA = 10000.0

HEADS = 4
HEAD_W = 128
SHORT_CONV = 5
C_QHEADS = 8
C_KVHEADS = 2
C_GROUP = C_QHEADS // C_KVHEADS
C_HD = 64
C_WINDOW = 128
C_BLOCK = 128
BRANCH_W = 512
D_FF = 2816

LANES = 128
SUBLANES = 8
BF16_ROWS = 16
VMEM_LIMIT = 56 * 1024 * 1024

COL_MG = 0
COL_QA = 3072
COL_KA = 3584
COL_VA = 4096
COL_GA = 4608
COL_QB = 5120
COL_IB = 5632
COL_GB = 6144
COL_QC = 6656
COL_KC = 7168
COL_VC = 7296
PROJ_W = 7424
COL32_FB = 0
COL32_GATES = 1024
PROJ32_W = 1152
_SRC = dict(qa=0, ka=512, va=1024, ga=1536, beta=2048, alpha=2056, qb=2064, ib=2576, fb=3088, gb=4112,
            qc=4624, kc=5136, vc=5264, mg=5392, end=8464)

CHUNK_A = 128
ROUNDS_A = 8
CHUNK_B = 64
TRI_BASE = 16
TRI_MERGE = 2
MM_TILE = 768


def _cparams(sem):
    return pltpu.CompilerParams(dimension_semantics=sem, vmem_limit_bytes=VMEM_LIMIT)


def _resident(shape, index_map):
    return pl.BlockSpec(shape, index_map, pipeline_mode=pl.Buffered(1))


def _layer_block(shape, layer):
    return _resident((None,) + tuple(shape), lambda *_: (layer,) + (0,) * len(shape))


def _mod_block(d, layer, cond):
    return pl.BlockSpec((None, 1, 6, d), lambda i: (layer, cond(i), 0, 0))


def _dot(a, b):
    return jnp.dot(a.astype(BF16), b.astype(BF16), preferred_element_type=F32)


def _dot_nt(a, b):
    return lax.dot_general(a.astype(BF16), b.astype(BF16), (((1,), (1,)), ((), ())), preferred_element_type=F32)


def _dot_tn(a, b):
    return lax.dot_general(a.astype(BF16), b.astype(BF16), (((0,), (0,)), ((), ())), preferred_element_type=F32)


def _sigmoid(x):
    return 1.0 / (1.0 + jnp.exp(-x))


def _silu(x):
    return x * _sigmoid(x)


def _softplus(x):
    return jnp.maximum(x, 0.0) + jnp.log(1.0 + jnp.exp(-jnp.abs(x)))


def _rms_rows(x, w):
    ms = jnp.mean(x * x, axis=-1, keepdims=True)
    return x * lax.rsqrt(ms + NORM_EPS) * w


def _split_bf16(x, n):
    parts, r = [], x
    for _ in range(n):
        p = r.astype(BF16)
        parts.append(p)
        r = r - p.astype(F32)
    return parts


def _col_chunks(width, step):
    return [(lo, min(lo + step, width)) for lo in range(0, width, step)]


def _mod_kernel(c_ref, w_ref, b_ref, o_ref):
    c = c_ref[...]
    o_ref[0] = _dot(_silu(c), w_ref[0]) + b_ref[0]


def _mod_call(cond, ada_w, ada_b):
    depth, d, n = ada_w.shape
    rows = cond.shape[0]
    tn = 768
    return pl.pallas_call(
        _mod_kernel,
        out_shape=jax.ShapeDtypeStruct((depth, rows, n), F32),
        grid=(depth, n // tn),
        in_specs=[pl.BlockSpec((rows, d), lambda l, j: (0, 0)),
                  pl.BlockSpec((1, d, tn), lambda l, j: (l, 0, j)),
                  pl.BlockSpec((1, 1, tn), lambda l, j: (l, 0, j))],
        out_specs=pl.BlockSpec((1, rows, tn), lambda l, j: (l, 0, j)),
        compiler_params=_cparams(("arbitrary", "arbitrary")),
        name="mod",
    )(cond, ada_w, ada_b.reshape(depth, 1, n))


def _in_proj_kernel(x_ref, mod_ref, nw_ref, w16_ref, w32_ref, o16_ref, o32_ref):
    h = _rms_rows(x_ref[...], nw_ref[...]) * (1.0 + mod_ref[0, 1:2, :]) + mod_ref[0, 0:1, :]
    hb = h.astype(BF16)
    for lo, hi in _col_chunks(PROJ_W, MM_TILE):
        o16_ref[:, lo:hi] = jnp.dot(hb, w16_ref[:, lo:hi], preferred_element_type=F32).astype(BF16)
    for lo, hi in _col_chunks(PROJ32_W, MM_TILE):
        o32_ref[:, lo:hi] = jnp.dot(hb, w32_ref[:, lo:hi], preferred_element_type=F32)


def _in_proj_call(x2d, prm, layer, cond, tm):
    m, d = x2d.shape
    return pl.pallas_call(
        _in_proj_kernel,
        out_shape=(jax.ShapeDtypeStruct((m, PROJ_W), BF16), jax.ShapeDtypeStruct((m, PROJ32_W), F32)),
        grid=(m // tm,),
        in_specs=[pl.BlockSpec((tm, d), lambda i: (i, 0)), _mod_block(d, layer, cond),
                  _layer_block((1, d), layer), _layer_block((d, PROJ_W), layer), _layer_block((d, PROJ32_W), layer)],
        out_specs=(pl.BlockSpec((tm, PROJ_W), lambda i: (i, 0)),
                   pl.BlockSpec((tm, PROJ32_W), lambda i: (i, 0))),
        compiler_params=_cparams(("arbitrary",)),
        name="in_proj",
    )(x2d, prm["mod"], prm["norm1_w"], prm["w16"], prm["w32"])


ROW_BLOCK = 256


def _gated_norm_epilogue(of_scr, ob_scr, gate_ref, nw_ref, o_ref, t, cols):
    nw = nw_ref[...]

    def body(r, carry):
        r0 = pl.multiple_of(r * ROW_BLOCK, ROW_BLOCK)
        o = of_scr[pl.ds(r0, ROW_BLOCK), :] + ob_scr[pl.ds(r0, ROW_BLOCK), :]
        y = _rms_rows(o, nw) * _silu(gate_ref[pl.ds(r0, ROW_BLOCK), cols].astype(F32))
        o_ref[pl.ds(r0, ROW_BLOCK), cols] = y.astype(o_ref.dtype)
        return carry

    lax.fori_loop(0, t // ROW_BLOCK, body, 0)


def _tri_inverse(mats, ri, ci, c, tick):
    n = range(len(mats))
    shift = int(np.log2(TRI_BASE))
    base = (ri >> shift) == (ci >> shift)
    mb = [jnp.where(base, m, 0.0) for m in mats]
    y = [-m for m in mb]
    p = [_dot(m, m) for m in mb]
    tick()
    for _ in range(shift - 2):
        yp = [_dot(jnp.concatenate([y[i], p[i]], axis=0), p[i]) for i in n]
        tick()
        y = [y[i] + p[i] + yp[i][:c] for i in n]
        p = [r[c:] for r in yp]
    yp = [_dot(y[i], p[i]) for i in n]
    tick()
    y = [y[i] + p[i] + yp[i] for i in n]
    total = int(np.log2(c))
    while shift < total:
        factors = min(TRI_MERGE, total - shift)
        inner = (ri >> shift) == (ci >> shift)
        outer = (ri >> (shift + factors)) == (ci >> (shift + factors))
        between = jnp.logical_and(outer, jnp.logical_not(inner))
        cm = [jnp.where(between, m, 0.0) for m in mats]
        w = [_dot(y[i], cm[i]) for i in n]
        tick()
        w = [cm[i] + w[i] for i in n]
        if factors == 1:
            wy = [_dot(w[i], y[i]) for i in n]
            tick()
            y = [y[i] - w[i] - wy[i] for i in n]
        else:
            r = [_dot(w[i], jnp.concatenate([w[i], y[i]], axis=1)) for i in n]
            tick()
            p, y = [x[:, :c] for x in r], [y[i] - w[i] - r[i][:, c:] for i in n]
            for f in range(1, factors):
                if f + 1 < factors:
                    r = [_dot(p[i], jnp.concatenate([p[i], y[i]], axis=1)) for i in n]
                    tick()
                    p, y = [x[:, :c] for x in r], [y[i] + p[i] + r[i][:, c:] for i in n]
                else:
                    py = [_dot(p[i], y[i]) for i in n]
                    tick()
                    y = [y[i] + p[i] + py[i] for i in n]
        shift += factors
    eye = jnp.where(ri == ci, 1.0, 0.0)
    return [eye + v for v in y]


def _delta_chunks(probs, states, c, tick):
    n = range(len(probs))
    ri = lax.broadcasted_iota(jnp.int32, (c, c), 0)
    ci = lax.broadcasted_iota(jnp.int32, (c, c), 1)
    eye = ri == ci
    pre = []
    for q, k, v, beta_row, alpha_row, a_neg, dt_b, chain, reverse in probs:
        incl, strict = (ri <= ci, ri < ci) if reverse else (ri >= ci, ri > ci)
        beta_r = _sigmoid(beta_row)
        g_r = a_neg * _softplus(alpha_row + dt_b)
        gc_col = jnp.sum(jnp.where(incl, jnp.broadcast_to(g_r, (c, c)), 0.0), axis=1, keepdims=True)
        beta_col = jnp.sum(jnp.where(eye, jnp.broadcast_to(beta_r, (c, c)), 0.0), axis=1, keepdims=True)
        gc_row = jnp.sum(jnp.where(eye, jnp.broadcast_to(gc_col, (c, c)), 0.0), axis=0, keepdims=True)
        g_tot = jnp.sum(g_r, axis=1, keepdims=True)
        decay = jnp.where(incl, jnp.exp(jnp.where(incl, gc_col - gc_row, 0.0)), 0.0)
        pre.append((strict, gc_col, beta_col, g_tot, decay, jnp.exp(gc_col)))

    kb = [pr[1].astype(BF16) for pr in probs]
    kq = [_dot_nt(jnp.concatenate([kb[i], probs[i][0].astype(BF16)], axis=0), kb[i]) for i in n]
    tick()
    t_inv = _tri_inverse([jnp.where(pre[i][0], kq[i][:c] * pre[i][2] * pre[i][4], 0.0) for i in n], ri, ci, c, tick)
    uw = [_dot(t_inv[i], jnp.concatenate([probs[i][2] * pre[i][2], probs[i][1] * (pre[i][2] * pre[i][5])], axis=1))
          for i in n]
    tick()
    wq_lhs = [jnp.concatenate([uw[i][:, HEAD_W:], probs[i][0] * pre[i][5]], axis=0).astype(BF16) for i in n]
    os_lhs = [jnp.concatenate([kq[i][c:] * pre[i][4], (probs[i][1] * jnp.exp(pre[i][3] - pre[i][1])).T],
                              axis=0).astype(BF16) for i in n]
    states = list(states)
    outs = [None] * len(probs)
    todo = list(n)
    while todo:
        front, seen = [], set()
        for i in todo:
            if probs[i][7] not in seen:
                seen.add(probs[i][7])
                front.append(i)
        todo = [i for i in todo if i not in front]
        wq = [_dot(wq_lhs[i], states[probs[i][7]]) for i in front]
        tick()
        v_new = [uw[i][:, :HEAD_W] - wq[j][:c] for j, i in enumerate(front)]
        os_ = [_dot(os_lhs[i], v_new[j]) for j, i in enumerate(front)]
        tick()
        for j, i in enumerate(front):
            outs[i] = wq[j][c:] + os_[j][:c]
            states[probs[i][7]] = states[probs[i][7]] * jnp.exp(pre[i][3]) + os_[j][c:]
    return outs, states


def _conv_silu_pass(x_ref, cw_ref, xp_scr, dst_scr, t, l2, scale, cols):
    nrb = t // ROW_BLOCK
    pad = SUBLANES
    half = SHORT_CONV // 2

    def cp(r, carry):
        r0 = pl.multiple_of(r * ROW_BLOCK, ROW_BLOCK)
        xp_scr[pl.ds(r0 + pad, ROW_BLOCK), :] = x_ref[pl.ds(r0, ROW_BLOCK), cols].astype(F32)
        return carry

    lax.fori_loop(0, nrb, cp, 0)
    cw = cw_ref[:, cols]

    def body(r, carry):
        r0 = pl.multiple_of(r * ROW_BLOCK, ROW_BLOCK)
        y = None
        for j in range(SHORT_CONV):
            term = xp_scr[pl.ds(r0 + (pad - half + j), ROW_BLOCK), :] * cw[j:j + 1, :]
            y = term if y is None else y + term
        y = _silu(y)
        if l2:
            y = y * (lax.rsqrt(jnp.sum(y * y, axis=-1, keepdims=True) + NORM_EPS) * scale)
        dst_scr[pl.ds(r0, ROW_BLOCK), :] = y
        return carry

    lax.fori_loop(0, nrb, body, 0)


def _hgrn_diag(q, kf, v, b, reverse, c):
    nb = c // SUBLANES
    q3, k3, v3, b3 = (a.reshape(nb, SUBLANES, HEAD_W) for a in (q, kf, v, b))
    sub = lax.broadcasted_iota(jnp.int32, (nb, SUBLANES, HEAD_W), 1)
    o3 = jnp.zeros((nb, SUBLANES, HEAD_W), F32)
    for j in range(SUBLANES):
        mask = (sub <= j) if reverse else (sub >= j)
        e = jnp.exp2(jnp.where(mask, b3 - b3[:, j:j + 1, :], NEG_BIG))
        a = jnp.sum(q3 * e * k3[:, j:j + 1, :], axis=-1, keepdims=True)
        o3 = o3 + a * v3[:, j:j + 1, :]
    return o3.reshape(c, HEAD_W)


def _hgrn_chunk(q, kf, v, lf, st, reverse, c):
    ri = lax.broadcasted_iota(jnp.int32, (c, c), 0)
    ci = lax.broadcasted_iota(jnp.int32, (c, c), 1)
    incl = (ri <= ci) if reverse else (ri >= ci)
    parts = jnp.concatenate(_split_bf16(lf, 3), axis=1)
    b3 = jnp.dot(jnp.where(incl, 1.0, 0.0).astype(BF16), parts, preferred_element_type=F32)
    b = b3[:, :HEAD_W] + b3[:, HEAD_W:2 * HEAD_W] + b3[:, 2 * HEAD_W:]
    b_tot = jnp.sum(lf, axis=0, keepdims=True)
    o = _dot_nt(q * jnp.exp2(b), st)
    row = lax.broadcasted_iota(jnp.int32, (c, 1), 0)
    att = jnp.zeros((c, c), F32)
    n = SUBLANES
    while n < c:
        pieces = []
        for g in range(c // (2 * n)):
            r = g * 2 * n + (n if reverse else n - 1)
            pieces.append(jnp.broadcast_to(b[r:r + 1, :], (2 * n, HEAD_W)))
        ref = pieces[0] if len(pieces) == 1 else jnp.concatenate(pieces, axis=0)
        e = jnp.exp2(-jnp.abs(b - ref))
        s = int(np.log2(n))
        second = ((row >> s) & 1) == 1
        q_part = jnp.logical_not(second) if reverse else second
        qt = jnp.where(q_part, q * e, 0.0)
        kt = jnp.where(q_part, 0.0, kf * e)
        att = att + jnp.where((ri >> (s + 1)) == (ci >> (s + 1)), _dot_nt(qt, kt), 0.0)
        n *= 2
    o = o + _dot(att, v) + _hgrn_diag(q, kf, v, b, reverse, c)
    kh = kf * jnp.exp2(b_tot - b)
    st_new = st * jnp.exp2(b_tot) + _dot_tn(v, kh)
    return o, st_new


def _mixer_ab_kernel(*refs, t, ca, cb, layer, hps, has_past, emit_state):
    refs = list(refs)
    (qa_ref, ka_ref, va_ref, ga_ref, gr_ref, cwq_ref, cwk_ref, cwv_ref, alog_ref, dtb_ref, nwa_ref,
     qb_ref, ib_ref, f0_ref, f1_ref, gb_ref, lb_ref, nwb_ref) = refs[:18]
    pos = 18
    sa0_ref = sb0_ref = None
    if has_past:
        sa0_ref, sb0_ref = refs[pos:pos + 2]
        pos += 2
    oa_ref, ob_ref = refs[pos:pos + 2]
    pos += 2
    sfa_ref = sfb_ref = None
    if emit_state:
        sfa_ref, sfb_ref = refs[pos:pos + 2]
        pos += 2
    xp_scr, qn_scr, kn_scr, vn_scr, af_scr, ab_scr, bf_scr, bb_scr, sa_scr, sb_scr = refs[pos:]

    heads = range(hps)
    cols = [slice(hh * HEAD_W, (hh + 1) * HEAD_W) for hh in heads]
    first_head = pl.program_id(1) * hps
    pad = SUBLANES
    xp_scr[0:pad, :] = jnp.zeros((pad, HEAD_W), F32)
    xp_scr[t + pad:t + 2 * pad, :] = jnp.zeros((pad, HEAD_W), F32)
    for hh in heads:
        _conv_silu_pass(qa_ref, cwq_ref, xp_scr, qn_scr.at[hh], t, True, HEAD_W ** -0.5, cols[hh])
        _conv_silu_pass(ka_ref, cwk_ref, xp_scr, kn_scr.at[hh], t, True, 1.0, cols[hh])
        _conv_silu_pass(va_ref, cwv_ref, xp_scr, vn_scr.at[hh], t, False, 1.0, cols[hh])

    if has_past:
        for hh in heads:
            for d in range(2):
                sa_scr[2 * hh + d] = sa0_ref[0, 0, d, hh]
                sb_scr[2 * hh + d] = sb0_ref[0, 0, d, hh].T
    else:
        sa_scr[...] = jnp.zeros(sa_scr.shape, F32)
        sb_scr[...] = jnp.zeros(sb_scr.shape, F32)

    ones = jnp.ones((1, ca), F32)
    a_neg = [[-jnp.exp(ones * alog_ref[layer, d, first_head + hh]) for d in range(2)] for hh in heads]
    dt_b = [[dtb_ref[layer, d, first_head + hh] for d in range(2)] for hh in heads]

    lb_terms = None
    if layer > 0:
        lb_terms = []
        for hh in heads:
            terms = []
            for d in range(2):
                lg = lb_ref[d][:, cols[hh]]
                ex = jnp.exp(lg - jnp.max(lg, axis=0, keepdims=True))
                pr = ex / jnp.sum(ex, axis=0, keepdims=True)
                lb = jnp.clip(jnp.sum(pr[1:layer + 1], axis=0, keepdims=True), LB_EPS, 1.0 - LB_EPS)
                terms.append((jnp.log(lb) * LOG2E, jnp.log1p(-lb) * LOG2E, 1.0 - lb))
            lb_terms.append(terms)

    f_refs = (f0_ref, f1_ref)
    nca, ncb = t // ca, t // cb
    rounds = math.gcd(ROUNDS_A // hps, nca)
    ratio = rounds * ca // cb

    def hgrn_gates(z, hh, d):
        z2 = z * LOG2E
        y = jnp.exp2(-jnp.abs(z2))
        one_y = 1.0 + y
        l2_sig = jnp.minimum(z2, 0.0) - jnp.log(one_y) * LOG2E
        sig_neg = jnp.where(z >= 0.0, y, 1.0) / one_y
        if layer == 0:
            return l2_sig, sig_neg
        l2_lb, l2_1m_lb, one_m_lb = lb_terms[hh][d]
        a2 = l2_1m_lb + l2_sig
        return (jnp.maximum(l2_lb, a2) + jnp.log(1.0 + jnp.exp2(-jnp.abs(l2_lb - a2))) * LOG2E,
                one_m_lb * sig_neg)

    def step(n, carry):
        a_dst, a_probs = [], []
        for rnd in range(rounds):
            for hh in heads:
                for d in range(2):
                    cidx = n * rounds + rnd if d == 0 else nca - 1 - (n * rounds + rnd)
                    r0 = pl.multiple_of(cidx * ca, ca)
                    gr = gr_ref[0, hh, cidx]
                    a_dst.append((af_scr if d == 0 else ab_scr, hh, r0))
                    a_probs.append((qn_scr[hh, pl.ds(r0, ca), :], kn_scr[hh, pl.ds(r0, ca), :],
                                    vn_scr[hh, pl.ds(r0, ca), :], gr[d:d + 1, :], gr[2 + d:3 + d, :],
                                    a_neg[hh][d], dt_b[hh][d], 2 * hh + d, d == 1))
        sa = [sa_scr[chain] for chain in range(2 * hps)]
        b_jobs = []
        for j in range(ratio):
            m = n * ratio + j
            for hh in heads:
                for d in range(2):
                    r0 = pl.multiple_of((m if d == 0 else ncb - 1 - m) * cb, cb)
                    b_jobs.append((hh, d, r0, f_refs[d][pl.ds(r0, cb), cols[hh]], qb_ref[pl.ds(r0, cb), cols[hh]],
                                   ib_ref[pl.ds(r0, cb), cols[hh]]))
        sb = [sb_scr[chain] for chain in range(2 * hps)]

        b_out = []
        pending = list(b_jobs)

        def tick():
            if pending:
                hh, d, _, z, qraw, iraw = pending.pop(0)
                lf, kf = hgrn_gates(z, hh, d)
                o, sb[2 * hh + d] = _hgrn_chunk(_silu(qraw.astype(F32)), kf, iraw.astype(F32), lf, sb[2 * hh + d],
                                                d == 1, cb)
                b_out.append(o)

        a_out, sa = _delta_chunks(a_probs, sa, ca, tick)
        while pending:
            tick()

        for (dst, hh, r0), o in zip(a_dst, a_out):
            dst[hh, pl.ds(r0, ca), :] = o
        for (hh, d, r0, _, _, _), o in zip(b_jobs, b_out):
            (bf_scr if d == 0 else bb_scr)[hh, pl.ds(r0, cb), :] = o
        for chain in range(2 * hps):
            sa_scr[chain] = sa[chain]
            sb_scr[chain] = sb[chain]
        return carry

    lax.fori_loop(0, nca // rounds, step, 0)

    for hh in heads:
        _gated_norm_epilogue(af_scr.at[hh], ab_scr.at[hh], ga_ref, nwa_ref, oa_ref, t, cols[hh])
        _gated_norm_epilogue(bf_scr.at[hh], bb_scr.at[hh], gb_ref, nwb_ref, ob_ref, t, cols[hh])
    if emit_state:
        for hh in heads:
            for d in range(2):
                sfa_ref[0, d, hh] = sa_scr[2 * hh + d]
                sfb_ref[0, d, hh] = sb_scr[2 * hh + d].T


def _mixer_ab_call(proj, proj32, gates_r, prm, past, layer, nseq, t, emit_state):
    ca, cb = min(CHUNK_A, t), min(CHUNK_B, t)
    depth = prm["lb_logits"].shape[1]
    per_head = t * HEAD_W * (7 * 4 + 2 * (7 * 2 + 2 * 4) + 2 * 2 * 2)
    hps = max(n for n in (1, 2, 4) if n == 1 or n * per_head <= VMEM_LIMIT // 2)
    slab = hps * HEAD_W
    col = lambda off: (lambda b, h: (b, off // slab + h))
    seq_in = lambda off: pl.BlockSpec((t, slab), col(off))
    conv = lambda part: pl.BlockSpec((None, SHORT_CONV, slab), lambda b, h: (layer, 0, part * (HEADS // hps) + h))
    smem = pl.BlockSpec(memory_space=pltpu.SMEM)
    norm = _layer_block((1, HEAD_W), layer)
    in_specs = [seq_in(COL_QA), seq_in(COL_KA), seq_in(COL_VA), seq_in(COL_GA),
                pl.BlockSpec((1, hps, t // ca, 4, ca), lambda b, h: (b, h, 0, 0, 0)),
                conv(0), conv(1), conv(2), smem, smem, norm,
                seq_in(COL_QB), seq_in(COL_IB), seq_in(COL32_FB), seq_in(COL32_FB + HEADS * HEAD_W), seq_in(COL_GB),
                pl.BlockSpec((2, depth, slab), lambda b, h: (0, 0, h)), norm]
    args = [proj, proj, proj, proj, gates_r, prm["conv_a"], prm["conv_a"], prm["conv_a"],
            prm["a_log"], prm["dt_bias"], prm["norm_a"],
            proj, proj, proj32, proj32, proj, prm["lb_logits"], prm["norm_b"]]
    state_in = pl.BlockSpec((1, 1, 2, hps, HEAD_W, HEAD_W), lambda b, h: (b, layer, 0, h, 0, 0))
    if past is not None:
        in_specs += [state_in, state_in]
        args += [past[0], past[1]]
    o_shape = jax.ShapeDtypeStruct((nseq * t, HEADS * HEAD_W), BF16)
    o_spec = pl.BlockSpec((t, slab), lambda b, h: (b, h))
    out_shape, out_specs = [o_shape, o_shape], [o_spec, o_spec]
    if emit_state:
        s_shape = jax.ShapeDtypeStruct((nseq, 2, HEADS, HEAD_W, HEAD_W), F32)
        s_spec = pl.BlockSpec((1, 2, hps, HEAD_W, HEAD_W), lambda b, h: (b, 0, h, 0, 0))
        out_shape += [s_shape, s_shape]
        out_specs += [s_spec, s_spec]
    seq = pltpu.VMEM((hps, t, HEAD_W), F32)
    state = pltpu.VMEM((2 * hps, HEAD_W, HEAD_W), F32)
    return pl.pallas_call(
        functools.partial(_mixer_ab_kernel, t=t, ca=ca, cb=cb, layer=layer, hps=hps, has_past=past is not None,
                          emit_state=emit_state),
        out_shape=tuple(out_shape), grid=(nseq, HEADS // hps), in_specs=in_specs, out_specs=tuple(out_specs),
        scratch_shapes=[pltpu.VMEM((t + 2 * SUBLANES, HEAD_W), F32), seq, seq, seq, seq, seq, seq, seq, state, state],
        compiler_params=_cparams(("arbitrary", "arbitrary")),
        name="mixer_ab",
    )(*args)


def _rms_head_pairs(x, w2):
    lane = lax.broadcasted_iota(jnp.int32, x.shape, 1)
    left = lane < C_HD
    sq = x * x
    s0 = jnp.sum(jnp.where(left, sq, 0.0), axis=-1, keepdims=True)
    s1 = jnp.sum(jnp.where(left, 0.0, sq), axis=-1, keepdims=True)
    ms = jnp.where(left, s0, s1) * (1.0 / C_HD)
    return x * lax.rsqrt(ms + NORM_EPS) * w2


def _rope_pairs(x, cos2, sin2):
    lane = lax.broadcasted_iota(jnp.int32, x.shape, 1)
    quarter = C_HD // 4
    swapped = jnp.where((lane & (2 * quarter - 1)) < quarter,
                        pltpu.roll(x, LANES - quarter, axis=1), pltpu.roll(x, quarter, axis=1))
    return x * cos2 + swapped * sin2


def _softmax_sink_av(scores, values, sink):
    m = sink
    for s in scores:
        m = jnp.maximum(m, jnp.max(s, axis=-1, keepdims=True))
    den = jnp.exp2(sink - m)
    acc = None
    for s, v in zip(scores, values):
        p = jnp.exp2(s - m)
        den = den + jnp.sum(p, axis=-1, keepdims=True)
        t = jnp.dot(p.astype(BF16), v, preferred_element_type=F32)
        acc = t if acc is None else acc + t
    return acc / den


def _attn_ctx_kernel(q_ref, k_ref, v_ref, qn_ref, kn_ref, sink_ref, o_ref, ko_ref, vo_ref, *, t, layer):
    qw, kw = qn_ref[...], kn_ref[...]
    kn = _rms_head_pairs(k_ref[...].astype(F32), kw)
    ko_ref[0] = kn
    v = v_ref[...]
    vo_ref[0] = v.astype(F32)
    knb, vb = kn.astype(BF16), v
    scale = C_HD ** -0.5 * LOG2E
    for pair in range(C_QHEADS // 2):
        qp = (_rms_head_pairs(q_ref[:, pair * LANES:(pair + 1) * LANES].astype(F32), qw) * scale).astype(BF16)
        outs = []
        for half in range(2):
            hq = 2 * pair + half
            hk = hq // C_GROUP
            qh = qp[:, half * C_HD:(half + 1) * C_HD]
            s = _dot_nt(qh, knb[:, hk * C_HD:(hk + 1) * C_HD])
            sink = jnp.full((1, 1), sink_ref[layer, hq] * LOG2E, F32)
            outs.append(_softmax_sink_av([s], [vb[:, hk * C_HD:(hk + 1) * C_HD]], sink))
        o_ref[:, pair * LANES:(pair + 1) * LANES] = jnp.concatenate(outs, axis=1).astype(o_ref.dtype)


def _attn_ctx_call(proj, prm, layer, nseq, t):
    return pl.pallas_call(
        functools.partial(_attn_ctx_kernel, t=t, layer=layer),
        out_shape=(jax.ShapeDtypeStruct((nseq * t, C_QHEADS * C_HD), BF16),
                   jax.ShapeDtypeStruct((nseq, t, LANES), F32), jax.ShapeDtypeStruct((nseq, t, LANES), F32)),
        grid=(nseq,),
        in_specs=[pl.BlockSpec((t, C_QHEADS * C_HD), lambda b: (b, COL_QC // (C_QHEADS * C_HD))),
                  pl.BlockSpec((t, LANES), lambda b: (b, COL_KC // LANES)),
                  pl.BlockSpec((t, LANES), lambda b: (b, COL_VC // LANES)),
                  _layer_block((1, LANES), layer), _layer_block((1, LANES), layer),
                  pl.BlockSpec(memory_space=pltpu.SMEM)],
        out_specs=(pl.BlockSpec((t, C_QHEADS * C_HD), lambda b: (b, 0)),
                   pl.BlockSpec((1, t, LANES), lambda b: (b, 0, 0)), pl.BlockSpec((1, t, LANES), lambda b: (b, 0, 0))),
        compiler_params=_cparams(("arbitrary",)),
        name="attn_ctx",
    )(proj, proj, proj, prm["q_norm2"], prm["k_norm2"], prm["sink"])


def _attn_lat_kernel(q_ref, k_ref, v_ref, kc_ref, vc_ref, qn_ref, kn_ref, cos_ref, sin_ref, sink_ref, o_ref,
                     qs_scr, ks_scr, vs_scr, bias_scr, *, t, past_len, layer):
    qw, kw = qn_ref[...], kn_ref[...]
    scale = C_HD ** -0.5 * LOG2E
    nrb = t // ROW_BLOCK
    blk = C_BLOCK

    ks_scr[0:blk, :] = jnp.zeros((blk, LANES), BF16)
    vs_scr[0:blk, :] = jnp.zeros((blk, LANES), BF16)
    ks_scr[t + blk:t + 2 * blk, :] = jnp.zeros((blk, LANES), BF16)
    vs_scr[t + blk:t + 2 * blk, :] = jnp.zeros((blk, LANES), BF16)

    grp_rows = C_GROUP * blk

    def prep(r, carry):
        r0 = pl.multiple_of(r * ROW_BLOCK, ROW_BLOCK)
        cos2, sin2 = cos_ref[pl.ds(r0, ROW_BLOCK), :], sin_ref[pl.ds(r0, ROW_BLOCK), :]
        kn = _rope_pairs(_rms_head_pairs(k_ref[pl.ds(r0, ROW_BLOCK), :].astype(F32), kw), cos2, sin2)
        ks_scr[pl.ds(r0 + blk, ROW_BLOCK), :] = kn.astype(BF16)
        vs_scr[pl.ds(r0 + blk, ROW_BLOCK), :] = v_ref[pl.ds(r0, ROW_BLOCK), :]
        for pair in range(C_QHEADS // 2):
            qp = _rms_head_pairs(q_ref[pl.ds(r0, ROW_BLOCK), pair * LANES:(pair + 1) * LANES].astype(F32), qw)
            qp = (_rope_pairs(qp, cos2, sin2) * scale).astype(BF16)
            for half in range(2):
                hq = 2 * pair + half
                hk, g = hq // C_GROUP, hq % C_GROUP
                for sub in range(ROW_BLOCK // blk):
                    dst = pl.multiple_of((r * (ROW_BLOCK // blk) + sub) * grp_rows + g * blk, blk)
                    qs_scr[hk, pl.ds(dst, blk), :] = qp[sub * blk:(sub + 1) * blk, half * C_HD:(half + 1) * C_HD]
        return carry

    lax.fori_loop(0, nrb, prep, 0)

    nkeys = 3 * blk + past_len
    kcb = kc_ref[0, 0].astype(BF16)
    vcb = vc_ref[0, 0].astype(BF16)
    qi = lax.broadcasted_iota(jnp.int32, (grp_rows, nkeys), 0) & (blk - 1)
    kj = lax.broadcasted_iota(jnp.int32, (grp_rows, nkeys), 1)
    visible = jnp.logical_or(kj >= 3 * blk, jnp.logical_and(kj >= qi, kj <= qi + 2 * C_WINDOW))
    bias_scr[...] = jnp.where(visible, 0.0, NEG_BIG)
    head_of_row = lax.broadcasted_iota(jnp.int32, (grp_rows, 1), 0) // blk
    kcol = lax.broadcasted_iota(jnp.int32, (1, nkeys), 1)

    def qblock(n, carry):
        r0 = pl.multiple_of(n * blk, blk)
        kpos = kcol + (r0 - blk)
        in_seq = jnp.logical_or(kcol >= 3 * blk, jnp.logical_and(kpos >= 0, kpos < t))
        edge = jnp.where(in_seq, 0.0, NEG_BIG)
        kwin = ks_scr[pl.ds(r0, 3 * blk), :]
        vwin = vs_scr[pl.ds(r0, 3 * blk), :]
        scores, vals, sinks = [], [], []
        for hk in range(C_KVHEADS):
            lo, hi = hk * C_HD, (hk + 1) * C_HD
            keys = jnp.concatenate([kwin[:, lo:hi], kcb[:, lo:hi]], axis=0)
            vals.append(jnp.concatenate([vwin[:, lo:hi], vcb[:, lo:hi]], axis=0))
            q_stack = qs_scr[hk, pl.ds(pl.multiple_of(n * grp_rows, grp_rows), grp_rows), :]
            scores.append(_dot_nt(q_stack, keys))
            sink = jnp.full((grp_rows, 1), sink_ref[layer, hk * C_GROUP], F32)
            for g in range(1, C_GROUP):
                sink = jnp.where(head_of_row == g, sink_ref[layer, hk * C_GROUP + g], sink)
            sinks.append(sink * LOG2E)
        outs = []
        for hk in range(C_KVHEADS):
            o = _softmax_sink_av([scores[hk] + bias_scr[...] + edge], [vals[hk]], sinks[hk])
            outs += [o[g * blk:(g + 1) * blk] for g in range(C_GROUP)]
        for pair in range(C_QHEADS // 2):
            o_ref[pl.ds(r0, blk), pair * LANES:(pair + 1) * LANES] = jnp.concatenate(
                outs[2 * pair:2 * pair + 2], axis=1).astype(o_ref.dtype)
        return carry

    lax.fori_loop(0, t // blk, qblock, 0)


def _attn_lat_call(proj, cache_k, cache_v, prm, cos2, sin2, layer, nseq, t):
    past_len = cache_k.shape[2]
    qw = C_QHEADS * C_HD
    return pl.pallas_call(
        functools.partial(_attn_lat_kernel, t=t, past_len=past_len, layer=layer),
        out_shape=jax.ShapeDtypeStruct((nseq * t, qw), BF16),
        grid=(nseq,),
        in_specs=[pl.BlockSpec((t, qw), lambda b: (b, COL_QC // qw)),
                  pl.BlockSpec((t, LANES), lambda b: (b, COL_KC // LANES)),
                  pl.BlockSpec((t, LANES), lambda b: (b, COL_VC // LANES)),
                  pl.BlockSpec((1, 1, past_len, LANES), lambda b: (b, layer, 0, 0)),
                  pl.BlockSpec((1, 1, past_len, LANES), lambda b: (b, layer, 0, 0)),
                  _layer_block((1, LANES), layer), _layer_block((1, LANES), layer),
                  _resident((t, LANES), lambda b: (0, 0)), _resident((t, LANES), lambda b: (0, 0)),
                  pl.BlockSpec(memory_space=pltpu.SMEM)],
        out_specs=pl.BlockSpec((t, qw), lambda b: (b, 0)),
        scratch_shapes=[pltpu.VMEM((C_KVHEADS, t * C_GROUP, C_HD), BF16),
                        pltpu.VMEM((t + 2 * C_BLOCK, LANES), BF16), pltpu.VMEM((t + 2 * C_BLOCK, LANES), BF16),
                        pltpu.VMEM((C_GROUP * C_BLOCK, 3 * C_BLOCK + past_len), F32)],
        compiler_params=_cparams(("arbitrary",)),
        name="attn_lat",
    )(proj, proj, proj, cache_k, cache_v, prm["q_norm2"], prm["k_norm2"], cos2, sin2, prm["sink"])


def _rope_tables(t):
    rows = t // GRID_W
    row = jnp.repeat(jnp.arange(rows, dtype=F32), GRID_W)
    col = jnp.tile(jnp.arange(GRID_W, dtype=F32), rows)
    nf = C_HD // 4
    inv = ROPE_THETA ** (-jnp.arange(nf, dtype=F32) / nf)
    ar, ac = row[:, None] * inv, col[:, None] * inv
    cos = jnp.concatenate([jnp.cos(ar), jnp.cos(ar), jnp.cos(ac), jnp.cos(ac)], axis=1)
    sin = jnp.concatenate([-jnp.sin(ar), jnp.sin(ar), -jnp.sin(ac), jnp.sin(ac)], axis=1)
    return jnp.tile(cos, (1, 2)), jnp.tile(sin, (1, 2))


def _merge_kernel(x_ref, g0_ref, g1_ref, g2_ref, oa_ref, ob_ref, oc_ref, mod_ref, wbr_ref, wout_ref, o_ref):
    merged = (_sigmoid(g0_ref[...].astype(F32)) * jnp.dot(oa_ref[...], wbr_ref[0], preferred_element_type=F32)
              + _sigmoid(g1_ref[...].astype(F32)) * jnp.dot(ob_ref[...], wbr_ref[1], preferred_element_type=F32)
              + _sigmoid(g2_ref[...].astype(F32)) * jnp.dot(oc_ref[...], wbr_ref[2], preferred_element_type=F32))
    res = jnp.dot(merged.astype(BF16), wout_ref[...], preferred_element_type=F32)
    o_ref[...] = x_ref[...] + mod_ref[0, 2:3, :] * res


def _merge_call(x2d, proj, oa, ob, oc, prm, layer, cond, tm):
    m, d = x2d.shape
    mg = lambda r: pl.BlockSpec((tm, d), lambda i: (i, COL_MG // d + r))
    br = pl.BlockSpec((tm, BRANCH_W), lambda i: (i, 0))
    return pl.pallas_call(
        _merge_kernel,
        out_shape=jax.ShapeDtypeStruct((m, d), F32),
        grid=(m // tm,),
        in_specs=[pl.BlockSpec((tm, d), lambda i: (i, 0)), mg(0), mg(1), mg(2), br, br, br,
                  _mod_block(d, layer, cond), _layer_block((3, BRANCH_W, d), layer), _layer_block((d, d), layer)],
        out_specs=pl.BlockSpec((tm, d), lambda i: (i, 0)),
        compiler_params=_cparams(("arbitrary",)),
        name="merge",
    )(x2d, proj, proj, proj, oa, ob, oc, prm["mod"], prm["w_branch"], prm["w_out"])


FF_CHUNK = 256
HALO = BF16_ROWS


def _ffn_kernel(x_ref, xp_ref, xn_ref, mod_ref, nw_ref, wup_ref, cw_ref, wd_ref, o_ref, h_scr, act_scr, *,
                tm, seq_len):
    i = pl.program_id(0)
    nseg = max(1, tm // seq_len)
    seg = tm // nseg
    nw, sh, sc = nw_ref[...], mod_ref[0, 3:4, :], mod_ref[0, 4:5, :]

    def norm(x):
        return _rms_rows(x, nw) * (1.0 + sc) + sh

    has_prev = ((i * tm) & (seq_len - 1)) != 0
    has_next = (((i + 1) * tm) & (seq_len - 1)) != 0
    zero_halo = jnp.zeros((HALO, x_ref.shape[1]), BF16)
    for s in range(nseg):
        h_scr[s, HALO:HALO + seg, :] = norm(x_ref[s * seg:(s + 1) * seg, :]).astype(BF16)
        if s == 0:
            h_scr[s, 0:HALO, :] = (norm(xp_ref[...]) * jnp.where(has_prev, 1.0, 0.0)).astype(BF16)
        else:
            h_scr[s, 0:HALO, :] = zero_halo
        if s == nseg - 1:
            h_scr[s, HALO + seg:2 * HALO + seg, :] = (norm(xn_ref[...]) * jnp.where(has_next, 1.0, 0.0)).astype(BF16)
        else:
            h_scr[s, HALO + seg:2 * HALO + seg, :] = zero_halo

    def conv(u, cw):
        return (u[HALO - 1:HALO - 1 + seg] * cw[0:1, :] + u[HALO:HALO + seg] * cw[1:2, :]
                + u[HALO + 1:HALO + 1 + seg] * cw[2:3, :])

    for lo, hi in _col_chunks(D_FF, FF_CHUNK):
        for s in range(nseg):
            h = h_scr[s]
            a = conv(jnp.dot(h, wup_ref[:, lo:hi], preferred_element_type=F32), cw_ref[:, lo:hi])
            u = conv(jnp.dot(h, wup_ref[:, D_FF + lo:D_FF + hi], preferred_element_type=F32),
                     cw_ref[:, D_FF + lo:D_FF + hi])
            act_scr[s * seg:(s + 1) * seg, lo:hi] = (_silu(a) * u).astype(BF16)

    o_ref[...] = x_ref[...] + mod_ref[0, 5:6, :] * jnp.dot(act_scr[...], wd_ref[...], preferred_element_type=F32)


def _ffn_call(x2d, prm, layer, cond, tm, seq_len):
    m, d = x2d.shape
    hb = tm // HALO
    last = m // HALO - 1
    nseg = max(1, tm // seq_len)
    seg = tm // nseg
    return pl.pallas_call(
        functools.partial(_ffn_kernel, tm=tm, seq_len=seq_len),
        out_shape=jax.ShapeDtypeStruct((m, d), F32),
        grid=(m // tm,),
        in_specs=[pl.BlockSpec((tm, d), lambda i: (i, 0)),
                  pl.BlockSpec((HALO, d), lambda i: (jnp.maximum(i * hb - 1, 0), 0)),
                  pl.BlockSpec((HALO, d), lambda i: (jnp.minimum((i + 1) * hb, last), 0)),
                  _mod_block(d, layer, cond), _layer_block((1, d), layer), _layer_block((d, 2 * D_FF), layer),
                  _layer_block((3, 2 * D_FF), layer), _layer_block((D_FF, d), layer)],
        out_specs=pl.BlockSpec((tm, d), lambda i: (i, 0)),
        scratch_shapes=[pltpu.VMEM((nseg, seg + 2 * HALO, d), BF16), pltpu.VMEM((tm, D_FF), BF16)],
        compiler_params=_cparams(("arbitrary",)),
        name="ffn",
    )(x2d, x2d, x2d, prm["mod"], prm["norm2_w"], prm["w_up"], prm["conv_ffn"], prm["w_down"])


def _permute_w_in(w):
    s = _SRC
    w16 = jnp.concatenate([w[..., s["mg"]:s["end"]], w[..., s["qa"]:s["beta"]], w[..., s["qb"]:s["fb"]],
                           w[..., s["gb"]:s["mg"]]], axis=-1)
    n_gate = s["qb"] - s["beta"]
    gates = jnp.pad(w[..., s["beta"]:s["qb"]], ((0, 0), (0, 0), (0, LANES - n_gate)))
    w32 = jnp.concatenate([w[..., s["fb"]:s["gb"]], gates], axis=-1)
    return w16.astype(BF16), w32.astype(BF16)


def _gate_rows(proj32, nseq, t, c):
    g = proj32[:, COL32_GATES:COL32_GATES + 4 * HEADS].reshape(nseq, t // c, c, 2, 2, HEADS)
    return jnp.transpose(g, (0, 5, 1, 3, 4, 2)).reshape(nseq, HEADS, t // c, 4, c)


def _row_tile(rows, t):
    tm = 512
    while rows % tm or (t % tm and tm % t):
        tm //= 2
    return tm


def _group_forward(x3d, first_cond, shared_cond, prm, past, tables):
    nseq, t, d = x3d.shape
    x = x3d.reshape(nseq * t, d)
    tm = _row_tile(nseq * t, t)
    tiles_per_seq = max(1, t // tm)
    cond = (lambda i: first_cond) if shared_cond else (lambda i: first_cond + i // tiles_per_seq)
    emit = past is None
    states_a, states_b, keys, vals = [], [], [], []
    for l in range(prm["w16"].shape[0]):
        proj, proj32 = _in_proj_call(x, prm, l, cond, tm)
        gates_r = _gate_rows(proj32, nseq, t, min(CHUNK_A, t))
        res_ab = _mixer_ab_call(proj, proj32, gates_r, prm, past, l, nseq, t, emit)
        if emit:
            oc, kn, vn = _attn_ctx_call(proj, prm, l, nseq, t)
            states_a.append(res_ab[2])
            states_b.append(res_ab[3])
            keys.append(kn.reshape(nseq, t, C_KVHEADS, C_HD))
            vals.append(vn.reshape(nseq, t, C_KVHEADS, C_HD))
        else:
            oc = _attn_lat_call(proj, past[2], past[3], prm, tables[0], tables[1], l, nseq, t)
        x = _merge_call(x, proj, res_ab[0], res_ab[1], oc, prm, l, cond, tm)
        x = _ffn_call(x, prm, l, cond, tm, t)
    return x.reshape(nseq, t, d), states_a, states_b, keys, vals


def kernel(x_prompt, x_sample, state_delta, state_hgrn, cache_k, cache_v, c, c_ctx, ada_w, ada_b, norm1_w, w_in, conv_a, a_log, dt_bias, norm_a, lb_logits, norm_b, q_norm, k_norm, sink, w_branch, w_out, norm2_w, w_up, conv_ffn, w_down):
    depth = w_in.shape[0]
    d = x_prompt.shape[-1]

    cond = jnp.concatenate([c_ctx[None, :], c], axis=0)
    rows = -(-cond.shape[0] // SUBLANES) * SUBLANES
    cond = jnp.pad(cond, ((0, rows - cond.shape[0]), (0, 0)))
    mod_all = _mod_call(cond, ada_w, ada_b).reshape(depth, rows, 6, d)

    w16, w32 = _permute_w_in(w_in)
    prm = dict(
        mod=mod_all, w16=w16, w32=w32,
        norm1_w=norm1_w.reshape(depth, 1, d), norm2_w=norm2_w.reshape(depth, 1, d),
        conv_a=conv_a, a_log=a_log, dt_bias=dt_bias, norm_a=norm_a.reshape(depth, 1, HEAD_W),
        lb_logits=lb_logits, norm_b=norm_b.reshape(depth, 1, HEAD_W),
        q_norm2=jnp.tile(q_norm, (1, 2)).reshape(depth, 1, LANES), k_norm2=jnp.tile(k_norm, (1, 2)).reshape(depth, 1, LANES),
        sink=sink, w_branch=w_branch.astype(BF16), w_out=w_out.astype(BF16),
        w_up=w_up.astype(BF16), conv_ffn=conv_ffn, w_down=w_down.astype(BF16))

    y_prompt, st_a, st_b, keys, vals = _group_forward(x_prompt, 0, True, prm, None, None)

    past_len = cache_k.shape[2]
    past = (state_delta, state_hgrn,
            cache_k.reshape(cache_k.shape[0], depth, past_len, C_KVHEADS * C_HD),
            cache_v.reshape(cache_v.shape[0], depth, past_len, C_KVHEADS * C_HD))
    y_sample, _, _, _, _ = _group_forward(x_sample, 1, False, prm, past, _rope_tables(x_sample.shape[1]))

    return (y_prompt, y_sample, jnp.stack(st_a, axis=1), jnp.stack(st_b, axis=1),
            jnp.stack(keys, axis=1), jnp.stack(vals, axis=1))
```

```python
import functools
import math

import numpy as np
import jax
import jax.numpy as jnp
from jax import lax
from jax.experimental import pallas as pl
from jax.experimental.pallas import tpu as pltpu

F32 = jnp.float32
BF16 = jnp.bfloat16

D_MODEL = 1024
NORM_EPS = 1e-6
LB_EPS = 1e-6
NEG_BIG = -1e30
LOG2E = 1.4426950408889634
GRID_W = 64
ROPE_THETA = 10000.0

HEADS = 4
HEAD_W = 128
SHORT_CONV = 5
C_QHEADS = 8
C_KVHEADS = 2
C_GROUP = C_QHEADS // C_KVHEADS
C_HD = 64
C_WINDOW = 128
C_BLOCK = 128
BRANCH_W = 512
D_FF = 2816

LANES = 128
SUBLANES = 8
BF16_ROWS = 16
VMEM_LIMIT = 56 * 1024 * 1024

COL_MG = 0
COL_QA = 3072
COL_KA = 3584
COL_VA = 4096
COL_GA = 4608
COL_QB = 5120
COL_IB = 5632
COL_GB = 6144
COL_QC = 6656
COL_KC = 7168
COL_VC = 7296
PROJ_W = 7424
COL32_FB = 0
COL32_GATES = 1024
PROJ32_W = 1152
_SRC = dict(qa=0, ka=512, va=1024, ga=1536, beta=2048, alpha=2056, qb=2064, ib=2576, fb=3088, gb=4112,
            qc=4624, kc=5136, vc=5264, mg=5392, end=8464)

CHUNK_A = 128
ROUNDS_A = 8
CHUNK_B = 64
TRI_BASE = 16
TRI_MERGE = 2
MM_TILE = 768


def _cparams(sem):
    return pltpu.CompilerParams(dimension_semantics=sem, vmem_limit_bytes=VMEM_LIMIT)


def _resident(shape, index_map):
    return pl.BlockSpec(shape, index_map, pipeline_mode=pl.Buffered(1))


def _layer_block(shape, layer):
    return _resident((None,) + tuple(shape), lambda *_: (layer,) + (0,) * len(shape))


def _mod_block(d, layer, cond):
    return pl.BlockSpec((None, 1, 6, d), lambda i: (layer, cond(i), 0, 0))


def _dot(a, b):
    return jnp.dot(a.astype(BF16), b.astype(BF16), preferred_element_type=F32)


def _dot_nt(a, b):
    return lax.dot_general(a.astype(BF16), b.astype(BF16), (((1,), (1,)), ((), ())), preferred_element_type=F32)


def _dot_tn(a, b):
    return lax.dot_general(a.astype(BF16), b.astype(BF16), (((0,), (0,)), ((), ())), preferred_element_type=F32)


def _sigmoid(x):
    return 1.0 / (1.0 + jnp.exp(-x))


def _silu(x):
    return x * _sigmoid(x)


def _softplus(x):
    return jnp.maximum(x, 0.0) + jnp.log(1.0 + jnp.exp(-jnp.abs(x)))


def _rms_rows(x, w):
    ms = jnp.mean(x * x, axis=-1, keepdims=True)
    return x * lax.rsqrt(ms + NORM_EPS) * w


def _split_bf16(x, n):
    parts, r = [], x
    for _ in range(n):
        p = r.astype(BF16)
        parts.append(p)
        r = r - p.astype(F32)
    return parts


def _col_chunks(width, step):
    return [(lo, min(lo + step, width)) for lo in range(0, width, step)]


def _mod_kernel(c_ref, w_ref, b_ref, o_ref):
    c = c_ref[...]
    o_ref[0] = _dot(_silu(c), w_ref[0]) + b_ref[0]


def _mod_call(cond, ada_w, ada_b):
    depth, d, n = ada_w.shape
    rows = cond.shape[0]
    tn = 768
    return pl.pallas_call(
        _mod_kernel,
        out_shape=jax.ShapeDtypeStruct((depth, rows, n), F32),
        grid=(depth, n // tn),
        in_specs=[pl.BlockSpec((rows, d), lambda l, j: (0, 0)),
                  pl.BlockSpec((1, d, tn), lambda l, j: (l, 0, j)),
                  pl.BlockSpec((1, 1, tn), lambda l, j: (l, 0, j))],
        out_specs=pl.BlockSpec((1, rows, tn), lambda l, j: (l, 0, j)),
        compiler_params=_cparams(("arbitrary", "arbitrary")),
        name="mod",
    )(cond, ada_w, ada_b.reshape(depth, 1, n))


def _in_proj_kernel(x_ref, mod_ref, nw_ref, w16_ref, w32_ref, o16_ref, o32_ref):
    h = _rms_rows(x_ref[...], nw_ref[...]) * (1.0 + mod_ref[0, 1:2, :]) + mod_ref[0, 0:1, :]
    hb = h.astype(BF16)
    for lo, hi in _col_chunks(PROJ_W, MM_TILE):
        o16_ref[:, lo:hi] = jnp.dot(hb, w16_ref[:, lo:hi], preferred_element_type=F32).astype(BF16)
    for lo, hi in _col_chunks(PROJ32_W, MM_TILE):
        o32_ref[:, lo:hi] = jnp.dot(hb, w32_ref[:, lo:hi], preferred_element_type=F32)


def _in_proj_call(x2d, prm, layer, cond, tm):
    m, d = x2d.shape
    return pl.pallas_call(
        _in_proj_kernel,
        out_shape=(jax.ShapeDtypeStruct((m, PROJ_W), BF16), jax.ShapeDtypeStruct((m, PROJ32_W), F32)),
        grid=(m // tm,),
        in_specs=[pl.BlockSpec((tm, d), lambda i: (i, 0)), _mod_block(d, layer, cond),
                  _layer_block((1, d), layer), _layer_block((d, PROJ_W), layer), _layer_block((d, PROJ32_W), layer)],
        out_specs=(pl.BlockSpec((tm, PROJ_W), lambda i: (i, 0)),
                   pl.BlockSpec((tm, PROJ32_W), lambda i: (i, 0))),
        compiler_params=_cparams(("arbitrary",)),
        name="in_proj",
    )(x2d, prm["mod"], prm["norm1_w"], prm["w16"], prm["w32"])


ROW_BLOCK = 256


def _gated_norm_epilogue(of_scr, ob_scr, gate_ref, nw_ref, o_ref, t, cols):
    nw = nw_ref[...]

    def body(r, carry):
        r0 = pl.multiple_of(r * ROW_BLOCK, ROW_BLOCK)
        o = of_scr[pl.ds(r0, ROW_BLOCK), :] + ob_scr[pl.ds(r0, ROW_BLOCK), :]
        y = _rms_rows(o, nw) * _silu(gate_ref[pl.ds(r0, ROW_BLOCK), cols].astype(F32))
        o_ref[pl.ds(r0, ROW_BLOCK), cols] = y.astype(o_ref.dtype)
        return carry

    lax.fori_loop(0, t // ROW_BLOCK, body, 0)


def _tri_inverse(mats, ri, ci, c, tick):
    n = range(len(mats))
    shift = int(np.log2(TRI_BASE))
    base = (ri >> shift) == (ci >> shift)
    mb = [jnp.where(base, m, 0.0) for m in mats]
    y = [-m for m in mb]
    p = [_dot(m, m) for m in mb]
    tick()
    for _ in range(shift - 2):
        yp = [_dot(jnp.concatenate([y[i], p[i]], axis=0), p[i]) for i in n]
        tick()
        y = [y[i] + p[i] + yp[i][:c] for i in n]
        p = [r[c:] for r in yp]
    yp = [_dot(y[i], p[i]) for i in n]
    tick()
    y = [y[i] + p[i] + yp[i] for i in n]
    total = int(np.log2(c))
    while shift < total:
        factors = min(TRI_MERGE, total - shift)
        inner = (ri >> shift) == (ci >> shift)
        outer = (ri >> (shift + factors)) == (ci >> (shift + factors))
        between = jnp.logical_and(outer, jnp.logical_not(inner))
        cm = [jnp.where(between, m, 0.0) for m in mats]
        w = [_dot(y[i], cm[i]) for i in n]
        tick()
        w = [cm[i] + w[i] for i in n]
        if factors == 1:
            wy = [_dot(w[i], y[i]) for i in n]
            tick()
            y = [y[i] - w[i] - wy[i] for i in n]
        else:
            r = [_dot(w[i], jnp.concatenate([w[i], y[i]], axis=1)) for i in n]
            tick()
            p, y = [x[:, :c] for x in r], [y[i] - w[i] - r[i][:, c:] for i in n]
            for f in range(1, factors):
                if f + 1 < factors:
                    r = [_dot(p[i], jnp.concatenate([p[i], y[i]], axis=1)) for i in n]
                    tick()
                    p, y = [x[:, :c] for x in r], [y[i] + p[i] + r[i][:, c:] for i in n]
                else:
                    py = [_dot(p[i], y[i]) for i in n]
                    tick()
                    y = [y[i] + p[i] + py[i] for i in n]
        shift += factors
    eye = jnp.where(ri == ci, 1.0, 0.0)
    return [eye + v for v in y]


def _delta_chunks(probs, states, c, tick):
    n = range(len(probs))
    ri = lax.broadcasted_iota(jnp.int32, (c, c), 0)
    ci = lax.broadcasted_iota(jnp.int32, (c, c), 1)
    eye = ri == ci
    pre = []
    for q, k, v, beta_row, alpha_row, a_neg, dt_b, chain, reverse in probs:
        incl, strict = (ri <= ci, ri < ci) if reverse else (ri >= ci, ri > ci)
        beta_r = _sigmoid(beta_row)
        g_r = a_neg * _softplus(alpha_row + dt_b)
        gc_col = jnp.sum(jnp.where(incl, jnp.broadcast_to(g_r, (c, c)), 0.0), axis=1, keepdims=True)
        beta_col = jnp.sum(jnp.where(eye, jnp.broadcast_to(beta_r, (c, c)), 0.0), axis=1, keepdims=True)
        gc_row = jnp.sum(jnp.where(eye, jnp.broadcast_to(gc_col, (c, c)), 0.0), axis=0, keepdims=True)
        g_tot = jnp.sum(g_r, axis=1, keepdims=True)
        decay = jnp.where(incl, jnp.exp(jnp.where(incl, gc_col - gc_row, 0.0)), 0.0)
        pre.append((strict, gc_col, beta_col, g_tot, decay, jnp.exp(gc_col)))

    kb = [pr[1].astype(BF16) for pr in probs]
    kq = [_dot_nt(jnp.concatenate([kb[i], probs[i][0].astype(BF16)], axis=0), kb[i]) for i in n]
    tick()
    t_inv = _tri_inverse([jnp.where(pre[i][0], kq[i][:c] * pre[i][2] * pre[i][4], 0.0) for i in n], ri, ci, c, tick)
    uw = [_dot(t_inv[i], jnp.concatenate([probs[i][2] * pre[i][2], probs[i][1] * (pre[i][2] * pre[i][5])], axis=1))
          for i in n]
    tick()
    wq_lhs = [jnp.concatenate([uw[i][:, HEAD_W:], probs[i][0] * pre[i][5]], axis=0).astype(BF16) for i in n]
    os_lhs = [jnp.concatenate([kq[i][c:] * pre[i][4], (probs[i][1] * jnp.exp(pre[i][3] - pre[i][1])).T],
                              axis=0).astype(BF16) for i in n]
    states = list(states)
    outs = [None] * len(probs)
    todo = list(n)
    while todo:
        front, seen = [], set()
        for i in todo:
            if probs[i][7] not in seen:
                seen.add(probs[i][7])
                front.append(i)
        todo = [i for i in todo if i not in front]
        wq = [_dot(wq_lhs[i], states[probs[i][7]]) for i in front]
        tick()
        v_new = [uw[i][:, :HEAD_W] - wq[j][:c] for j, i in enumerate(front)]
        os_ = [_dot(os_lhs[i], v_new[j]) for j, i in enumerate(front)]
        tick()
        for j, i in enumerate(front):
            outs[i] = wq[j][c:] + os_[j][:c]
            states[probs[i][7]] = states[probs[i][7]] * jnp.exp(pre[i][3]) + os_[j][c:]
    return outs, states


def _conv_silu_pass(x_ref, cw_ref, xp_scr, dst_scr, t, l2, scale, cols):
    nrb = t // ROW_BLOCK
    pad = SUBLANES
    half = SHORT_CONV // 2

    def cp(r, carry):
        r0 = pl.multiple_of(r * ROW_BLOCK, ROW_BLOCK)
        xp_scr[pl.ds(r0 + pad, ROW_BLOCK), :] = x_ref[pl.ds(r0, ROW_BLOCK), cols].astype(F32)
        return carry

    lax.fori_loop(0, nrb, cp, 0)
    cw = cw_ref[:, cols]

    def body(r, carry):
        r0 = pl.multiple_of(r * ROW_BLOCK, ROW_BLOCK)
        y = None
        for j in range(SHORT_CONV):
            term = xp_scr[pl.ds(r0 + (pad - half + j), ROW_BLOCK), :] * cw[j:j + 1, :]
            y = term if y is None else y + term
        y = _silu(y)
        if l2:
            y = y * (lax.rsqrt(jnp.sum(y * y, axis=-1, keepdims=True) + NORM_EPS) * scale)
        dst_scr[pl.ds(r0, ROW_BLOCK), :] = y
        return carry

    lax.fori_loop(0, nrb, body, 0)


def _hgrn_diag(q, kf, v, b, reverse, c):
    nb = c // SUBLANES
    q3, k3, v3, b3 = (a.reshape(nb, SUBLANES, HEAD_W) for a in (q, kf, v, b))
    sub = lax.broadcasted_iota(jnp.int32, (nb, SUBLANES, HEAD_W), 1)
    o3 = jnp.zeros((nb, SUBLANES, HEAD_W), F32)
    for j in range(SUBLANES):
        mask = (sub <= j) if reverse else (sub >= j)
        e = jnp.exp2(jnp.where(mask, b3 - b3[:, j:j + 1, :], NEG_BIG))
        a = jnp.sum(q3 * e * k3[:, j:j + 1, :], axis=-1, keepdims=True)
        o3 = o3 + a * v3[:, j:j + 1, :]
    return o3.reshape(c, HEAD_W)


def _hgrn_chunk(q, kf, v, lf, st, reverse, c):
    ri = lax.broadcasted_iota(jnp.int32, (c, c), 0)
    ci = lax.broadcasted_iota(jnp.int32, (c, c), 1)
    incl = (ri <= ci) if reverse else (ri >= ci)
    parts = jnp.concatenate(_split_bf16(lf, 3), axis=1)
    b3 = jnp.dot(jnp.where(incl, 1.0, 0.0).astype(BF16), parts, preferred_element_type=F32)
    b = b3[:, :HEAD_W] + b3[:, HEAD_W:2 * HEAD_W] + b3[:, 2 * HEAD_W:]
    b_tot = jnp.sum(lf, axis=0, keepdims=True)
    o = _dot_nt(q * jnp.exp2(b), st)
    row = lax.broadcasted_iota(jnp.int32, (c, 1), 0)
    att = jnp.zeros((c, c), F32)
    n = SUBLANES
    while n < c:
        pieces = []
        for g in range(c // (2 * n)):
            r = g * 2 * n + (n if reverse else n - 1)
            pieces.append(jnp.broadcast_to(b[r:r + 1, :], (2 * n, HEAD_W)))
        ref = pieces[0] if len(pieces) == 1 else jnp.concatenate(pieces, axis=0)
        e = jnp.exp2(-jnp.abs(b - ref))
        s = int(np.log2(n))
        second = ((row >> s) & 1) == 1
        q_part = jnp.logical_not(second) if reverse else second
        qt = jnp.where(q_part, q * e, 0.0)
        kt = jnp.where(q_part, 0.0, kf * e)
        att = att + jnp.where((ri >> (s + 1)) == (ci >> (s + 1)), _dot_nt(qt, kt), 0.0)
        n *= 2
    o = o + _dot(att, v) + _hgrn_diag(q, kf, v, b, reverse, c)
    kh = kf * jnp.exp2(b_tot - b)
    st_new = st * jnp.exp2(b_tot) + _dot_tn(v, kh)
    return o, st_new


def _mixer_ab_kernel(*refs, t, ca, cb, layer, hps, has_past, emit_state):
    refs = list(refs)
    (qa_ref, ka_ref, va_ref, ga_ref, gr_ref, cwq_ref, cwk_ref, cwv_ref, alog_ref, dtb_ref, nwa_ref,
     qb_ref, ib_ref, f0_ref, f1_ref, gb_ref, lb_ref, nwb_ref) = refs[:18]
    pos = 18
    sa0_ref = sb0_ref = None
    if has_past:
        sa0_ref, sb0_ref = refs[pos:pos + 2]
        pos += 2
    oa_ref, ob_ref = refs[pos:pos + 2]
    pos += 2
    sfa_ref = sfb_ref = None
    if emit_state:
        sfa_ref, sfb_ref = refs[pos:pos + 2]
        pos += 2
    xp_scr, qn_scr, kn_scr, vn_scr, af_scr, ab_scr, bf_scr, bb_scr, sa_scr, sb_scr = refs[pos:]

    heads = range(hps)
    cols = [slice(hh * HEAD_W, (hh + 1) * HEAD_W) for hh in heads]
    first_head = pl.program_id(1) * hps
    pad = SUBLANES
    xp_scr[0:pad, :] = jnp.zeros((pad, HEAD_W), F32)
    xp_scr[t + pad:t + 2 * pad, :] = jnp.zeros((pad, HEAD_W), F32)
    for hh in heads:
        _conv_silu_pass(qa_ref, cwq_ref, xp_scr, qn_scr.at[hh], t, True, HEAD_W ** -0.5, cols[hh])
        _conv_silu_pass(ka_ref, cwk_ref, xp_scr, kn_scr.at[hh], t, True, 1.0, cols[hh])
        _conv_silu_pass(va_ref, cwv_ref, xp_scr, vn_scr.at[hh], t, False, 1.0, cols[hh])

    if has_past:
        for hh in heads:
            for d in range(2):
                sa_scr[2 * hh + d] = sa0_ref[0, 0, d, hh]
                sb_scr[2 * hh + d] = sb0_ref[0, 0, d, hh].T
    else:
        sa_scr[...] = jnp.zeros(sa_scr.shape, F32)
        sb_scr[...] = jnp.zeros(sb_scr.shape, F32)

    ones = jnp.ones((1, ca), F32)
    a_neg = [[-jnp.exp(ones * alog_ref[layer, d, first_head + hh]) for d in range(2)] for hh in heads]
    dt_b = [[dtb_ref[layer, d, first_head + hh] for d in range(2)] for hh in heads]

    lb_terms = None
    if layer > 0:
        lb_terms = []
        for hh in heads:
            terms = []
            for d in range(2):
                lg = lb_ref[d][:, cols[hh]]
                ex = jnp.exp(lg - jnp.max(lg, axis=0, keepdims=True))
                pr = ex / jnp.sum(ex, axis=0, keepdims=True)
                lb = jnp.clip(jnp.sum(pr[1:layer + 1], axis=0, keepdims=True), LB_EPS, 1.0 - LB_EPS)
                terms.append((jnp.log(lb) * LOG2E, jnp.log1p(-lb) * LOG2E, 1.0 - lb))
            lb_terms.append(terms)

    f_refs = (f0_ref, f1_ref)
    nca, ncb = t // ca, t // cb
    rounds = math.gcd(ROUNDS_A // hps, nca)
    ratio = rounds * ca // cb

    def hgrn_gates(z, hh, d):
        z2 = z * LOG2E
        y = jnp.exp2(-jnp.abs(z2))
        one_y = 1.0 + y
        l2_sig = jnp.minimum(z2, 0.0) - jnp.log(one_y) * LOG2E
        sig_neg = jnp.where(z >= 0.0, y, 1.0) / one_y
        if layer == 0:
            return l2_sig, sig_neg
        l2_lb, l2_1m_lb, one_m_lb = lb_terms[hh][d]
        a2 = l2_1m_lb + l2_sig
        return (jnp.maximum(l2_lb, a2) + jnp.log(1.0 + jnp.exp2(-jnp.abs(l2_lb - a2))) * LOG2E,
                one_m_lb * sig_neg)

    def step(n, carry):
        a_dst, a_probs = [], []
        for rnd in range(rounds):
            for hh in heads:
                for d in range(2):
                    cidx = n * rounds + rnd if d == 0 else nca - 1 - (n * rounds + rnd)
                    r0 = pl.multiple_of(cidx * ca, ca)
                    gr = gr_ref[0, hh, cidx]
                    a_dst.append((af_scr if d == 0 else ab_scr, hh, r0))
                    a_probs.append((qn_scr[hh, pl.ds(r0, ca), :], kn_scr[hh, pl.ds(r0, ca), :],
                                    vn_scr[hh, pl.ds(r0, ca), :], gr[d:d + 1, :], gr[2 + d:3 + d, :],
                                    a_neg[hh][d], dt_b[hh][d], 2 * hh + d, d == 1))
        sa = [sa_scr[chain] for chain in range(2 * hps)]
        b_jobs = []
        for j in range(ratio):
            m = n * ratio + j
            for hh in heads:
                for d in range(2):
                    r0 = pl.multiple_of((m if d == 0 else ncb - 1 - m) * cb, cb)
                    b_jobs.append((hh, d, r0, f_refs[d][pl.ds(r0, cb), cols[hh]], qb_ref[pl.ds(r0, cb), cols[hh]],
                                   ib_ref[pl.ds(r0, cb), cols[hh]]))
        sb = [sb_scr[chain] for chain in range(2 * hps)]

        b_out = []
        pending = list(b_jobs)

        def tick():
            if pending:
                hh, d, _, z, qraw, iraw = pending.pop(0)
                lf, kf = hgrn_gates(z, hh, d)
                o, sb[2 * hh + d] = _hgrn_chunk(_silu(qraw.astype(F32)), kf, iraw.astype(F32), lf, sb[2 * hh + d],
                                                d == 1, cb)
                b_out.append(o)

        a_out, sa = _delta_chunks(a_probs, sa, ca, tick)
        while pending:
            tick()

        for (dst, hh, r0), o in zip(a_dst, a_out):
            dst[hh, pl.ds(r0, ca), :] = o
        for (hh, d, r0, _, _, _), o in zip(b_jobs, b_out):
            (bf_scr if d == 0 else bb_scr)[hh, pl.ds(r0, cb), :] = o
        for chain in range(2 * hps):
            sa_scr[chain] = sa[chain]
            sb_scr[chain] = sb[chain]
        return carry

    lax.fori_loop(0, nca // rounds, step, 0)

    for hh in heads:
        _gated_norm_epilogue(af_scr.at[hh], ab_scr.at[hh], ga_ref, nwa_ref, oa_ref, t, cols[hh])
        _gated_norm_epilogue(bf_scr.at[hh], bb_scr.at[hh], gb_ref, nwb_ref, ob_ref, t, cols[hh])
    if emit_state:
        for hh in heads:
            for d in range(2):
                sfa_ref[0, d, hh] = sa_scr[2 * hh + d]
                sfb_ref[0, d, hh] = sb_scr[2 * hh + d].T


def _mixer_ab_call(proj, proj32, gates_r, prm, past, layer, nseq, t, emit_state):
    ca, cb = min(CHUNK_A, t), min(CHUNK_B, t)
    depth = prm["lb_logits"].shape[1]
    per_head = t * HEAD_W * (7 * 4 + 2 * (7 * 2 + 2 * 4) + 2 * 2 * 2)
    hps = max(n for n in (1, 2, 4) if n == 1 or n * per_head <= VMEM_LIMIT // 2)
    slab = hps * HEAD_W
    col = lambda off: (lambda b, h: (b, off // slab + h))
    seq_in = lambda off: pl.BlockSpec((t, slab), col(off))
    conv = lambda part: pl.BlockSpec((None, SHORT_CONV, slab), lambda b, h: (layer, 0, part * (HEADS // hps) + h))
    smem = pl.BlockSpec(memory_space=pltpu.SMEM)
    norm = _layer_block((1, HEAD_W), layer)
    in_specs = [seq_in(COL_QA), seq_in(COL_KA), seq_in(COL_VA), seq_in(COL_GA),
                pl.BlockSpec((1, hps, t // ca, 4, ca), lambda b, h: (b, h, 0, 0, 0)),
                conv(0), conv(1), conv(2), smem, smem, norm,
                seq_in(COL_QB), seq_in(COL_IB), seq_in(COL32_FB), seq_in(COL32_FB + HEADS * HEAD_W), seq_in(COL_GB),
                pl.BlockSpec((2, depth, slab), lambda b, h: (0, 0, h)), norm]
    args = [proj, proj, proj, proj, gates_r, prm["conv_a"], prm["conv_a"], prm["conv_a"],
            prm["a_log"], prm["dt_bias"], prm["norm_a"],
            proj, proj, proj32, proj32, proj, prm["lb_logits"], prm["norm_b"]]
    state_in = pl.BlockSpec((1, 1, 2, hps, HEAD_W, HEAD_W), lambda b, h: (b, layer, 0, h, 0, 0))
    if past is not None:
        in_specs += [state_in, state_in]
        args += [past[0], past[1]]
    o_shape = jax.ShapeDtypeStruct((nseq * t, HEADS * HEAD_W), BF16)
    o_spec = pl.BlockSpec((t, slab), lambda b, h: (b, h))
    out_shape, out_specs = [o_shape, o_shape], [o_spec, o_spec]
    if emit_state:
        s_shape = jax.ShapeDtypeStruct((nseq, 2, HEADS, HEAD_W, HEAD_W), F32)
        s_spec = pl.BlockSpec((1, 2, hps, HEAD_W, HEAD_W), lambda b, h: (b, 0, h, 0, 0))
        out_shape += [s_shape, s_shape]
        out_specs += [s_spec, s_spec]
    seq = pltpu.VMEM((hps, t, HEAD_W), F32)
    state = pltpu.VMEM((2 * hps, HEAD_W, HEAD_W), F32)
    return pl.pallas_call(
        functools.partial(_mixer_ab_kernel, t=t, ca=ca, cb=cb, layer=layer, hps=hps, has_past=past is not None,
                          emit_state=emit_state),
        out_shape=tuple(out_shape), grid=(nseq, HEADS // hps), in_specs=in_specs, out_specs=tuple(out_specs),
        scratch_shapes=[pltpu.VMEM((t + 2 * SUBLANES, HEAD_W), F32), seq, seq, seq, seq, seq, seq, seq, state, state],
        compiler_params=_cparams(("arbitrary", "arbitrary")),
        name="mixer_ab",
    )(*args)


def _rms_head_pairs(x, w2):
    lane = lax.broadcasted_iota(jnp.int32, x.shape, 1)
    left = lane < C_HD
    sq = x * x
    s0 = jnp.sum(jnp.where(left, sq, 0.0), axis=-1, keepdims=True)
    s1 = jnp.sum(jnp.where(left, 0.0, sq), axis=-1, keepdims=True)
    ms = jnp.where(left, s0, s1) * (1.0 / C_HD)
    return x * lax.rsqrt(ms + NORM_EPS) * w2


def _rope_pairs(x, cos2, sin2):
    lane = lax.broadcasted_iota(jnp.int32, x.shape, 1)
    quarter = C_HD // 4
    swapped = jnp.where((lane & (2 * quarter - 1)) < quarter,
                        pltpu.roll(x, LANES - quarter, axis=1), pltpu.roll(x, quarter, axis=1))
    return x * cos2 + swapped * sin2


def _softmax_sink_av(scores, values, sink):
    m = sink
    for s in scores:
        m = jnp.maximum(m, jnp.max(s, axis=-1, keepdims=True))
    den = jnp.exp2(sink - m)
    acc = None
    for s, v in zip(scores, values):
        p = jnp.exp2(s - m)
        den = den + jnp.sum(p, axis=-1, keepdims=True)
        t = jnp.dot(p.astype(BF16), v, preferred_element_type=F32)
        acc = t if acc is None else acc + t
    return acc / den


def _attn_ctx_kernel(q_ref, k_ref, v_ref, qn_ref, kn_ref, sink_ref, o_ref, ko_ref, vo_ref, *, t, layer):
    qw, kw = qn_ref[...], kn_ref[...]
    kn = _rms_head_pairs(k_ref[...].astype(F32), kw)
    ko_ref[0] = kn
    v = v_ref[...]
    vo_ref[0] = v.astype(F32)
    knb, vb = kn.astype(BF16), v
    scale = C_HD ** -0.5 * LOG2E
    for pair in range(C_QHEADS // 2):
        qp = (_rms_head_pairs(q_ref[:, pair * LANES:(pair + 1) * LANES].astype(F32), qw) * scale).astype(BF16)
        outs = []
        for half in range(2):
            hq = 2 * pair + half
            hk = hq // C_GROUP
            qh = qp[:, half * C_HD:(half + 1) * C_HD]
            s = _dot_nt(qh, knb[:, hk * C_HD:(hk + 1) * C_HD])
            sink = jnp.full((1, 1), sink_ref[layer, hq] * LOG2E, F32)
            outs.append(_softmax_sink_av([s], [vb[:, hk * C_HD:(hk + 1) * C_HD]], sink))
        o_ref[:, pair * LANES:(pair + 1) * LANES] = jnp.concatenate(outs, axis=1).astype(o_ref.dtype)


def _attn_ctx_call(proj, prm, layer, nseq, t):
    return pl.pallas_call(
        functools.partial(_attn_ctx_kernel, t=t, layer=layer),
        out_shape=(jax.ShapeDtypeStruct((nseq * t, C_QHEADS * C_HD), BF16),
                   jax.ShapeDtypeStruct((nseq, t, LANES), F32), jax.ShapeDtypeStruct((nseq, t, LANES), F32)),
        grid=(nseq,),
        in_specs=[pl.BlockSpec((t, C_QHEADS * C_HD), lambda b: (b, COL_QC // (C_QHEADS * C_HD))),
                  pl.BlockSpec((t, LANES), lambda b: (b, COL_KC // LANES)),
                  pl.BlockSpec((t, LANES), lambda b: (b, COL_VC // LANES)),
                  _layer_block((1, LANES), layer), _layer_block((1, LANES), layer),
                  pl.BlockSpec(memory_space=pltpu.SMEM)],
        out_specs=(pl.BlockSpec((t, C_QHEADS * C_HD), lambda b: (b, 0)),
                   pl.BlockSpec((1, t, LANES), lambda b: (b, 0, 0)), pl.BlockSpec((1, t, LANES), lambda b: (b, 0, 0))),
        compiler_params=_cparams(("arbitrary",)),
        name="attn_ctx",
    )(proj, proj, proj, prm["q_norm2"], prm["k_norm2"], prm["sink"])


def _attn_lat_kernel(q_ref, k_ref, v_ref, kc_ref, vc_ref, qn_ref, kn_ref, cos_ref, sin_ref, sink_ref, o_ref,
                     qs_scr, ks_scr, vs_scr, bias_scr, *, t, past_len, layer):
    qw, kw = qn_ref[...], kn_ref[...]
    scale = C_HD ** -0.5 * LOG2E
    nrb = t // ROW_BLOCK
    blk = C_BLOCK

    ks_scr[0:blk, :] = jnp.zeros((blk, LANES), BF16)
    vs_scr[0:blk, :] = jnp.zeros((blk, LANES), BF16)
    ks_scr[t + blk:t + 2 * blk, :] = jnp.zeros((blk, LANES), BF16)
    vs_scr[t + blk:t + 2 * blk, :] = jnp.zeros((blk, LANES), BF16)

    grp_rows = C_GROUP * blk

    def prep(r, carry):
        r0 = pl.multiple_of(r * ROW_BLOCK, ROW_BLOCK)
        cos2, sin2 = cos_ref[pl.ds(r0, ROW_BLOCK), :], sin_ref[pl.ds(r0, ROW_BLOCK), :]
        kn = _rope_pairs(_rms_head_pairs(k_ref[pl.ds(r0, ROW_BLOCK), :].astype(F32), kw), cos2, sin2)
        ks_scr[pl.ds(r0 + blk, ROW_BLOCK), :] = kn.astype(BF16)
        vs_scr[pl.ds(r0 + blk, ROW_BLOCK), :] = v_ref[pl.ds(r0, ROW_BLOCK), :]
        for pair in range(C_QHEADS // 2):
            qp = _rms_head_pairs(q_ref[pl.ds(r0, ROW_BLOCK), pair * LANES:(pair + 1) * LANES].astype(F32), qw)
            qp = (_rope_pairs(qp, cos2, sin2) * scale).astype(BF16)
            for half in range(2):
                hq = 2 * pair + half
                hk, g = hq // C_GROUP, hq % C_GROUP
                for sub in range(ROW_BLOCK // blk):
                    dst = pl.multiple_of((r * (ROW_BLOCK // blk) + sub) * grp_rows + g * blk, blk)
                    qs_scr[hk, pl.ds(dst, blk), :] = qp[sub * blk:(sub + 1) * blk, half * C_HD:(half + 1) * C_HD]
        return carry

    lax.fori_loop(0, nrb, prep, 0)

    nkeys = 3 * blk + past_len
    kcb = kc_ref[0, 0].astype(BF16)
    vcb = vc_ref[0, 0].astype(BF16)
    qi = lax.broadcasted_iota(jnp.int32, (grp_rows, nkeys), 0) & (blk - 1)
    kj = lax.broadcasted_iota(jnp.int32, (grp_rows, nkeys), 1)
    visible = jnp.logical_or(kj >= 3 * blk, jnp.logical_and(kj >= qi, kj <= qi + 2 * C_WINDOW))
    bias_scr[...] = jnp.where(visible, 0.0, NEG_BIG)
    head_of_row = lax.broadcasted_iota(jnp.int32, (grp_rows, 1), 0) // blk
    kcol = lax.broadcasted_iota(jnp.int32, (1, nkeys), 1)

    sinks = []
    for hk in range(C_KVHEADS):
        sink = jnp.full((grp_rows, 1), sink_ref[layer, hk * C_GROUP], F32)
        for g in range(1, C_GROUP):
            sink = jnp.where(head_of_row == g, sink_ref[layer, hk * C_GROUP + g], sink)
        sinks.append(sink * LOG2E)
    blocks_per_step = 2 if (t // blk) % 2 == 0 else 1

    def qblocks(step, carry):
        jobs, rows = [], []
        for sub in range(blocks_per_step):
            n = step * blocks_per_step + sub
            r0 = pl.multiple_of(n * blk, blk)
            rows.append(r0)
            kpos = kcol + (r0 - blk)
            in_seq = jnp.logical_or(kcol >= 3 * blk, jnp.logical_and(kpos >= 0, kpos < t))
            edge = jnp.where(in_seq, 0.0, NEG_BIG)
            kwin = ks_scr[pl.ds(r0, 3 * blk), :]
            vwin = vs_scr[pl.ds(r0, 3 * blk), :]
            for hk in range(C_KVHEADS):
                lo, hi = hk * C_HD, (hk + 1) * C_HD
                keys = jnp.concatenate([kwin[:, lo:hi], kcb[:, lo:hi]], axis=0)
                vals = jnp.concatenate([vwin[:, lo:hi], vcb[:, lo:hi]], axis=0)
                q_stack = qs_scr[hk, pl.ds(pl.multiple_of(n * grp_rows, grp_rows), grp_rows), :]
                jobs.append((sub, hk, _dot_nt(q_stack, keys), edge, vals))
        outs = [[] for _ in range(blocks_per_step)]
        for sub, hk, s, edge, vals in jobs:
            o = _softmax_sink_av([s + bias_scr[...] + edge], [vals], sinks[hk])
            outs[sub] += [o[g * blk:(g + 1) * blk] for g in range(C_GROUP)]
        for sub in range(blocks_per_step):
            for pair in range(C_QHEADS // 2):
                o_ref[pl.ds(rows[sub], blk), pair * LANES:(pair + 1) * LANES] = jnp.concatenate(
                    outs[sub][2 * pair:2 * pair + 2], axis=1).astype(o_ref.dtype)
        return carry

    lax.fori_loop(0, t // (blk * blocks_per_step), qblocks, 0)


def _attn_lat_call(proj, cache_k, cache_v, prm, cos2, sin2, layer, nseq, t):
    past_len = cache_k.shape[2]
    qw = C_QHEADS * C_HD
    return pl.pallas_call(
        functools.partial(_attn_lat_kernel, t=t, past_len=past_len, layer=layer),
        out_shape=jax.ShapeDtypeStruct((nseq * t, qw), BF16),
        grid=(nseq,),
        in_specs=[pl.BlockSpec((t, qw), lambda b: (b, COL_QC // qw)),
                  pl.BlockSpec((t, LANES), lambda b: (b, COL_KC // LANES)),
                  pl.BlockSpec((t, LANES), lambda b: (b, COL_VC // LANES)),
                  pl.BlockSpec((1, 1, past_len, LANES), lambda b: (b, layer, 0, 0)),
                  pl.BlockSpec((1, 1, past_len, LANES), lambda b: (b, layer, 0, 0)),
                  _layer_block((1, LANES), layer), _layer_block((1, LANES), layer),
                  _resident((t, LANES), lambda b: (0, 0)), _resident((t, LANES), lambda b: (0, 0)),
                  pl.BlockSpec(memory_space=pltpu.SMEM)],
        out_specs=pl.BlockSpec((t, qw), lambda b: (b, 0)),
        scratch_shapes=[pltpu.VMEM((C_KVHEADS, t * C_GROUP, C_HD), BF16),
                        pltpu.VMEM((t + 2 * C_BLOCK, LANES), BF16), pltpu.VMEM((t + 2 * C_BLOCK, LANES), BF16),
                        pltpu.VMEM((C_GROUP * C_BLOCK, 3 * C_BLOCK + past_len), F32)],
        compiler_params=_cparams(("arbitrary",)),
        name="attn_lat",
    )(proj, proj, proj, cache_k, cache_v, prm["q_norm2"], prm["k_norm2"], cos2, sin2, prm["sink"])


def _rope_tables(t):
    rows = t // GRID_W
    row = jnp.repeat(jnp.arange(rows, dtype=F32), GRID_W)
    col = jnp.tile(jnp.arange(GRID_W, dtype=F32), rows)
    nf = C_HD // 4
    inv = ROPE_THETA ** (-jnp.arange(nf, dtype=F32) / nf)
    ar, ac = row[:, None] * inv, col[:, None] * inv
    cos = jnp.concatenate([jnp.cos(ar), jnp.cos(ar), jnp.cos(ac), jnp.cos(ac)], axis=1)
    sin = jnp.concatenate([-jnp.sin(ar), jnp.sin(ar), -jnp.sin(ac), jnp.sin(ac)], axis=1)
    return jnp.tile(cos, (1, 2)), jnp.tile(sin, (1, 2))


def _merge_kernel(x_ref, g0_ref, g1_ref, g2_ref, oa_ref, ob_ref, oc_ref, mod_ref, wbr_ref, wout_ref, o_ref):
    merged = (_sigmoid(g0_ref[...].astype(F32)) * jnp.dot(oa_ref[...], wbr_ref[0], preferred_element_type=F32)
              + _sigmoid(g1_ref[...].astype(F32)) * jnp.dot(ob_ref[...], wbr_ref[1], preferred_element_type=F32)
              + _sigmoid(g2_ref[...].astype(F32)) * jnp.dot(oc_ref[...], wbr_ref[2], preferred_element_type=F32))
    res = jnp.dot(merged.astype(BF16), wout_ref[...], preferred_element_type=F32)
    o_ref[...] = x_ref[...] + mod_ref[0, 2:3, :] * res


def _merge_call(x2d, proj, oa, ob, oc, prm, layer, cond, tm):
    m, d = x2d.shape
    mg = lambda r: pl.BlockSpec((tm, d), lambda i: (i, COL_MG // d + r))
    br = pl.BlockSpec((tm, BRANCH_W), lambda i: (i, 0))
    return pl.pallas_call(
        _merge_kernel,
        out_shape=jax.ShapeDtypeStruct((m, d), F32),
        grid=(m // tm,),
        in_specs=[pl.BlockSpec((tm, d), lambda i: (i, 0)), mg(0), mg(1), mg(2), br, br, br,
                  _mod_block(d, layer, cond), _layer_block((3, BRANCH_W, d), layer), _layer_block((d, d), layer)],
        out_specs=pl.BlockSpec((tm, d), lambda i: (i, 0)),
        compiler_params=_cparams(("arbitrary",)),
        name="merge",
    )(x2d, proj, proj, proj, oa, ob, oc, prm["mod"], prm["w_branch"], prm["w_out"])


FF_CHUNK = 256
HALO = BF16_ROWS


def _ffn_kernel(x_ref, xp_ref, xn_ref, mod_ref, nw_ref, wup_ref, cw_ref, wd_ref, o_ref, h_scr, act_scr, *,
                tm, seq_len):
    i = pl.program_id(0)
    nseg = max(1, tm // seq_len)
    seg = tm // nseg
    nw, sh, sc = nw_ref[...], mod_ref[0, 3:4, :], mod_ref[0, 4:5, :]

    def norm(x):
        return _rms_rows(x, nw) * (1.0 + sc) + sh

    has_prev = ((i * tm) & (seq_len - 1)) != 0
    has_next = (((i + 1) * tm) & (seq_len - 1)) != 0
    zero_halo = jnp.zeros((HALO, x_ref.shape[1]), BF16)
    for s in range(nseg):
        h_scr[s, HALO:HALO + seg, :] = norm(x_ref[s * seg:(s + 1) * seg, :]).astype(BF16)
        if s == 0:
            h_scr[s, 0:HALO, :] = (norm(xp_ref[...]) * jnp.where(has_prev, 1.0, 0.0)).astype(BF16)
        else:
            h_scr[s, 0:HALO, :] = zero_halo
        if s == nseg - 1:
            h_scr[s, HALO + seg:2 * HALO + seg, :] = (norm(xn_ref[...]) * jnp.where(has_next, 1.0, 0.0)).astype(BF16)
        else:
            h_scr[s, HALO + seg:2 * HALO + seg, :] = zero_halo

    def conv(u, cw):
        return (u[HALO - 1:HALO - 1 + seg] * cw[0:1, :] + u[HALO:HALO + seg] * cw[1:2, :]
                + u[HALO + 1:HALO + 1 + seg] * cw[2:3, :])

    for lo, hi in _col_chunks(D_FF, FF_CHUNK):
        for s in range(nseg):
            h = h_scr[s]
            a = conv(jnp.dot(h, wup_ref[:, lo:hi], preferred_element_type=F32), cw_ref[:, lo:hi])
            u = conv(jnp.dot(h, wup_ref[:, D_FF + lo:D_FF + hi], preferred_element_type=F32),
                     cw_ref[:, D_FF + lo:D_FF + hi])
            act_scr[s * seg:(s + 1) * seg, lo:hi] = (_silu(a) * u).astype(BF16)

    o_ref[...] = x_ref[...] + mod_ref[0, 5:6, :] * jnp.dot(act_scr[...], wd_ref[...], preferred_element_type=F32)


def _ffn_call(x2d, prm, layer, cond, tm, seq_len):
    m, d = x2d.shape
    hb = tm // HALO
    last = m // HALO - 1
    nseg = max(1, tm // seq_len)
    seg = tm // nseg
    return pl.pallas_call(
        functools.partial(_ffn_kernel, tm=tm, seq_len=seq_len),
        out_shape=jax.ShapeDtypeStruct((m, d), F32),
        grid=(m // tm,),
        in_specs=[pl.BlockSpec((tm, d), lambda i: (i, 0)),
                  pl.BlockSpec((HALO, d), lambda i: (jnp.maximum(i * hb - 1, 0), 0)),
                  pl.BlockSpec((HALO, d), lambda i: (jnp.minimum((i + 1) * hb, last), 0)),
                  _mod_block(d, layer, cond), _layer_block((1, d), layer), _layer_block((d, 2 * D_FF), layer),
                  _layer_block((3, 2 * D_FF), layer), _layer_block((D_FF, d), layer)],
        out_specs=pl.BlockSpec((tm, d), lambda i: (i, 0)),
        scratch_shapes=[pltpu.VMEM((nseg, seg + 2 * HALO, d), BF16), pltpu.VMEM((tm, D_FF), BF16)],
        compiler_params=_cparams(("arbitrary",)),
        name="ffn",
    )(x2d, x2d, x2d, prm["mod"], prm["norm2_w"], prm["w_up"], prm["conv_ffn"], prm["w_down"])


def _permute_w_in(w):
    s = _SRC
    w16 = jnp.concatenate([w[..., s["mg"]:s["end"]], w[..., s["qa"]:s["beta"]], w[..., s["qb"]:s["fb"]],
                           w[..., s["gb"]:s["mg"]]], axis=-1)
    n_gate = s["qb"] - s["beta"]
    gates = jnp.pad(w[..., s["beta"]:s["qb"]], ((0, 0), (0, 0), (0, LANES - n_gate)))
    w32 = jnp.concatenate([w[..., s["fb"]:s["gb"]], gates], axis=-1)
    return w16.astype(BF16), w32.astype(BF16)


def _gate_rows(proj32, nseq, t, c):
    g = proj32[:, COL32_GATES:COL32_GATES + 4 * HEADS].reshape(nseq, t // c, c, 2, 2, HEADS)
    return jnp.transpose(g, (0, 5, 1, 3, 4, 2)).reshape(nseq, HEADS, t // c, 4, c)


def _row_tile(rows, t):
    tm = 512
    while rows % tm or (t % tm and tm % t):
        tm //= 2
    return tm


def _group_forward(x3d, first_cond, shared_cond, prm, past, tables):
    nseq, t, d = x3d.shape
    x = x3d.reshape(nseq * t, d)
    tm = _row_tile(nseq * t, t)
    tiles_per_seq = max(1, t // tm)
    cond = (lambda i: first_cond) if shared_cond else (lambda i: first_cond + i // tiles_per_seq)
    emit = past is None
    states_a, states_b, keys, vals = [], [], [], []
    for l in range(prm["w16"].shape[0]):
        proj, proj32 = _in_proj_call(x, prm, l, cond, tm)
        gates_r = _gate_rows(proj32, nseq, t, min(CHUNK_A, t))
        res_ab = _mixer_ab_call(proj, proj32, gates_r, prm, past, l, nseq, t, emit)
        if emit:
            oc, kn, vn = _attn_ctx_call(proj, prm, l, nseq, t)
            states_a.append(res_ab[2])
            states_b.append(res_ab[3])
            keys.append(kn.reshape(nseq, t, C_KVHEADS, C_HD))
            vals.append(vn.reshape(nseq, t, C_KVHEADS, C_HD))
        else:
            oc = _attn_lat_call(proj, past[2], past[3], prm, tables[0], tables[1], l, nseq, t)
        x = _merge_call(x, proj, res_ab[0], res_ab[1], oc, prm, l, cond, tm)
        x = _ffn_call(x, prm, l, cond, tm, t)
    return x.reshape(nseq, t, d), states_a, states_b, keys, vals


def kernel(x_prompt, x_sample, state_delta, state_hgrn, cache_k, cache_v, c, c_ctx, ada_w, ada_b, norm1_w, w_in, conv_a, a_log, dt_bias, norm_a, lb_logits, norm_b, q_norm, k_norm, sink, w_branch, w_out, norm2_w, w_up, conv_ffn, w_down):
    depth = w_in.shape[0]
    d = x_prompt.shape[-1]

    cond = jnp.concatenate([c_ctx[None, :], c], axis=0)
    rows = -(-cond.shape[0] // SUBLANES) * SUBLANES
    cond = jnp.pad(cond, ((0, rows - cond.shape[0]), (0, 0)))
    mod_all = _mod_call(cond, ada_w, ada_b).reshape(depth, rows, 6, d)

    w16, w32 = _permute_w_in(w_in)
    prm = dict(
        mod=mod_all, w16=w16, w32=w32,
        norm1_w=norm1_w.reshape(depth, 1, d), norm2_w=norm2_w.reshape(depth, 1, d),
        conv_a=conv_a, a_log=a_log, dt_bias=dt_bias, norm_a=norm_a.reshape(depth, 1, HEAD_W),
        lb_logits=lb_logits, norm_b=norm_b.reshape(depth, 1, HEAD_W),
        q_norm2=jnp.tile(q_norm, (1, 2)).reshape(depth, 1, LANES), k_norm2=jnp.tile(k_norm, (1, 2)).reshape(depth, 1, LANES),
        sink=sink, w_branch=w_branch.astype(BF16), w_out=w_out.astype(BF16),
        w_up=w_up.astype(BF16), conv_ffn=conv_ffn, w_down=w_down.astype(BF16))

    y_prompt, st_a, st_b, keys, vals = _group_forward(x_prompt, 0, True, prm, None, None)

    past_len = cache_k.shape[2]
    past = (state_delta, state_hgrn,
            cache_k.reshape(cache_k.shape[0], depth, past_len, C_KVHEADS * C_HD),
            cache_v.reshape(cache_v.shape[0], depth, past_len, C_KVHEADS * C_HD))
    y_sample, _, _, _, _ = _group_forward(x_sample, 1, False, prm, past, _rope_tables(x_sample.shape[1]))

    return (y_prompt, y_sample, jnp.stack(st_a, axis=1), jnp.stack(st_b, axis=1),
            jnp.stack(keys, axis=1), jnp.stack(vals, axis=1))
```

```python
import functools
import math

import numpy as np
import jax
import jax.numpy as jnp
from jax import lax
from jax.experimental import pallas as pl
from jax.experimental.pallas import tpu as pltpu

F32 = jnp.float32
BF16 = jnp.bfloat16

D_MODEL = 1024
NORM_EPS = 1e-6
LB_EPS = 1e-6
NEG_BIG = -1e30
LOG2E = 1.4426950408889634
GRID_W = 64
ROPE_THETA = 10000.0

HEADS = 4
HEAD_W = 128
SHORT_CONV = 5
C_QHEADS = 8
C_KVHEADS = 2
C_GROUP = C_QHEADS // C_KVHEADS
C_HD = 64
C_WINDOW = 128
C_BLOCK = 128
BRANCH_W = 512
D_FF = 2816

LANES = 128
SUBLANES = 8
BF16_ROWS = 16
VMEM_LIMIT = 56 * 1024 * 1024

COL_MG = 0
COL_QA = 3072
COL_KA = 3584
COL_VA = 4096
COL_GA = 4608
COL_QB = 5120
COL_IB = 5632
COL_GB = 6144
COL_QC = 6656
COL_KC = 7168
COL_VC = 7296
PROJ_W = 7424
COL32_FB = 0
COL32_GATES = 1024
PROJ32_W = 1152
_SRC = dict(qa=0, ka=512, va=1024, ga=1536, beta=2048, alpha=2056, qb=2064, ib=2576, fb=3088, gb=4112,
            qc=4624, kc=5136, vc=5264, mg=5392, end=8464)

CHUNK_A = 128
ROUNDS_A = 8
CHUNK_B = 64
TRI_BASE = 16
TRI_MERGE = 2
MM_TILE = 768


def _cparams(sem):
    return pltpu.CompilerParams(dimension_semantics=sem, vmem_limit_bytes=VMEM_LIMIT)


def _resident(shape, index_map):
    return pl.BlockSpec(shape, index_map, pipeline_mode=pl.Buffered(1))


def _layer_block(shape, layer):
    return _resident((None,) + tuple(shape), lambda *_: (layer,) + (0,) * len(shape))


def _mod_block(d, layer, cond):
    return pl.BlockSpec((None, 1, 6, d), lambda i: (layer, cond(i), 0, 0))


def _dot(a, b):
    return jnp.dot(a.astype(BF16), b.astype(BF16), preferred_element_type=F32)


def _dot_nt(a, b):
    return lax.dot_general(a.astype(BF16), b.astype(BF16), (((1,), (1,)), ((), ())), preferred_element_type=F32)


def _dot_tn(a, b):
    return lax.dot_general(a.astype(BF16), b.astype(BF16), (((0,), (0,)), ((), ())), preferred_element_type=F32)


def _sigmoid(x):
    return 1.0 / (1.0 + jnp.exp(-x))


def _silu(x):
    return x * _sigmoid(x)


def _softplus(x):
    return jnp.maximum(x, 0.0) + jnp.log(1.0 + jnp.exp(-jnp.abs(x)))


def _rms_rows(x, w):
    ms = jnp.mean(x * x, axis=-1, keepdims=True)
    return x * lax.rsqrt(ms + NORM_EPS) * w


def _split_bf16(x, n):
    parts, r = [], x
    for _ in range(n):
        p = r.astype(BF16)
        parts.append(p)
        r = r - p.astype(F32)
    return parts


def _col_chunks(width, step):
    return [(lo, min(lo + step, width)) for lo in range(0, width, step)]


def _mod_kernel(c_ref, w_ref, b_ref, o_ref):
    c = c_ref[...]
    o_ref[0] = _dot(_silu(c), w_ref[0]) + b_ref[0]


def _mod_call(cond, ada_w, ada_b):
    depth, d, n = ada_w.shape
    rows = cond.shape[0]
    tn = 768
    return pl.pallas_call(
        _mod_kernel,
        out_shape=jax.ShapeDtypeStruct((depth, rows, n), F32),
        grid=(depth, n // tn),
        in_specs=[pl.BlockSpec((rows, d), lambda l, j: (0, 0)),
                  pl.BlockSpec((1, d, tn), lambda l, j: (l, 0, j)),
                  pl.BlockSpec((1, 1, tn), lambda l, j: (l, 0, j))],
        out_specs=pl.BlockSpec((1, rows, tn), lambda l, j: (l, 0, j)),
        compiler_params=_cparams(("arbitrary", "arbitrary")),
        name="mod",
    )(cond, ada_w, ada_b.reshape(depth, 1, n))


def _in_proj_kernel(x_ref, mod_ref, nw_ref, w16_ref, w32_ref, o16_ref, o32_ref):
    h = _rms_rows(x_ref[...], nw_ref[...]) * (1.0 + mod_ref[0, 1:2, :]) + mod_ref[0, 0:1, :]
    hb = h.astype(BF16)
    for lo, hi in _col_chunks(PROJ_W, MM_TILE):
        o16_ref[:, lo:hi] = jnp.dot(hb, w16_ref[:, lo:hi], preferred_element_type=F32).astype(BF16)
    for lo, hi in _col_chunks(PROJ32_W, MM_TILE):
        o32_ref[:, lo:hi] = jnp.dot(hb, w32_ref[:, lo:hi], preferred_element_type=F32)


def _in_proj_call(x2d, prm, layer, cond, tm):
    m, d = x2d.shape
    return pl.pallas_call(
        _in_proj_kernel,
        out_shape=(jax.ShapeDtypeStruct((m, PROJ_W), BF16), jax.ShapeDtypeStruct((m, PROJ32_W), F32)),
        grid=(m // tm,),
        in_specs=[pl.BlockSpec((tm, d), lambda i: (i, 0)), _mod_block(d, layer, cond),
                  _layer_block((1, d), layer), _layer_block((d, PROJ_W), layer), _layer_block((d, PROJ32_W), layer)],
        out_specs=(pl.BlockSpec((tm, PROJ_W), lambda i: (i, 0)),
                   pl.BlockSpec((tm, PROJ32_W), lambda i: (i, 0))),
        compiler_params=_cparams(("arbitrary",)),
        name="in_proj",
    )(x2d, prm["mod"], prm["norm1_w"], prm["w16"], prm["w32"])


ROW_BLOCK = 256


def _gated_norm_epilogue(of_scr, ob_scr, gate_ref, nw_ref, o_ref, t, cols):
    nw = nw_ref[...]

    def body(r, carry):
        r0 = pl.multiple_of(r * ROW_BLOCK, ROW_BLOCK)
        o = of_scr[pl.ds(r0, ROW_BLOCK), :] + ob_scr[pl.ds(r0, ROW_BLOCK), :]
        y = _rms_rows(o, nw) * _silu(gate_ref[pl.ds(r0, ROW_BLOCK), cols].astype(F32))
        o_ref[pl.ds(r0, ROW_BLOCK), cols] = y.astype(o_ref.dtype)
        return carry

    lax.fori_loop(0, t // ROW_BLOCK, body, 0)


def _tri_inverse(mats, ri, ci, c, tick):
    n = range(len(mats))
    shift = int(np.log2(TRI_BASE))
    base = (ri >> shift) == (ci >> shift)
    mb = [jnp.where(base, m, 0.0) for m in mats]
    y = [-m for m in mb]
    p = [_dot(m, m) for m in mb]
    tick()
    for _ in range(shift - 2):
        yp = [_dot(jnp.concatenate([y[i], p[i]], axis=0), p[i]) for i in n]
        tick()
        y = [y[i] + p[i] + yp[i][:c] for i in n]
        p = [r[c:] for r in yp]
    yp = [_dot(y[i], p[i]) for i in n]
    tick()
    y = [y[i] + p[i] + yp[i] for i in n]
    total = int(np.log2(c))
    while shift < total:
        factors = min(TRI_MERGE, total - shift)
        inner = (ri >> shift) == (ci >> shift)
        outer = (ri >> (shift + factors)) == (ci >> (shift + factors))
        between = jnp.logical_and(outer, jnp.logical_not(inner))
        cm = [jnp.where(between, m, 0.0) for m in mats]
        w = [_dot(y[i], cm[i]) for i in n]
        tick()
        w = [cm[i] + w[i] for i in n]
        if factors == 1:
            wy = [_dot(w[i], y[i]) for i in n]
            tick()
            y = [y[i] - w[i] - wy[i] for i in n]
        else:
            r = [_dot(w[i], jnp.concatenate([w[i], y[i]], axis=1)) for i in n]
            tick()
            p, y = [x[:, :c] for x in r], [y[i] - w[i] - r[i][:, c:] for i in n]
            for f in range(1, factors):
                if f + 1 < factors:
                    r = [_dot(p[i], jnp.concatenate([p[i], y[i]], axis=1)) for i in n]
                    tick()
                    p, y = [x[:, :c] for x in r], [y[i] + p[i] + r[i][:, c:] for i in n]
                else:
                    py = [_dot(p[i], y[i]) for i in n]
                    tick()
                    y = [y[i] + p[i] + py[i] for i in n]
        shift += factors
    eye = jnp.where(ri == ci, 1.0, 0.0)
    return [eye + v for v in y]


def _delta_chunks(probs, states, c, tick):
    n = range(len(probs))
    ri = lax.broadcasted_iota(jnp.int32, (c, c), 0)
    ci = lax.broadcasted_iota(jnp.int32, (c, c), 1)
    eye = ri == ci
    pre = []
    for q, k, v, beta_row, alpha_row, a_neg, dt_b, chain, reverse in probs:
        incl, strict = (ri <= ci, ri < ci) if reverse else (ri >= ci, ri > ci)
        beta_r = _sigmoid(beta_row)
        g_r = a_neg * _softplus(alpha_row + dt_b)
        gc_col = jnp.sum(jnp.where(incl, jnp.broadcast_to(g_r, (c, c)), 0.0), axis=1, keepdims=True)
        beta_col = jnp.sum(jnp.where(eye, jnp.broadcast_to(beta_r, (c, c)), 0.0), axis=1, keepdims=True)
        gc_row = jnp.sum(jnp.where(eye, jnp.broadcast_to(gc_col, (c, c)), 0.0), axis=0, keepdims=True)
        g_tot = jnp.sum(g_r, axis=1, keepdims=True)
        decay = jnp.where(incl, jnp.exp(jnp.where(incl, gc_col - gc_row, 0.0)), 0.0)
        pre.append((strict, gc_col, beta_col, g_tot, decay, jnp.exp(gc_col)))

    kb = [pr[1].astype(BF16) for pr in probs]
    kq = [_dot_nt(jnp.concatenate([kb[i], probs[i][0].astype(BF16)], axis=0), kb[i]) for i in n]
    tick()
    t_inv = _tri_inverse([jnp.where(pre[i][0], kq[i][:c] * pre[i][2] * pre[i][4], 0.0) for i in n], ri, ci, c, tick)
    uw = [_dot(t_inv[i], jnp.concatenate([probs[i][2] * pre[i][2], probs[i][1] * (pre[i][2] * pre[i][5])], axis=1))
          for i in n]
    tick()
    wq_lhs = [jnp.concatenate([uw[i][:, HEAD_W:], probs[i][0] * pre[i][5]], axis=0).astype(BF16) for i in n]
    os_lhs = [jnp.concatenate([kq[i][c:] * pre[i][4], (probs[i][1] * jnp.exp(pre[i][3] - pre[i][1])).T],
                              axis=0).astype(BF16) for i in n]
    states = list(states)
    outs = [None] * len(probs)
    todo = list(n)
    while todo:
        front, seen = [], set()
        for i in todo:
            if probs[i][7] not in seen:
                seen.add(probs[i][7])
                front.append(i)
        todo = [i for i in todo if i not in front]
        wq = [_dot(wq_lhs[i], states[probs[i][7]]) for i in front]
        tick()
        v_new = [uw[i][:, :HEAD_W] - wq[j][:c] for j, i in enumerate(front)]
        os_ = [_dot(os_lhs[i], v_new[j]) for j, i in enumerate(front)]
        tick()
        for j, i in enumerate(front):
            outs[i] = wq[j][c:] + os_[j][:c]
            states[probs[i][7]] = states[probs[i][7]] * jnp.exp(pre[i][3]) + os_[j][c:]
    return outs, states


def _conv_silu_pass(x_ref, cw_ref, xp_scr, dst_scr, t, l2, scale, cols):
    nrb = t // ROW_BLOCK
    pad = SUBLANES
    half = SHORT_CONV // 2

    def cp(r, carry):
        r0 = pl.multiple_of(r * ROW_BLOCK, ROW_BLOCK)
        xp_scr[pl.ds(r0 + pad, ROW_BLOCK), :] = x_ref[pl.ds(r0, ROW_BLOCK), cols].astype(F32)
        return carry

    lax.fori_loop(0, nrb, cp, 0)
    cw = cw_ref[:, cols]

    def body(r, carry):
        r0 = pl.multiple_of(r * ROW_BLOCK, ROW_BLOCK)
        y = None
        for j in range(SHORT_CONV):
            term = xp_scr[pl.ds(r0 + (pad - half + j), ROW_BLOCK), :] * cw[j:j + 1, :]
            y = term if y is None else y + term
        y = _silu(y)
        if l2:
            y = y * (lax.rsqrt(jnp.sum(y * y, axis=-1, keepdims=True) + NORM_EPS) * scale)
        dst_scr[pl.ds(r0, ROW_BLOCK), :] = y
        return carry

    lax.fori_loop(0, nrb, body, 0)


def _hgrn_diag(q, kf, v, b, reverse, c):
    nb = c // SUBLANES
    q3, k3, v3, b3 = (a.reshape(nb, SUBLANES, HEAD_W) for a in (q, kf, v, b))
    sub = lax.broadcasted_iota(jnp.int32, (nb, SUBLANES, HEAD_W), 1)
    o3 = jnp.zeros((nb, SUBLANES, HEAD_W), F32)
    for j in range(SUBLANES):
        mask = (sub <= j) if reverse else (sub >= j)
        e = jnp.exp2(jnp.where(mask, b3 - b3[:, j:j + 1, :], NEG_BIG))
        a = jnp.sum(q3 * e * k3[:, j:j + 1, :], axis=-1, keepdims=True)
        o3 = o3 + a * v3[:, j:j + 1, :]
    return o3.reshape(c, HEAD_W)


def _hgrn_chunk(q, kf, v, lf, st, reverse, c):
    ri = lax.broadcasted_iota(jnp.int32, (c, c), 0)
    ci = lax.broadcasted_iota(jnp.int32, (c, c), 1)
    incl = (ri <= ci) if reverse else (ri >= ci)
    parts = jnp.concatenate(_split_bf16(lf, 3), axis=1)
    b3 = jnp.dot(jnp.where(incl, 1.0, 0.0).astype(BF16), parts, preferred_element_type=F32)
    b = b3[:, :HEAD_W] + b3[:, HEAD_W:2 * HEAD_W] + b3[:, 2 * HEAD_W:]
    b_tot = jnp.sum(lf, axis=0, keepdims=True)
    o = _dot_nt(q * jnp.exp2(b), st)
    row = lax.broadcasted_iota(jnp.int32, (c, 1), 0)
    att = jnp.zeros((c, c), F32)
    n = SUBLANES
    while n < c:
        pieces = []
        for g in range(c // (2 * n)):
            r = g * 2 * n + (n if reverse else n - 1)
            pieces.append(jnp.broadcast_to(b[r:r + 1, :], (2 * n, HEAD_W)))
        ref = pieces[0] if len(pieces) == 1 else jnp.concatenate(pieces, axis=0)
        e = jnp.exp2(-jnp.abs(b - ref))
        s = int(np.log2(n))
        second = ((row >> s) & 1) == 1
        q_part = jnp.logical_not(second) if reverse else second
        qt = jnp.where(q_part, q * e, 0.0)
        kt = jnp.where(q_part, 0.0, kf * e)
        att = att + jnp.where((ri >> (s + 1)) == (ci >> (s + 1)), _dot_nt(qt, kt), 0.0)
        n *= 2
    o = o + _dot(att, v) + _hgrn_diag(q, kf, v, b, reverse, c)
    kh = kf * jnp.exp2(b_tot - b)
    st_new = st * jnp.exp2(b_tot) + _dot_tn(v, kh)
    return o, st_new


def _mixer_ab_kernel(*refs, t, ca, cb, layer, hps, has_past, emit_state):
    refs = list(refs)
    (qa_ref, ka_ref, va_ref, ga_ref, gr_ref, cwq_ref, cwk_ref, cwv_ref, alog_ref, dtb_ref, nwa_ref,
     qb_ref, ib_ref, f0_ref, f1_ref, gb_ref, lb_ref, nwb_ref) = refs[:18]
    pos = 18
    sa0_ref = sb0_ref = None
    if has_past:
        sa0_ref, sb0_ref = refs[pos:pos + 2]
        pos += 2
    oa_ref, ob_ref = refs[pos:pos + 2]
    pos += 2
    sfa_ref = sfb_ref = None
    if emit_state:
        sfa_ref, sfb_ref = refs[pos:pos + 2]
        pos += 2
    xp_scr, qn_scr, kn_scr, vn_scr, af_scr, ab_scr, bf_scr, bb_scr, sa_scr, sb_scr = refs[pos:]

    heads = range(hps)
    cols = [slice(hh * HEAD_W, (hh + 1) * HEAD_W) for hh in heads]
    first_head = pl.program_id(1) * hps
    pad = SUBLANES
    xp_scr[0:pad, :] = jnp.zeros((pad, HEAD_W), F32)
    xp_scr[t + pad:t + 2 * pad, :] = jnp.zeros((pad, HEAD_W), F32)
    for hh in heads:
        _conv_silu_pass(qa_ref, cwq_ref, xp_scr, qn_scr.at[hh], t, True, HEAD_W ** -0.5, cols[hh])
        _conv_silu_pass(ka_ref, cwk_ref, xp_scr, kn_scr.at[hh], t, True, 1.0, cols[hh])
        _conv_silu_pass(va_ref, cwv_ref, xp_scr, vn_scr.at[hh], t, False, 1.0, cols[hh])

    if has_past:
        for hh in heads:
            for d in range(2):
                sa_scr[2 * hh + d] = sa0_ref[0, 0, d, hh]
                sb_scr[2 * hh + d] = sb0_ref[0, 0, d, hh].T
    else:
        sa_scr[...] = jnp.zeros(sa_scr.shape, F32)
        sb_scr[...] = jnp.zeros(sb_scr.shape, F32)

    ones = jnp.ones((1, ca), F32)
    a_neg = [[-jnp.exp(ones * alog_ref[layer, d, first_head + hh]) for d in range(2)] for hh in heads]
    dt_b = [[dtb_ref[layer, d, first_head + hh] for d in range(2)] for hh in heads]

    lb_terms = None
    if layer > 0:
        lb_terms = []
        for hh in heads:
            terms = []
            for d in range(2):
                lg = lb_ref[d][:, cols[hh]]
                ex = jnp.exp(lg - jnp.max(lg, axis=0, keepdims=True))
                pr = ex / jnp.sum(ex, axis=0, keepdims=True)
                lb = jnp.clip(jnp.sum(pr[1:layer + 1], axis=0, keepdims=True), LB_EPS, 1.0 - LB_EPS)
                terms.append((jnp.log(lb) * LOG2E, jnp.log1p(-lb) * LOG2E, 1.0 - lb))
            lb_terms.append(terms)

    f_refs = (f0_ref, f1_ref)
    nca, ncb = t // ca, t // cb
    rounds = math.gcd(ROUNDS_A // hps, nca)
    ratio = rounds * ca // cb

    def hgrn_gates(z, hh, d):
        z2 = z * LOG2E
        y = jnp.exp2(-jnp.abs(z2))
        one_y = 1.0 + y
        l2_sig = jnp.minimum(z2, 0.0) - jnp.log(one_y) * LOG2E
        sig_neg = jnp.where(z >= 0.0, y, 1.0) / one_y
        if layer == 0:
            return l2_sig, sig_neg
        l2_lb, l2_1m_lb, one_m_lb = lb_terms[hh][d]
        a2 = l2_1m_lb + l2_sig
        return (jnp.maximum(l2_lb, a2) + jnp.log(1.0 + jnp.exp2(-jnp.abs(l2_lb - a2))) * LOG2E,
                one_m_lb * sig_neg)

    def step(n, carry):
        a_dst, a_probs = [], []
        for rnd in range(rounds):
            for hh in heads:
                for d in range(2):
                    cidx = n * rounds + rnd if d == 0 else nca - 1 - (n * rounds + rnd)
                    r0 = pl.multiple_of(cidx * ca, ca)
                    gr = gr_ref[0, hh, cidx]
                    a_dst.append((af_scr if d == 0 else ab_scr, hh, r0))
                    a_probs.append((qn_scr[hh, pl.ds(r0, ca), :], kn_scr[hh, pl.ds(r0, ca), :],
                                    vn_scr[hh, pl.ds(r0, ca), :], gr[d:d + 1, :], gr[2 + d:3 + d, :],
                                    a_neg[hh][d], dt_b[hh][d], 2 * hh + d, d == 1))
        sa = [sa_scr[chain] for chain in range(2 * hps)]
        b_jobs = []
        for j in range(ratio):
            m = n * ratio + j
            for hh in heads:
                for d in range(2):
                    r0 = pl.multiple_of((m if d == 0 else ncb - 1 - m) * cb, cb)
                    b_jobs.append((hh, d, r0, f_refs[d][pl.ds(r0, cb), cols[hh]], qb_ref[pl.ds(r0, cb), cols[hh]],
                                   ib_ref[pl.ds(r0, cb), cols[hh]]))
        sb = [sb_scr[chain] for chain in range(2 * hps)]

        b_out = []
        pending = list(b_jobs)

        def tick():
            if pending:
                hh, d, _, z, qraw, iraw = pending.pop(0)
                lf, kf = hgrn_gates(z, hh, d)
                o, sb[2 * hh + d] = _hgrn_chunk(_silu(qraw.astype(F32)), kf, iraw.astype(F32), lf, sb[2 * hh + d],
                                                d == 1, cb)
                b_out.append(o)

        a_out, sa = _delta_chunks(a_probs, sa, ca, tick)
        while pending:
            tick()

        for (dst, hh, r0), o in zip(a_dst, a_out):
            dst[hh, pl.ds(r0, ca), :] = o
        for (hh, d, r0, _, _, _), o in zip(b_jobs, b_out):
            (bf_scr if d == 0 else bb_scr)[hh, pl.ds(r0, cb), :] = o
        for chain in range(2 * hps):
            sa_scr[chain] = sa[chain]
            sb_scr[chain] = sb[chain]
        return carry

    lax.fori_loop(0, nca // rounds, step, 0)

    for hh in heads:
        _gated_norm_epilogue(af_scr.at[hh], ab_scr.at[hh], ga_ref, nwa_ref, oa_ref, t, cols[hh])
        _gated_norm_epilogue(bf_scr.at[hh], bb_scr.at[hh], gb_ref, nwb_ref, ob_ref, t, cols[hh])
    if emit_state:
        for hh in heads:
            for d in range(2):
                sfa_ref[0, d, hh] = sa_scr[2 * hh + d]
                sfb_ref[0, d, hh] = sb_scr[2 * hh + d].T


def _mixer_ab_call(proj, proj32, gates_r, prm, past, layer, nseq, t, emit_state):
    ca, cb = min(CHUNK_A, t), min(CHUNK_B, t)
    depth = prm["lb_logits"].shape[1]
    per_head = t * HEAD_W * (7 * 4 + 2 * (7 * 2 + 2 * 4) + 2 * 2 * 2)
    hps = max(n for n in (1, 2, 4) if n == 1 or n * per_head <= VMEM_LIMIT // 2)
    slab = hps * HEAD_W
    col = lambda off: (lambda b, h: (b, off // slab + h))
    seq_in = lambda off: pl.BlockSpec((t, slab), col(off))
    conv = lambda part: pl.BlockSpec((None, SHORT_CONV, slab), lambda b, h: (layer, 0, part * (HEADS // hps) + h))
    smem = pl.BlockSpec(memory_space=pltpu.SMEM)
    norm = _layer_block((1, HEAD_W), layer)
    in_specs = [seq_in(COL_QA), seq_in(COL_KA), seq_in(COL_VA), seq_in(COL_GA),
                pl.BlockSpec((1, hps, t // ca, 4, ca), lambda b, h: (b, h, 0, 0, 0)),
                conv(0), conv(1), conv(2), smem, smem, norm,
                seq_in(COL_QB), seq_in(COL_IB), seq_in(COL32_FB), seq_in(COL32_FB + HEADS * HEAD_W), seq_in(COL_GB),
                pl.BlockSpec((2, depth, slab), lambda b, h: (0, 0, h)), norm]
    args = [proj, proj, proj, proj, gates_r, prm["conv_a"], prm["conv_a"], prm["conv_a"],
            prm["a_log"], prm["dt_bias"], prm["norm_a"],
            proj, proj, proj32, proj32, proj, prm["lb_logits"], prm["norm_b"]]
    state_in = pl.BlockSpec((1, 1, 2, hps, HEAD_W, HEAD_W), lambda b, h: (b, layer, 0, h, 0, 0))
    if past is not None:
        in_specs += [state_in, state_in]
        args += [past[0], past[1]]
    o_shape = jax.ShapeDtypeStruct((nseq * t, HEADS * HEAD_W), BF16)
    o_spec = pl.BlockSpec((t, slab), lambda b, h: (b, h))
    out_shape, out_specs = [o_shape, o_shape], [o_spec, o_spec]
    if emit_state:
        s_shape = jax.ShapeDtypeStruct((nseq, 2, HEADS, HEAD_W, HEAD_W), F32)
        s_spec = pl.BlockSpec((1, 2, hps, HEAD_W, HEAD_W), lambda b, h: (b, 0, h, 0, 0))
        out_shape += [s_shape, s_shape]
        out_specs += [s_spec, s_spec]
    seq = pltpu.VMEM((hps, t, HEAD_W), F32)
    state = pltpu.VMEM((2 * hps, HEAD_W, HEAD_W), F32)
    return pl.pallas_call(
        functools.partial(_mixer_ab_kernel, t=t, ca=ca, cb=cb, layer=layer, hps=hps, has_past=past is not None,
                          emit_state=emit_state),
        out_shape=tuple(out_shape), grid=(nseq, HEADS // hps), in_specs=in_specs, out_specs=tuple(out_specs),
        scratch_shapes=[pltpu.VMEM((t + 2 * SUBLANES, HEAD_W), F32), seq, seq, seq, seq, seq, seq, seq, state, state],
        compiler_params=_cparams(("arbitrary", "arbitrary")),
        name="mixer_ab",
    )(*args)


def _rms_head_pairs(x, w2):
    lane = lax.broadcasted_iota(jnp.int32, x.shape, 1)
    left = lane < C_HD
    sq = x * x
    s0 = jnp.sum(jnp.where(left, sq, 0.0), axis=-1, keepdims=True)
    s1 = jnp.sum(jnp.where(left, 0.0, sq), axis=-1, keepdims=True)
    ms = jnp.where(left, s0, s1) * (1.0 / C_HD)
    return x * lax.rsqrt(ms + NORM_EPS) * w2


def _rope_pairs(x, cos2, sin2):
    lane = lax.broadcasted_iota(jnp.int32, x.shape, 1)
    quarter = C_HD // 4
    swapped = jnp.where((lane & (2 * quarter - 1)) < quarter,
                        pltpu.roll(x, LANES - quarter, axis=1), pltpu.roll(x, quarter, axis=1))
    return x * cos2 + swapped * sin2


def _softmax_sink_av(scores, values, sink):
    m = sink
    for s in scores:
        m = jnp.maximum(m, jnp.max(s, axis=-1, keepdims=True))
    den = jnp.exp2(sink - m)
    acc = None
    for s, v in zip(scores, values):
        p = jnp.exp2(s - m)
        den = den + jnp.sum(p, axis=-1, keepdims=True)
        t = jnp.dot(p.astype(BF16), v, preferred_element_type=F32)
        acc = t if acc is None else acc + t
    return acc / den


def _attn_ctx_kernel(q_ref, k_ref, v_ref, qn_ref, kn_ref, sink_ref, o_ref, ko_ref, vo_ref, *, t, layer):
    qw, kw = qn_ref[...], kn_ref[...]
    kn = _rms_head_pairs(k_ref[...].astype(F32), kw)
    ko_ref[0] = kn
    v = v_ref[...]
    vo_ref[0] = v.astype(F32)
    knb, vb = kn.astype(BF16), v
    scale = C_HD ** -0.5 * LOG2E
    heads = []
    for pair in range(C_QHEADS // 2):
        qp = (_rms_head_pairs(q_ref[:, pair * LANES:(pair + 1) * LANES].astype(F32), qw) * scale).astype(BF16)
        heads += [qp[:, :C_HD], qp[:, C_HD:]]
    head_of_row = lax.broadcasted_iota(jnp.int32, (C_GROUP * t, 1), 0) // t
    scores, sinks = [], []
    for hk in range(C_KVHEADS):
        q_stack = jnp.concatenate(heads[hk * C_GROUP:(hk + 1) * C_GROUP], axis=0)
        scores.append(_dot_nt(q_stack, knb[:, hk * C_HD:(hk + 1) * C_HD]))
        sink = jnp.full((C_GROUP * t, 1), sink_ref[layer, hk * C_GROUP], F32)
        for g in range(1, C_GROUP):
            sink = jnp.where(head_of_row == g, sink_ref[layer, hk * C_GROUP + g], sink)
        sinks.append(sink * LOG2E)
    outs = []
    for hk in range(C_KVHEADS):
        o = _softmax_sink_av([scores[hk]], [vb[:, hk * C_HD:(hk + 1) * C_HD]], sinks[hk])
        outs += [o[g * t:(g + 1) * t] for g in range(C_GROUP)]
    for pair in range(C_QHEADS // 2):
        o_ref[:, pair * LANES:(pair + 1) * LANES] = jnp.concatenate(outs[2 * pair:2 * pair + 2],
                                                                    axis=1).astype(o_ref.dtype)


def _attn_ctx_call(proj, prm, layer, nseq, t):
    return pl.pallas_call(
        functools.partial(_attn_ctx_kernel, t=t, layer=layer),
        out_shape=(jax.ShapeDtypeStruct((nseq * t, C_QHEADS * C_HD), BF16),
                   jax.ShapeDtypeStruct((nseq, t, LANES), F32), jax.ShapeDtypeStruct((nseq, t, LANES), F32)),
        grid=(nseq,),
        in_specs=[pl.BlockSpec((t, C_QHEADS * C_HD), lambda b: (b, COL_QC // (C_QHEADS * C_HD))),
                  pl.BlockSpec((t, LANES), lambda b: (b, COL_KC // LANES)),
                  pl.BlockSpec((t, LANES), lambda b: (b, COL_VC // LANES)),
                  _layer_block((1, LANES), layer), _layer_block((1, LANES), layer),
                  pl.BlockSpec(memory_space=pltpu.SMEM)],
        out_specs=(pl.BlockSpec((t, C_QHEADS * C_HD), lambda b: (b, 0)),
                   pl.BlockSpec((1, t, LANES), lambda b: (b, 0, 0)), pl.BlockSpec((1, t, LANES), lambda b: (b, 0, 0))),
        compiler_params=_cparams(("arbitrary",)),
        name="attn_ctx",
    )(proj, proj, proj, prm["q_norm2"], prm["k_norm2"], prm["sink"])


def _attn_lat_kernel(q_ref, k_ref, v_ref, kc_ref, vc_ref, qn_ref, kn_ref, cos_ref, sin_ref, sink_ref, o_ref,
                     qs_scr, ks_scr, vs_scr, bias_scr, *, t, past_len, layer):
    qw, kw = qn_ref[...], kn_ref[...]
    scale = C_HD ** -0.5 * LOG2E
    nrb = t // ROW_BLOCK
    blk = C_BLOCK

    ks_scr[0:blk, :] = jnp.zeros((blk, LANES), BF16)
    vs_scr[0:blk, :] = jnp.zeros((blk, LANES), BF16)
    ks_scr[t + blk:t + 2 * blk, :] = jnp.zeros((blk, LANES), BF16)
    vs_scr[t + blk:t + 2 * blk, :] = jnp.zeros((blk, LANES), BF16)

    grp_rows = C_GROUP * blk

    def prep(r, carry):
        r0 = pl.multiple_of(r * ROW_BLOCK, ROW_BLOCK)
        cos2, sin2 = cos_ref[pl.ds(r0, ROW_BLOCK), :], sin_ref[pl.ds(r0, ROW_BLOCK), :]
        kn = _rope_pairs(_rms_head_pairs(k_ref[pl.ds(r0, ROW_BLOCK), :].astype(F32), kw), cos2, sin2)
        ks_scr[pl.ds(r0 + blk, ROW_BLOCK), :] = kn.astype(BF16)
        vs_scr[pl.ds(r0 + blk, ROW_BLOCK), :] = v_ref[pl.ds(r0, ROW_BLOCK), :]
        for pair in range(C_QHEADS // 2):
            qp = _rms_head_pairs(q_ref[pl.ds(r0, ROW_BLOCK), pair * LANES:(pair + 1) * LANES].astype(F32), qw)
            qp = (_rope_pairs(qp, cos2, sin2) * scale).astype(BF16)
            for half in range(2):
                hq = 2 * pair + half
                hk, g = hq // C_GROUP, hq % C_GROUP
                for sub in range(ROW_BLOCK // blk):
                    dst = pl.multiple_of((r * (ROW_BLOCK // blk) + sub) * grp_rows + g * blk, blk)
                    qs_scr[hk, pl.ds(dst, blk), :] = qp[sub * blk:(sub + 1) * blk, half * C_HD:(half + 1) * C_HD]
        return carry

    lax.fori_loop(0, nrb, prep, 0)

    nkeys = 3 * blk + past_len
    kcb = kc_ref[0, 0].astype(BF16)
    vcb = vc_ref[0, 0].astype(BF16)
    qi = lax.broadcasted_iota(jnp.int32, (grp_rows, nkeys), 0) & (blk - 1)
    kj = lax.broadcasted_iota(jnp.int32, (grp_rows, nkeys), 1)
    visible = jnp.logical_or(kj >= 3 * blk, jnp.logical_and(kj >= qi, kj <= qi + 2 * C_WINDOW))
    bias_scr[...] = jnp.where(visible, 0.0, NEG_BIG)
    head_of_row = lax.broadcasted_iota(jnp.int32, (grp_rows, 1), 0) // blk
    kcol = lax.broadcasted_iota(jnp.int32, (1, nkeys), 1)

    sinks = []
    for hk in range(C_KVHEADS):
        sink = jnp.full((grp_rows, 1), sink_ref[layer, hk * C_GROUP], F32)
        for g in range(1, C_GROUP):
            sink = jnp.where(head_of_row == g, sink_ref[layer, hk * C_GROUP + g], sink)
        sinks.append(sink * LOG2E)
    blocks_per_step = 2 if (t // blk) % 2 == 0 else 1

    def qblocks(step, carry):
        jobs, rows = [], []
        for sub in range(blocks_per_step):
            n = step * blocks_per_step + sub
            r0 = pl.multiple_of(n * blk, blk)
            rows.append(r0)
            kpos = kcol + (r0 - blk)
            in_seq = jnp.logical_or(kcol >= 3 * blk, jnp.logical_and(kpos >= 0, kpos < t))
            edge = jnp.where(in_seq, 0.0, NEG_BIG)
            kwin = ks_scr[pl.ds(r0, 3 * blk), :]
            vwin = vs_scr[pl.ds(r0, 3 * blk), :]
            for hk in range(C_KVHEADS):
                lo, hi = hk * C_HD, (hk + 1) * C_HD
                keys = jnp.concatenate([kwin[:, lo:hi], kcb[:, lo:hi]], axis=0)
                vals = jnp.concatenate([vwin[:, lo:hi], vcb[:, lo:hi]], axis=0)
                q_stack = qs_scr[hk, pl.ds(pl.multiple_of(n * grp_rows, grp_rows), grp_rows), :]
                jobs.append((sub, hk, _dot_nt(q_stack, keys), edge, vals))
        outs = [[] for _ in range(blocks_per_step)]
        for sub, hk, s, edge, vals in jobs:
            o = _softmax_sink_av([s + bias_scr[...] + edge], [vals], sinks[hk])
            outs[sub] += [o[g * blk:(g + 1) * blk] for g in range(C_GROUP)]
        for sub in range(blocks_per_step):
            for pair in range(C_QHEADS // 2):
                o_ref[pl.ds(rows[sub], blk), pair * LANES:(pair + 1) * LANES] = jnp.concatenate(
                    outs[sub][2 * pair:2 * pair + 2], axis=1).astype(o_ref.dtype)
        return carry

    lax.fori_loop(0, t // (blk * blocks_per_step), qblocks, 0)


def _attn_lat_call(proj, cache_k, cache_v, prm, cos2, sin2, layer, nseq, t):
    past_len = cache_k.shape[2]
    qw = C_QHEADS * C_HD
    return pl.pallas_call(
        functools.partial(_attn_lat_kernel, t=t, past_len=past_len, layer=layer),
        out_shape=jax.ShapeDtypeStruct((nseq * t, qw), BF16),
        grid=(nseq,),
        in_specs=[pl.BlockSpec((t, qw), lambda b: (b, COL_QC // qw)),
                  pl.BlockSpec((t, LANES), lambda b: (b, COL_KC // LANES)),
                  pl.BlockSpec((t, LANES), lambda b: (b, COL_VC // LANES)),
                  pl.BlockSpec((1, 1, past_len, LANES), lambda b: (b, layer, 0, 0)),
                  pl.BlockSpec((1, 1, past_len, LANES), lambda b: (b, layer, 0, 0)),
                  _layer_block((1, LANES), layer), _layer_block((1, LANES), layer),
                  _resident((t, LANES), lambda b: (0, 0)), _resident((t, LANES), lambda b: (0, 0)),
                  pl.BlockSpec(memory_space=pltpu.SMEM)],
        out_specs=pl.BlockSpec((t, qw), lambda b: (b, 0)),
        scratch_shapes=[pltpu.VMEM((C_KVHEADS, t * C_GROUP, C_HD), BF16),
                        pltpu.VMEM((t + 2 * C_BLOCK, LANES), BF16), pltpu.VMEM((t + 2 * C_BLOCK, LANES), BF16),
                        pltpu.VMEM((C_GROUP * C_BLOCK, 3 * C_BLOCK + past_len), F32)],
        compiler_params=_cparams(("arbitrary",)),
        name="attn_lat",
    )(proj, proj, proj, cache_k, cache_v, prm["q_norm2"], prm["k_norm2"], cos2, sin2, prm["sink"])


def _rope_tables(t):
    rows = t // GRID_W
    row = jnp.repeat(jnp.arange(rows, dtype=F32), GRID_W)
    col = jnp.tile(jnp.arange(GRID_W, dtype=F32), rows)
    nf = C_HD // 4
    inv = ROPE_THETA ** (-jnp.arange(nf, dtype=F32) / nf)
    ar, ac = row[:, None] * inv, col[:, None] * inv
    cos = jnp.concatenate([jnp.cos(ar), jnp.cos(ar), jnp.cos(ac), jnp.cos(ac)], axis=1)
    sin = jnp.concatenate([-jnp.sin(ar), jnp.sin(ar), -jnp.sin(ac), jnp.sin(ac)], axis=1)
    return jnp.tile(cos, (1, 2)), jnp.tile(sin, (1, 2))


def _merge_kernel(x_ref, g0_ref, g1_ref, g2_ref, oa_ref, ob_ref, oc_ref, mod_ref, wbr_ref, wout_ref, o_ref):
    merged = (_sigmoid(g0_ref[...].astype(F32)) * jnp.dot(oa_ref[...], wbr_ref[0], preferred_element_type=F32)
              + _sigmoid(g1_ref[...].astype(F32)) * jnp.dot(ob_ref[...], wbr_ref[1], preferred_element_type=F32)
              + _sigmoid(g2_ref[...].astype(F32)) * jnp.dot(oc_ref[...], wbr_ref[2], preferred_element_type=F32))
    res = jnp.dot(merged.astype(BF16), wout_ref[...], preferred_element_type=F32)
    o_ref[...] = x_ref[...] + mod_ref[0, 2:3, :] * res


def _merge_call(x2d, proj, oa, ob, oc, prm, layer, cond, tm):
    m, d = x2d.shape
    mg = lambda r: pl.BlockSpec((tm, d), lambda i: (i, COL_MG // d + r))
    br = pl.BlockSpec((tm, BRANCH_W), lambda i: (i, 0))
    return pl.pallas_call(
        _merge_kernel,
        out_shape=jax.ShapeDtypeStruct((m, d), F32),
        grid=(m // tm,),
        in_specs=[pl.BlockSpec((tm, d), lambda i: (i, 0)), mg(0), mg(1), mg(2), br, br, br,
                  _mod_block(d, layer, cond), _layer_block((3, BRANCH_W, d), layer), _layer_block((d, d), layer)],
        out_specs=pl.BlockSpec((tm, d), lambda i: (i, 0)),
        compiler_params=_cparams(("arbitrary",)),
        name="merge",
    )(x2d, proj, proj, proj, oa, ob, oc, prm["mod"], prm["w_branch"], prm["w_out"])


FF_CHUNK = 256
HALO = BF16_ROWS


def _ffn_kernel(x_ref, xp_ref, xn_ref, mod_ref, nw_ref, wup_ref, cw_ref, wd_ref, o_ref, h_scr, act_scr, *,
                tm, seq_len):
    i = pl.program_id(0)
    nseg = max(1, tm // seq_len)
    seg = tm // nseg
    nw, sh, sc = nw_ref[...], mod_ref[0, 3:4, :], mod_ref[0, 4:5, :]

    def norm(x):
        return _rms_rows(x, nw) * (1.0 + sc) + sh

    has_prev = ((i * tm) & (seq_len - 1)) != 0
    has_next = (((i + 1) * tm) & (seq_len - 1)) != 0
    zero_halo = jnp.zeros((HALO, x_ref.shape[1]), BF16)
    for s in range(nseg):
        h_scr[s, HALO:HALO + seg, :] = norm(x_ref[s * seg:(s + 1) * seg, :]).astype(BF16)
        if s == 0:
            h_scr[s, 0:HALO, :] = (norm(xp_ref[...]) * jnp.where(has_prev, 1.0, 0.0)).astype(BF16)
        else:
            h_scr[s, 0:HALO, :] = zero_halo
        if s == nseg - 1:
            h_scr[s, HALO + seg:2 * HALO + seg, :] = (norm(xn_ref[...]) * jnp.where(has_next, 1.0, 0.0)).astype(BF16)
        else:
            h_scr[s, HALO + seg:2 * HALO + seg, :] = zero_halo

    def conv(u, cw):
        return (u[HALO - 1:HALO - 1 + seg] * cw[0:1, :] + u[HALO:HALO + seg] * cw[1:2, :]
                + u[HALO + 1:HALO + 1 + seg] * cw[2:3, :])

    for lo, hi in _col_chunks(D_FF, FF_CHUNK):
        for s in range(nseg):
            h = h_scr[s]
            a = conv(jnp.dot(h, wup_ref[:, lo:hi], preferred_element_type=F32), cw_ref[:, lo:hi])
            u = conv(jnp.dot(h, wup_ref[:, D_FF + lo:D_FF + hi], preferred_element_type=F32),
                     cw_ref[:, D_FF + lo:D_FF + hi])
            act_scr[s * seg:(s + 1) * seg, lo:hi] = (_silu(a) * u).astype(BF16)

    o_ref[...] = x_ref[...] + mod_ref[0, 5:6, :] * jnp.dot(act_scr[...], wd_ref[...], preferred_element_type=F32)


def _ffn_call(x2d, prm, layer, cond, tm, seq_len):
    m, d = x2d.shape
    hb = tm // HALO
    last = m // HALO - 1
    nseg = max(1, tm // seq_len)
    seg = tm // nseg
    return pl.pallas_call(
        functools.partial(_ffn_kernel, tm=tm, seq_len=seq_len),
        out_shape=jax.ShapeDtypeStruct((m, d), F32),
        grid=(m // tm,),
        in_specs=[pl.BlockSpec((tm, d), lambda i: (i, 0)),
                  pl.BlockSpec((HALO, d), lambda i: (jnp.maximum(i * hb - 1, 0), 0)),
                  pl.BlockSpec((HALO, d), lambda i: (jnp.minimum((i + 1) * hb, last), 0)),
                  _mod_block(d, layer, cond), _layer_block((1, d), layer), _layer_block((d, 2 * D_FF), layer),
                  _layer_block((3, 2 * D_FF), layer), _layer_block((D_FF, d), layer)],
        out_specs=pl.BlockSpec((tm, d), lambda i: (i, 0)),
        scratch_shapes=[pltpu.VMEM((nseg, seg + 2 * HALO, d), BF16), pltpu.VMEM((tm, D_FF), BF16)],
        compiler_params=_cparams(("arbitrary",)),
        name="ffn",
    )(x2d, x2d, x2d, prm["mod"], prm["norm2_w"], prm["w_up"], prm["conv_ffn"], prm["w_down"])


def _permute_w_in(w):
    s = _SRC
    w16 = jnp.concatenate([w[..., s["mg"]:s["end"]], w[..., s["qa"]:s["beta"]], w[..., s["qb"]:s["fb"]],
                           w[..., s["gb"]:s["mg"]]], axis=-1)
    n_gate = s["qb"] - s["beta"]
    gates = jnp.pad(w[..., s["beta"]:s["qb"]], ((0, 0), (0, 0), (0, LANES - n_gate)))
    w32 = jnp.concatenate([w[..., s["fb"]:s["gb"]], gates], axis=-1)
    return w16.astype(BF16), w32.astype(BF16)


def _gate_rows(proj32, nseq, t, c):
    g = proj32[:, COL32_GATES:COL32_GATES + 4 * HEADS].reshape(nseq, t // c, c, 2, 2, HEADS)
    return jnp.transpose(g, (0, 5, 1, 3, 4, 2)).reshape(nseq, HEADS, t // c, 4, c)


def _row_tile(rows, t):
    tm = 512
    while rows % tm or (t % tm and tm % t):
        tm //= 2
    return tm


def _group_forward(x3d, first_cond, shared_cond, prm, past, tables):
    nseq, t, d = x3d.shape
    x = x3d.reshape(nseq * t, d)
    tm = _row_tile(nseq * t, t)
    tiles_per_seq = max(1, t // tm)
    cond = (lambda i: first_cond) if shared_cond else (lambda i: first_cond + i // tiles_per_seq)
    emit = past is None
    states_a, states_b, keys, vals = [], [], [], []
    for l in range(prm["w16"].shape[0]):
        proj, proj32 = _in_proj_call(x, prm, l, cond, tm)
        gates_r = _gate_rows(proj32, nseq, t, min(CHUNK_A, t))
        res_ab = _mixer_ab_call(proj, proj32, gates_r, prm, past, l, nseq, t, emit)
        if emit:
            oc, kn, vn = _attn_ctx_call(proj, prm, l, nseq, t)
            states_a.append(res_ab[2])
            states_b.append(res_ab[3])
            keys.append(kn.reshape(nseq, t, C_KVHEADS, C_HD))
            vals.append(vn.reshape(nseq, t, C_KVHEADS, C_HD))
        else:
            oc = _attn_lat_call(proj, past[2], past[3], prm, tables[0], tables[1], l, nseq, t)
        x = _merge_call(x, proj, res_ab[0], res_ab[1], oc, prm, l, cond, tm)
        x = _ffn_call(x, prm, l, cond, tm, t)
    return x.reshape(nseq, t, d), states_a, states_b, keys, vals


def kernel(x_prompt, x_sample, state_delta, state_hgrn, cache_k, cache_v, c, c_ctx, ada_w, ada_b, norm1_w, w_in, conv_a, a_log, dt_bias, norm_a, lb_logits, norm_b, q_norm, k_norm, sink, w_branch, w_out, norm2_w, w_up, conv_ffn, w_down):
    depth = w_in.shape[0]
    d = x_prompt.shape[-1]

    cond = jnp.concatenate([c_ctx[None, :], c], axis=0)
    rows = -(-cond.shape[0] // SUBLANES) * SUBLANES
    cond = jnp.pad(cond, ((0, rows - cond.shape[0]), (0, 0)))
    mod_all = _mod_call(cond, ada_w, ada_b).reshape(depth, rows, 6, d)

    w16, w32 = _permute_w_in(w_in)
    prm = dict(
        mod=mod_all, w16=w16, w32=w32,
        norm1_w=norm1_w.reshape(depth, 1, d), norm2_w=norm2_w.reshape(depth, 1, d),
        conv_a=conv_a, a_log=a_log, dt_bias=dt_bias, norm_a=norm_a.reshape(depth, 1, HEAD_W),
        lb_logits=lb_logits, norm_b=norm_b.reshape(depth, 1, HEAD_W),
        q_norm2=jnp.tile(q_norm, (1, 2)).reshape(depth, 1, LANES), k_norm2=jnp.tile(k_norm, (1, 2)).reshape(depth, 1, LANES),
        sink=sink, w_branch=w_branch.astype(BF16), w_out=w_out.astype(BF16),
        w_up=w_up.astype(BF16), conv_ffn=conv_ffn, w_down=w_down.astype(BF16))

    y_prompt, st_a, st_b, keys, vals = _group_forward(x_prompt, 0, True, prm, None, None)

    past_len = cache_k.shape[2]
    past = (state_delta, state_hgrn,
            cache_k.reshape(cache_k.shape[0], depth, past_len, C_KVHEADS * C_HD),
            cache_v.reshape(cache_v.shape[0], depth, past_len, C_KVHEADS * C_HD))
    y_sample, _, _, _, _ = _group_forward(x_sample, 1, False, prm, past, _rope_tables(x_sample.shape[1]))

    return (y_prompt, y_sample, jnp.stack(st_a, axis=1), jnp.stack(st_b, axis=1),
            jnp.stack(keys, axis=1), jnp.stack(vals, axis=1))
```

```python
import functools
import math

import numpy as np
import jax
import jax.numpy as jnp
from jax import lax
from jax.experimental import pallas as pl
from jax.experimental.pallas import tpu as pltpu

F32 = jnp.float32
BF16 = jnp.bfloat16

D_MODEL = 1024
NORM_EPS = 1e-6
LB_EPS = 1e-6
NEG_BIG = -1e30
LOG2E = 1.4426950408889634
GRID_W = 64
ROPE_THETA = 10000.0

HEADS = 4
HEAD_W = 128
SHORT_CONV = 5
C_QHEADS = 8
C_KVHEADS = 2
C_GROUP = C_QHEADS // C_KVHEADS
C_HD = 64
C_WINDOW = 128
C_BLOCK = 128
BRANCH_W = 512
D_FF = 2816

LANES = 128
SUBLANES = 8
BF16_ROWS = 16
VMEM_LIMIT = 56 * 1024 * 1024

COL_MG = 0
COL_QA = 3072
COL_KA = 3584
COL_VA = 4096
COL_GA = 4608
COL_QB = 5120
COL_IB = 5632
COL_GB = 6144
COL_QC = 6656
COL_KC = 7168
COL_VC = 7296
PROJ_W = 7424
COL32_FB = 0
COL32_GATES = 1024
PROJ32_W = 1152
_SRC = dict(qa=0, ka=512, va=1024, ga=1536, beta=2048, alpha=2056, qb=2064, ib=2576, fb=3088, gb=4112,
            qc=4624, kc=5136, vc=5264, mg=5392, end=8464)

CHUNK_A = 128
ROUNDS_A = 8
CHUNK_B = 64
TRI_BASE = 16
TRI_MERGE = 2
MM_TILE = 768


def _cparams(sem):
    return pltpu.CompilerParams(dimension_semantics=sem, vmem_limit_bytes=VMEM_LIMIT)


def _resident(shape, index_map):
    return pl.BlockSpec(shape, index_map, pipeline_mode=pl.Buffered(1))


def _layer_block(shape, layer):
    return _resident((None,) + tuple(shape), lambda *_: (layer,) + (0,) * len(shape))


def _mod_block(d, layer, cond):
    return pl.BlockSpec((None, 1, 6, d), lambda i: (layer, cond(i), 0, 0))


def _dot(a, b):
    return jnp.dot(a.astype(BF16), b.astype(BF16), preferred_element_type=F32)


def _dot_nt(a, b):
    return lax.dot_general(a.astype(BF16), b.astype(BF16), (((1,), (1,)), ((), ())), preferred_element_type=F32)


def _dot_tn(a, b):
    return lax.dot_general(a.astype(BF16), b.astype(BF16), (((0,), (0,)), ((), ())), preferred_element_type=F32)


def _sigmoid(x):
    return 1.0 / (1.0 + jnp.exp(-x))


def _silu(x):
    return x * _sigmoid(x)


def _softplus(x):
    return jnp.maximum(x, 0.0) + jnp.log(1.0 + jnp.exp(-jnp.abs(x)))


def _rms_rows(x, w):
    ms = jnp.mean(x * x, axis=-1, keepdims=True)
    return x * lax.rsqrt(ms + NORM_EPS) * w


def _split_bf16(x, n):
    parts, r = [], x
    for _ in range(n):
        p = r.astype(BF16)
        parts.append(p)
        r = r - p.astype(F32)
    return parts


def _col_chunks(width, step):
    return [(lo, min(lo + step, width)) for lo in range(0, width, step)]


def _mod_kernel(c_ref, w_ref, b_ref, o_ref):
    c = c_ref[...]
    o_ref[0] = _dot(_silu(c), w_ref[0]) + b_ref[0]


def _mod_call(cond, ada_w, ada_b):
    depth, d, n = ada_w.shape
    rows = cond.shape[0]
    tn = 768
    return pl.pallas_call(
        _mod_kernel,
        out_shape=jax.ShapeDtypeStruct((depth, rows, n), F32),
        grid=(depth, n // tn),
        in_specs=[pl.BlockSpec((rows, d), lambda l, j: (0, 0)),
                  pl.BlockSpec((1, d, tn), lambda l, j: (l, 0, j)),
                  pl.BlockSpec((1, 1, tn), lambda l, j: (l, 0, j))],
        out_specs=pl.BlockSpec((1, rows, tn), lambda l, j: (l, 0, j)),
        compiler_params=_cparams(("arbitrary", "arbitrary")),
        name="mod",
    )(cond, ada_w, ada_b.reshape(depth, 1, n))


def _in_proj_kernel(x_ref, mod_ref, nw_ref, w16_ref, w32_ref, o16_ref, o32_ref):
    h = _rms_rows(x_ref[...], nw_ref[...]) * (1.0 + mod_ref[0, 1:2, :]) + mod_ref[0, 0:1, :]
    hb = h.astype(BF16)
    for lo, hi in _col_chunks(PROJ_W, MM_TILE):
        o16_ref[:, lo:hi] = jnp.dot(hb, w16_ref[:, lo:hi], preferred_element_type=F32).astype(BF16)
    for lo, hi in _col_chunks(PROJ32_W, MM_TILE):
        o32_ref[:, lo:hi] = jnp.dot(hb, w32_ref[:, lo:hi], preferred_element_type=F32)


def _in_proj_call(x2d, prm, layer, cond, tm):
    m, d = x2d.shape
    return pl.pallas_call(
        _in_proj_kernel,
        out_shape=(jax.ShapeDtypeStruct((m, PROJ_W), BF16), jax.ShapeDtypeStruct((m, PROJ32_W), F32)),
        grid=(m // tm,),
        in_specs=[pl.BlockSpec((tm, d), lambda i: (i, 0)), _mod_block(d, layer, cond),
                  _layer_block((1, d), layer), _layer_block((d, PROJ_W), layer), _layer_block((d, PROJ32_W), layer)],
        out_specs=(pl.BlockSpec((tm, PROJ_W), lambda i: (i, 0)),
                   pl.BlockSpec((tm, PROJ32_W), lambda i: (i, 0))),
        compiler_params=_cparams(("arbitrary",)),
        name="in_proj",
    )(x2d, prm["mod"], prm["norm1_w"], prm["w16"], prm["w32"])


ROW_BLOCK = 256


def _gated_norm_epilogue(of_scr, ob_scr, gate_ref, nw_ref, o_ref, t, cols):
    nw = nw_ref[...]

    def body(r, carry):
        r0 = pl.multiple_of(r * ROW_BLOCK, ROW_BLOCK)
        o = of_scr[pl.ds(r0, ROW_BLOCK), :] + ob_scr[pl.ds(r0, ROW_BLOCK), :]
        y = _rms_rows(o, nw) * _silu(gate_ref[pl.ds(r0, ROW_BLOCK), cols].astype(F32))
        o_ref[pl.ds(r0, ROW_BLOCK), cols] = y.astype(o_ref.dtype)
        return carry

    lax.fori_loop(0, t // ROW_BLOCK, body, 0)


def _tri_inverse(mats, ri, ci, c, tick):
    n = range(len(mats))
    shift = int(np.log2(TRI_BASE))
    base = (ri >> shift) == (ci >> shift)
    mb = [jnp.where(base, m, 0.0) for m in mats]
    y = [-m for m in mb]
    p = [_dot(m, m) for m in mb]
    tick()
    for _ in range(shift - 2):
        yp = [_dot(jnp.concatenate([y[i], p[i]], axis=0), p[i]) for i in n]
        tick()
        y = [y[i] + p[i] + yp[i][:c] for i in n]
        p = [r[c:] for r in yp]
    yp = [_dot(y[i], p[i]) for i in n]
    tick()
    y = [y[i] + p[i] + yp[i] for i in n]
    total = int(np.log2(c))
    while shift < total:
        factors = min(TRI_MERGE, total - shift)
        inner = (ri >> shift) == (ci >> shift)
        outer = (ri >> (shift + factors)) == (ci >> (shift + factors))
        between = jnp.logical_and(outer, jnp.logical_not(inner))
        cm = [jnp.where(between, m, 0.0) for m in mats]
        w = [_dot(y[i], cm[i]) for i in n]
        tick()
        w = [cm[i] + w[i] for i in n]
        if factors == 1:
            wy = [_dot(w[i], y[i]) for i in n]
            tick()
            y = [y[i] - w[i] - wy[i] for i in n]
        else:
            r = [_dot(w[i], jnp.concatenate([w[i], y[i]], axis=1)) for i in n]
            tick()
            p, y = [x[:, :c] for x in r], [y[i] - w[i] - r[i][:, c:] for i in n]
            for f in range(1, factors):
                if f + 1 < factors:
                    r = [_dot(p[i], jnp.concatenate([p[i], y[i]], axis=1)) for i in n]
                    tick()
                    p, y = [x[:, :c] for x in r], [y[i] + p[i] + r[i][:, c:] for i in n]
                else:
                    py = [_dot(p[i], y[i]) for i in n]
                    tick()
                    y = [y[i] + p[i] + py[i] for i in n]
        shift += factors
    eye = jnp.where(ri == ci, 1.0, 0.0)
    return [eye + v for v in y]


def _delta_chunks(probs, states, c, tick):
    n = range(len(probs))
    ri = lax.broadcasted_iota(jnp.int32, (c, c), 0)
    ci = lax.broadcasted_iota(jnp.int32, (c, c), 1)
    eye = ri == ci
    pre = []
    for q, k, v, beta_row, alpha_row, a_neg, dt_b, chain, reverse in probs:
        incl, strict = (ri <= ci, ri < ci) if reverse else (ri >= ci, ri > ci)
        beta_r = _sigmoid(beta_row)
        g_r = a_neg * _softplus(alpha_row + dt_b)
        gc_col = jnp.sum(jnp.where(incl, jnp.broadcast_to(g_r, (c, c)), 0.0), axis=1, keepdims=True)
        beta_col = jnp.sum(jnp.where(eye, jnp.broadcast_to(beta_r, (c, c)), 0.0), axis=1, keepdims=True)
        gc_row = jnp.sum(jnp.where(eye, jnp.broadcast_to(gc_col, (c, c)), 0.0), axis=0, keepdims=True)
        g_tot = jnp.sum(g_r, axis=1, keepdims=True)
        decay = jnp.where(incl, jnp.exp(jnp.where(incl, gc_col - gc_row, 0.0)), 0.0)
        pre.append((strict, gc_col, beta_col, g_tot, decay, jnp.exp(gc_col)))

    kb = [pr[1].astype(BF16) for pr in probs]
    kq = [_dot_nt(jnp.concatenate([kb[i], probs[i][0].astype(BF16)], axis=0), kb[i]) for i in n]
    tick()
    t_inv = _tri_inverse([jnp.where(pre[i][0], kq[i][:c] * pre[i][2] * pre[i][4], 0.0) for i in n], ri, ci, c, tick)
    uw = [_dot(t_inv[i], jnp.concatenate([probs[i][2] * pre[i][2], probs[i][1] * (pre[i][2] * pre[i][5])], axis=1))
          for i in n]
    tick()
    wq_lhs = [jnp.concatenate([uw[i][:, HEAD_W:], probs[i][0] * pre[i][5]], axis=0).astype(BF16) for i in n]
    os_lhs = [jnp.concatenate([kq[i][c:] * pre[i][4], (probs[i][1] * jnp.exp(pre[i][3] - pre[i][1])).T],
                              axis=0).astype(BF16) for i in n]
    states = list(states)
    outs = [None] * len(probs)
    todo = list(n)
    while todo:
        front, seen = [], set()
        for i in todo:
            if probs[i][7] not in seen:
                seen.add(probs[i][7])
                front.append(i)
        todo = [i for i in todo if i not in front]
        wq = [_dot(wq_lhs[i], states[probs[i][7]]) for i in front]
        tick()
        v_new = [uw[i][:, :HEAD_W] - wq[j][:c] for j, i in enumerate(front)]
        os_ = [_dot(os_lhs[i], v_new[j]) for j, i in enumerate(front)]
        tick()
        for j, i in enumerate(front):
            outs[i] = wq[j][c:] + os_[j][:c]
            states[probs[i][7]] = states[probs[i][7]] * jnp.exp(pre[i][3]) + os_[j][c:]
    return outs, states


def _conv_silu_pass(x_ref, cw_ref, xp_scr, dst_scr, t, l2, scale, cols):
    nrb = t // ROW_BLOCK
    pad = SUBLANES
    half = SHORT_CONV // 2

    def cp(r, carry):
        r0 = pl.multiple_of(r * ROW_BLOCK, ROW_BLOCK)
        xp_scr[pl.ds(r0 + pad, ROW_BLOCK), :] = x_ref[pl.ds(r0, ROW_BLOCK), cols].astype(F32)
        return carry

    lax.fori_loop(0, nrb, cp, 0)
    cw = cw_ref[:, cols]

    def body(r, carry):
        r0 = pl.multiple_of(r * ROW_BLOCK, ROW_BLOCK)
        y = None
        for j in range(SHORT_CONV):
            term = xp_scr[pl.ds(r0 + (pad - half + j), ROW_BLOCK), :] * cw[j:j + 1, :]
            y = term if y is None else y + term
        y = _silu(y)
        if l2:
            y = y * (lax.rsqrt(jnp.sum(y * y, axis=-1, keepdims=True) + NORM_EPS) * scale)
        dst_scr[pl.ds(r0, ROW_BLOCK), :] = y
        return carry

    lax.fori_loop(0, nrb, body, 0)


def _hgrn_diag(q, kf, v, b, reverse, c):
    nb = c // SUBLANES
    q3, k3, v3, b3 = (a.reshape(nb, SUBLANES, HEAD_W) for a in (q, kf, v, b))
    sub = lax.broadcasted_iota(jnp.int32, (nb, SUBLANES, HEAD_W), 1)
    o3 = jnp.zeros((nb, SUBLANES, HEAD_W), F32)
    for j in range(SUBLANES):
        mask = (sub <= j) if reverse else (sub >= j)
        e = jnp.exp2(jnp.where(mask, b3 - b3[:, j:j + 1, :], NEG_BIG))
        a = jnp.sum(q3 * e * k3[:, j:j + 1, :], axis=-1, keepdims=True)
        o3 = o3 + a * v3[:, j:j + 1, :]
    return o3.reshape(c, HEAD_W)


def _hgrn_chunk(q, kf, v, lf, st, reverse, c):
    ri = lax.broadcasted_iota(jnp.int32, (c, c), 0)
    ci = lax.broadcasted_iota(jnp.int32, (c, c), 1)
    incl = (ri <= ci) if reverse else (ri >= ci)
    parts = jnp.concatenate(_split_bf16(lf, 3), axis=1)
    b3 = jnp.dot(jnp.where(incl, 1.0, 0.0).astype(BF16), parts, preferred_element_type=F32)
    b = b3[:, :HEAD_W] + b3[:, HEAD_W:2 * HEAD_W] + b3[:, 2 * HEAD_W:]
    b_tot = jnp.sum(lf, axis=0, keepdims=True)
    o = _dot_nt(q * jnp.exp2(b), st)
    row = lax.broadcasted_iota(jnp.int32, (c, 1), 0)
    att = jnp.zeros((c, c), F32)
    n = SUBLANES
    while n < c:
        pieces = []
        for g in range(c // (2 * n)):
            r = g * 2 * n + (n if reverse else n - 1)
            pieces.append(jnp.broadcast_to(b[r:r + 1, :], (2 * n, HEAD_W)))
        ref = pieces[0] if len(pieces) == 1 else jnp.concatenate(pieces, axis=0)
        e = jnp.exp2(-jnp.abs(b - ref))
        s = int(np.log2(n))
        second = ((row >> s) & 1) == 1
        q_part = jnp.logical_not(second) if reverse else second
        qt = jnp.where(q_part, q * e, 0.0)
        kt = jnp.where(q_part, 0.0, kf * e)
        att = att + jnp.where((ri >> (s + 1)) == (ci >> (s + 1)), _dot_nt(qt, kt), 0.0)
        n *= 2
    o = o + _dot(att, v) + _hgrn_diag(q, kf, v, b, reverse, c)
    kh = kf * jnp.exp2(b_tot - b)
    st_new = st * jnp.exp2(b_tot) + _dot_tn(v, kh)
    return o, st_new


def _mixer_ab_kernel(*refs, t, ca, cb, layer, hps, has_past, emit_state):
    refs = list(refs)
    (qa_ref, ka_ref, va_ref, ga_ref, gr_ref, cwq_ref, cwk_ref, cwv_ref, alog_ref, dtb_ref, nwa_ref,
     qb_ref, ib_ref, f0_ref, f1_ref, gb_ref, lb_ref, nwb_ref) = refs[:18]
    pos = 18
    sa0_ref = sb0_ref = None
    if has_past:
        sa0_ref, sb0_ref = refs[pos:pos + 2]
        pos += 2
    oa_ref, ob_ref = refs[pos:pos + 2]
    pos += 2
    sfa_ref = sfb_ref = None
    if emit_state:
        sfa_ref, sfb_ref = refs[pos:pos + 2]
        pos += 2
    xp_scr, qn_scr, kn_scr, vn_scr, af_scr, ab_scr, bf_scr, bb_scr, sa_scr, sb_scr = refs[pos:]

    heads = range(hps)
    cols = [slice(hh * HEAD_W, (hh + 1) * HEAD_W) for hh in heads]
    first_head = pl.program_id(1) * hps
    pad = SUBLANES
    xp_scr[0:pad, :] = jnp.zeros((pad, HEAD_W), F32)
    xp_scr[t + pad:t + 2 * pad, :] = jnp.zeros((pad, HEAD_W), F32)
    for hh in heads:
        _conv_silu_pass(qa_ref, cwq_ref, xp_scr, qn_scr.at[hh], t, True, HEAD_W ** -0.5, cols[hh])
        _conv_silu_pass(ka_ref, cwk_ref, xp_scr, kn_scr.at[hh], t, True, 1.0, cols[hh])
        _conv_silu_pass(va_ref, cwv_ref, xp_scr, vn_scr.at[hh], t, False, 1.0, cols[hh])

    if has_past:
        for hh in heads:
            for d in range(2):
                sa_scr[2 * hh + d] = sa0_ref[0, 0, d, hh]
                sb_scr[2 * hh + d] = sb0_ref[0, 0, d, hh].T
    else:
        sa_scr[...] = jnp.zeros(sa_scr.shape, F32)
        sb_scr[...] = jnp.zeros(sb_scr.shape, F32)

    ones = jnp.ones((1, ca), F32)
    a_neg = [[-jnp.exp(ones * alog_ref[layer, d, first_head + hh]) for d in range(2)] for hh in heads]
    dt_b = [[dtb_ref[layer, d, first_head + hh] for d in range(2)] for hh in heads]

    lb_terms = None
    if layer > 0:
        lb_terms = []
        for hh in heads:
            terms = []
            for d in range(2):
                lg = lb_ref[d][:, cols[hh]]
                ex = jnp.exp(lg - jnp.max(lg, axis=0, keepdims=True))
                pr = ex / jnp.sum(ex, axis=0, keepdims=True)
                lb = jnp.clip(jnp.sum(pr[1:layer + 1], axis=0, keepdims=True), LB_EPS, 1.0 - LB_EPS)
                terms.append((jnp.log(lb) * LOG2E, jnp.log1p(-lb) * LOG2E, 1.0 - lb))
            lb_terms.append(terms)

    f_refs = (f0_ref, f1_ref)
    nca, ncb = t // ca, t // cb
    rounds = math.gcd(ROUNDS_A // hps, nca)
    ratio = rounds * ca // cb

    def hgrn_gates(z, hh, d):
        z2 = z * LOG2E
        y = jnp.exp2(-jnp.abs(z2))
        one_y = 1.0 + y
        l2_sig = jnp.minimum(z2, 0.0) - jnp.log(one_y) * LOG2E
        sig_neg = jnp.where(z >= 0.0, y, 1.0) / one_y
        if layer == 0:
            return l2_sig, sig_neg
        l2_lb, l2_1m_lb, one_m_lb = lb_terms[hh][d]
        a2 = l2_1m_lb + l2_sig
        return (jnp.maximum(l2_lb, a2) + jnp.log(1.0 + jnp.exp2(-jnp.abs(l2_lb - a2))) * LOG2E,
                one_m_lb * sig_neg)

    def step(n, carry):
        a_dst, a_probs = [], []
        for rnd in range(rounds):
            for hh in heads:
                for d in range(2):
                    cidx = n * rounds + rnd if d == 0 else nca - 1 - (n * rounds + rnd)
                    r0 = pl.multiple_of(cidx * ca, ca)
                    gr = gr_ref[0, hh, cidx]
                    a_dst.append((af_scr if d == 0 else ab_scr, hh, r0))
                    a_probs.append((qn_scr[hh, pl.ds(r0, ca), :], kn_scr[hh, pl.ds(r0, ca), :],
                                    vn_scr[hh, pl.ds(r0, ca), :], gr[d:d + 1, :], gr[2 + d:3 + d, :],
                                    a_neg[hh][d], dt_b[hh][d], 2 * hh + d, d == 1))
        sa = [sa_scr[chain] for chain in range(2 * hps)]
        b_jobs = []
        for j in range(ratio):
            m = n * ratio + j
            for hh in heads:
                for d in range(2):
                    r0 = pl.multiple_of((m if d == 0 else ncb - 1 - m) * cb, cb)
                    b_jobs.append((hh, d, r0, f_refs[d][pl.ds(r0, cb), cols[hh]], qb_ref[pl.ds(r0, cb), cols[hh]],
                                   ib_ref[pl.ds(r0, cb), cols[hh]]))
        sb = [sb_scr[chain] for chain in range(2 * hps)]

        b_out = []
        pending = list(b_jobs)

        def tick():
            if pending:
                hh, d, _, z, qraw, iraw = pending.pop(0)
                lf, kf = hgrn_gates(z, hh, d)
                o, sb[2 * hh + d] = _hgrn_chunk(_silu(qraw.astype(F32)), kf, iraw.astype(F32), lf, sb[2 * hh + d],
                                                d == 1, cb)
                b_out.append(o)

        a_out, sa = _delta_chunks(a_probs, sa, ca, tick)
        while pending:
            tick()

        for (dst, hh, r0), o in zip(a_dst, a_out):
            dst[hh, pl.ds(r0, ca), :] = o
        for (hh, d, r0, _, _, _), o in zip(b_jobs, b_out):
            (bf_scr if d == 0 else bb_scr)[hh, pl.ds(r0, cb), :] = o
        for chain in range(2 * hps):
            sa_scr[chain] = sa[chain]
            sb_scr[chain] = sb[chain]
        return carry

    lax.fori_loop(0, nca // rounds, step, 0)

    for hh in heads:
        _gated_norm_epilogue(af_scr.at[hh], ab_scr.at[hh], ga_ref, nwa_ref, oa_ref, t, cols[hh])
        _gated_norm_epilogue(bf_scr.at[hh], bb_scr.at[hh], gb_ref, nwb_ref, ob_ref, t, cols[hh])
    if emit_state:
        for hh in heads:
            for d in range(2):
                sfa_ref[0, d, hh] = sa_scr[2 * hh + d]
                sfb_ref[0, d, hh] = sb_scr[2 * hh + d].T


def _mixer_ab_call(proj, proj32, gates_r, prm, past, layer, nseq, t, emit_state):
    ca, cb = min(CHUNK_A, t), min(CHUNK_B, t)
    depth = prm["lb_logits"].shape[1]
    per_head = t * HEAD_W * (7 * 4 + 2 * (7 * 2 + 2 * 4) + 2 * 2 * 2)
    hps = max(n for n in (1, 2, 4) if n == 1 or n * per_head <= VMEM_LIMIT // 2)
    slab = hps * HEAD_W
    col = lambda off: (lambda b, h: (b, off // slab + h))
    seq_in = lambda off: pl.BlockSpec((t, slab), col(off))
    conv = lambda part: pl.BlockSpec((None, SHORT_CONV, slab), lambda b, h: (layer, 0, part * (HEADS // hps) + h))
    smem = pl.BlockSpec(memory_space=pltpu.SMEM)
    norm = _layer_block((1, HEAD_W), layer)
    in_specs = [seq_in(COL_QA), seq_in(COL_KA), seq_in(COL_VA), seq_in(COL_GA),
                pl.BlockSpec((1, hps, t // ca, 4, ca), lambda b, h: (b, h, 0, 0, 0)),
                conv(0), conv(1), conv(2), smem, smem, norm,
                seq_in(COL_QB), seq_in(COL_IB), seq_in(COL32_FB), seq_in(COL32_FB + HEADS * HEAD_W), seq_in(COL_GB),
                pl.BlockSpec((2, depth, slab), lambda b, h: (0, 0, h)), norm]
    args = [proj, proj, proj, proj, gates_r, prm["conv_a"], prm["conv_a"], prm["conv_a"],
            prm["a_log"], prm["dt_bias"], prm["norm_a"],
            proj, proj, proj32, proj32, proj, prm["lb_logits"], prm["norm_b"]]
    state_in = pl.BlockSpec((1, 1, 2, hps, HEAD_W, HEAD_W), lambda b, h: (b, layer, 0, h, 0, 0))
    if past is not None:
        in_specs += [state_in, state_in]
        args += [past[0], past[1]]
    o_shape = jax.ShapeDtypeStruct((nseq * t, HEADS * HEAD_W), BF16)
    o_spec = pl.BlockSpec((t, slab), lambda b, h: (b, h))
    out_shape, out_specs = [o_shape, o_shape], [o_spec, o_spec]
    if emit_state:
        s_shape = jax.ShapeDtypeStruct((nseq, 2, HEADS, HEAD_W, HEAD_W), F32)
        s_spec = pl.BlockSpec((1, 2, hps, HEAD_W, HEAD_W), lambda b, h: (b, 0, h, 0, 0))
        out_shape += [s_shape, s_shape]
        out_specs += [s_spec, s_spec]
    seq = pltpu.VMEM((hps, t, HEAD_W), F32)
    state = pltpu.VMEM((2 * hps, HEAD_W, HEAD_W), F32)
    return pl.pallas_call(
        functools.partial(_mixer_ab_kernel, t=t, ca=ca, cb=cb, layer=layer, hps=hps, has_past=past is not None,
                          emit_state=emit_state),
        out_shape=tuple(out_shape), grid=(nseq, HEADS // hps), in_specs=in_specs, out_specs=tuple(out_specs),
        scratch_shapes=[pltpu.VMEM((t + 2 * SUBLANES, HEAD_W), F32), seq, seq, seq, seq, seq, seq, seq, state, state],
        compiler_params=_cparams(("arbitrary", "arbitrary")),
        name="mixer_ab",
    )(*args)


def _rms_head_pairs(x, w2):
    lane = lax.broadcasted_iota(jnp.int32, x.shape, 1)
    left = lane < C_HD
    sq = x * x
    s0 = jnp.sum(jnp.where(left, sq, 0.0), axis=-1, keepdims=True)
    s1 = jnp.sum(jnp.where(left, 0.0, sq), axis=-1, keepdims=True)
    ms = jnp.where(left, s0, s1) * (1.0 / C_HD)
    return x * lax.rsqrt(ms + NORM_EPS) * w2


def _rope_pairs(x, cos2, sin2):
    lane = lax.broadcasted_iota(jnp.int32, x.shape, 1)
    quarter = C_HD // 4
    swapped = jnp.where((lane & (2 * quarter - 1)) < quarter,
                        pltpu.roll(x, LANES - quarter, axis=1), pltpu.roll(x, quarter, axis=1))
    return x * cos2 + swapped * sin2


def _softmax_sink_av(scores, values, sink):
    m = sink
    for s in scores:
        m = jnp.maximum(m, jnp.max(s, axis=-1, keepdims=True))
    den = jnp.exp2(sink - m)
    acc = None
    for s, v in zip(scores, values):
        p = jnp.exp2(s - m)
        den = den + jnp.sum(p, axis=-1, keepdims=True)
        t = jnp.dot(p.astype(BF16), v, preferred_element_type=F32)
        acc = t if acc is None else acc + t
    return acc / den


def _attn_ctx_kernel(q_ref, k_ref, v_ref, qn_ref, kn_ref, sink_ref, o_ref, ko_ref, vo_ref, *, t, layer):
    qw, kw = qn_ref[...], kn_ref[...]
    kn = _rms_head_pairs(k_ref[...].astype(F32), kw)
    ko_ref[0] = kn
    v = v_ref[...]
    vo_ref[0] = v.astype(F32)
    knb, vb = kn.astype(BF16), v
    scale = C_HD ** -0.5 * LOG2E
    heads = []
    for pair in range(C_QHEADS // 2):
        qp = (_rms_head_pairs(q_ref[:, pair * LANES:(pair + 1) * LANES].astype(F32), qw) * scale).astype(BF16)
        heads += [qp[:, :C_HD], qp[:, C_HD:]]
    head_of_row = lax.broadcasted_iota(jnp.int32, (C_GROUP * t, 1), 0) // t
    scores, sinks = [], []
    for hk in range(C_KVHEADS):
        q_stack = jnp.concatenate(heads[hk * C_GROUP:(hk + 1) * C_GROUP], axis=0)
        scores.append(_dot_nt(q_stack, knb[:, hk * C_HD:(hk + 1) * C_HD]))
        sink = jnp.full((C_GROUP * t, 1), sink_ref[layer, hk * C_GROUP], F32)
        for g in range(1, C_GROUP):
            sink = jnp.where(head_of_row == g, sink_ref[layer, hk * C_GROUP + g], sink)
        sinks.append(sink * LOG2E)
    outs = []
    for hk in range(C_KVHEADS):
        o = _softmax_sink_av([scores[hk]], [vb[:, hk * C_HD:(hk + 1) * C_HD]], sinks[hk])
        outs += [o[g * t:(g + 1) * t] for g in range(C_GROUP)]
    for pair in range(C_QHEADS // 2):
        o_ref[:, pair * LANES:(pair + 1) * LANES] = jnp.concatenate(outs[2 * pair:2 * pair + 2],
                                                                    axis=1).astype(o_ref.dtype)


def _attn_ctx_call(proj, prm, layer, nseq, t):
    return pl.pallas_call(
        functools.partial(_attn_ctx_kernel, t=t, layer=layer),
        out_shape=(jax.ShapeDtypeStruct((nseq * t, C_QHEADS * C_HD), BF16),
                   jax.ShapeDtypeStruct((nseq, t, LANES), F32), jax.ShapeDtypeStruct((nseq, t, LANES), F32)),
        grid=(nseq,),
        in_specs=[pl.BlockSpec((t, C_QHEADS * C_HD), lambda b: (b, COL_QC // (C_QHEADS * C_HD))),
                  pl.BlockSpec((t, LANES), lambda b: (b, COL_KC // LANES)),
                  pl.BlockSpec((t, LANES), lambda b: (b, COL_VC // LANES)),
                  _layer_block((1, LANES), layer), _layer_block((1, LANES), layer),
                  pl.BlockSpec(memory_space=pltpu.SMEM)],
        out_specs=(pl.BlockSpec((t, C_QHEADS * C_HD), lambda b: (b, 0)),
                   pl.BlockSpec((1, t, LANES), lambda b: (b, 0, 0)), pl.BlockSpec((1, t, LANES), lambda b: (b, 0, 0))),
        compiler_params=_cparams(("arbitrary",)),
        name="attn_ctx",
    )(proj, proj, proj, prm["q_norm2"], prm["k_norm2"], prm["sink"])


def _attn_lat_kernel(q_ref, k_ref, v_ref, kc_ref, vc_ref, qn_ref, kn_ref, cos_ref, sin_ref, sink_ref, o_ref,
                     qs_scr, ks_scr, vs_scr, bias_scr, *, t, past_len, layer):
    qw, kw = qn_ref[...], kn_ref[...]
    scale = C_HD ** -0.5 * LOG2E
    nrb = t // ROW_BLOCK
    blk = C_BLOCK

    ks_scr[0:blk, :] = jnp.zeros((blk, LANES), BF16)
    vs_scr[0:blk, :] = jnp.zeros((blk, LANES), BF16)
    ks_scr[t + blk:t + 2 * blk, :] = jnp.zeros((blk, LANES), BF16)
    vs_scr[t + blk:t + 2 * blk, :] = jnp.zeros((blk, LANES), BF16)

    grp_rows = C_GROUP * blk

    def prep(r, carry):
        r0 = pl.multiple_of(r * ROW_BLOCK, ROW_BLOCK)
        cos2, sin2 = cos_ref[pl.ds(r0, ROW_BLOCK), :], sin_ref[pl.ds(r0, ROW_BLOCK), :]
        kn = _rope_pairs(_rms_head_pairs(k_ref[pl.ds(r0, ROW_BLOCK), :].astype(F32), kw), cos2, sin2)
        ks_scr[pl.ds(r0 + blk, ROW_BLOCK), :] = kn.astype(BF16)
        vs_scr[pl.ds(r0 + blk, ROW_BLOCK), :] = v_ref[pl.ds(r0, ROW_BLOCK), :]
        for pair in range(C_QHEADS // 2):
            qp = _rms_head_pairs(q_ref[pl.ds(r0, ROW_BLOCK), pair * LANES:(pair + 1) * LANES].astype(F32), qw)
            qp = (_rope_pairs(qp, cos2, sin2) * scale).astype(BF16)
            for half in range(2):
                hq = 2 * pair + half
                hk, g = hq // C_GROUP, hq % C_GROUP
                for sub in range(ROW_BLOCK // blk):
                    dst = pl.multiple_of((r * (ROW_BLOCK // blk) + sub) * grp_rows + g * blk, blk)
                    qs_scr[hk, pl.ds(dst, blk), :] = qp[sub * blk:(sub + 1) * blk, half * C_HD:(half + 1) * C_HD]
        return carry

    lax.fori_loop(0, nrb, prep, 0)

    nkeys = 3 * blk + past_len
    kcb = kc_ref[0, 0].astype(BF16)
    vcb = vc_ref[0, 0].astype(BF16)
    qi = lax.broadcasted_iota(jnp.int32, (grp_rows, nkeys), 0) & (blk - 1)
    kj = lax.broadcasted_iota(jnp.int32, (grp_rows, nkeys), 1)
    visible = jnp.logical_or(kj >= 3 * blk, jnp.logical_and(kj >= qi, kj <= qi + 2 * C_WINDOW))
    bias_scr[...] = jnp.where(visible, 0.0, NEG_BIG)
    head_of_row = lax.broadcasted_iota(jnp.int32, (grp_rows, 1), 0) // blk
    kcol = lax.broadcasted_iota(jnp.int32, (1, nkeys), 1)

    sinks = []
    for hk in range(C_KVHEADS):
        sink = jnp.full((grp_rows, 1), sink_ref[layer, hk * C_GROUP], F32)
        for g in range(1, C_GROUP):
            sink = jnp.where(head_of_row == g, sink_ref[layer, hk * C_GROUP + g], sink)
        sinks.append(sink * LOG2E)
    blocks_per_step = 2 if (t // blk) % 2 == 0 else 1

    def qblocks(step, carry):
        jobs, rows = [], []
        for sub in range(blocks_per_step):
            n = step * blocks_per_step + sub
            r0 = pl.multiple_of(n * blk, blk)
            rows.append(r0)
            kpos = kcol + (r0 - blk)
            in_seq = jnp.logical_or(kcol >= 3 * blk, jnp.logical_and(kpos >= 0, kpos < t))
            edge = jnp.where(in_seq, 0.0, NEG_BIG)
            kwin = ks_scr[pl.ds(r0, 3 * blk), :]
            vwin = vs_scr[pl.ds(r0, 3 * blk), :]
            for hk in range(C_KVHEADS):
                lo, hi = hk * C_HD, (hk + 1) * C_HD
                keys = jnp.concatenate([kwin[:, lo:hi], kcb[:, lo:hi]], axis=0)
                vals = jnp.concatenate([vwin[:, lo:hi], vcb[:, lo:hi]], axis=0)
                q_stack = qs_scr[hk, pl.ds(pl.multiple_of(n * grp_rows, grp_rows), grp_rows), :]
                jobs.append((sub, hk, _dot_nt(q_stack, keys), edge, vals))
        outs = [[] for _ in range(blocks_per_step)]
        for sub, hk, s, edge, vals in jobs:
            o = _softmax_sink_av([s + bias_scr[...] + edge], [vals], sinks[hk])
            outs[sub] += [o[g * blk:(g + 1) * blk] for g in range(C_GROUP)]
        for sub in range(blocks_per_step):
            for pair in range(C_QHEADS // 2):
                o_ref[pl.ds(rows[sub], blk), pair * LANES:(pair + 1) * LANES] = jnp.concatenate(
                    outs[sub][2 * pair:2 * pair + 2], axis=1).astype(o_ref.dtype)
        return carry

    lax.fori_loop(0, t // (blk * blocks_per_step), qblocks, 0)


def _attn_lat_call(proj, cache_k, cache_v, prm, cos2, sin2, layer, nseq, t):
    past_len = cache_k.shape[2]
    qw = C_QHEADS * C_HD
    return pl.pallas_call(
        functools.partial(_attn_lat_kernel, t=t, past_len=past_len, layer=layer),
        out_shape=jax.ShapeDtypeStruct((nseq * t, qw), BF16),
        grid=(nseq,),
        in_specs=[pl.BlockSpec((t, qw), lambda b: (b, COL_QC // qw)),
                  pl.BlockSpec((t, LANES), lambda b: (b, COL_KC // LANES)),
                  pl.BlockSpec((t, LANES), lambda b: (b, COL_VC // LANES)),
                  pl.BlockSpec((1, 1, past_len, LANES), lambda b: (b, layer, 0, 0)),
                  pl.BlockSpec((1, 1, past_len, LANES), lambda b: (b, layer, 0, 0)),
                  _layer_block((1, LANES), layer), _layer_block((1, LANES), layer),
                  _resident((t, LANES), lambda b: (0, 0)), _resident((t, LANES), lambda b: (0, 0)),
                  pl.BlockSpec(memory_space=pltpu.SMEM)],
        out_specs=pl.BlockSpec((t, qw), lambda b: (b, 0)),
        scratch_shapes=[pltpu.VMEM((C_KVHEADS, t * C_GROUP, C_HD), BF16),
                        pltpu.VMEM((t + 2 * C_BLOCK, LANES), BF16), pltpu.VMEM((t + 2 * C_BLOCK, LANES), BF16),
                        pltpu.VMEM((C_GROUP * C_BLOCK, 3 * C_BLOCK + past_len), F32)],
        compiler_params=_cparams(("arbitrary",)),
        name="attn_lat",
    )(proj, proj, proj, cache_k, cache_v, prm["q_norm2"], prm["k_norm2"], cos2, sin2, prm["sink"])


def _rope_tables(t):
    rows = t // GRID_W
    row = jnp.repeat(jnp.arange(rows, dtype=F32), GRID_W)
    col = jnp.tile(jnp.arange(GRID_W, dtype=F32), rows)
    nf = C_HD // 4
    inv = ROPE_THETA ** (-jnp.arange(nf, dtype=F32) / nf)
    ar, ac = row[:, None] * inv, col[:, None] * inv
    cos = jnp.concatenate([jnp.cos(ar), jnp.cos(ar), jnp.cos(ac), jnp.cos(ac)], axis=1)
    sin = jnp.concatenate([-jnp.sin(ar), jnp.sin(ar), -jnp.sin(ac), jnp.sin(ac)], axis=1)
    return jnp.tile(cos, (1, 2)), jnp.tile(sin, (1, 2))


def _merge_kernel(x_ref, g0_ref, g1_ref, g2_ref, oa_ref, ob_ref, oc_ref, mod_ref, wbr_ref, wout_ref, o_ref):
    merged = (_sigmoid(g0_ref[...].astype(F32)) * jnp.dot(oa_ref[...], wbr_ref[0], preferred_element_type=F32)
              + _sigmoid(g1_ref[...].astype(F32)) * jnp.dot(ob_ref[...], wbr_ref[1], preferred_element_type=F32)
              + _sigmoid(g2_ref[...].astype(F32)) * jnp.dot(oc_ref[...], wbr_ref[2], preferred_element_type=F32))
    res = jnp.dot(merged.astype(BF16), wout_ref[...], preferred_element_type=F32)
    o_ref[...] = x_ref[...] + mod_ref[0, 2:3, :] * res


def _merge_call(x2d, proj, oa, ob, oc, prm, layer, cond, tm):
    m, d = x2d.shape
    mg = lambda r: pl.BlockSpec((tm, d), lambda i: (i, COL_MG // d + r))
    br = pl.BlockSpec((tm, BRANCH_W), lambda i: (i, 0))
    return pl.pallas_call(
        _merge_kernel,
        out_shape=jax.ShapeDtypeStruct((m, d), F32),
        grid=(m // tm,),
        in_specs=[pl.BlockSpec((tm, d), lambda i: (i, 0)), mg(0), mg(1), mg(2), br, br, br,
                  _mod_block(d, layer, cond), _layer_block((3, BRANCH_W, d), layer), _layer_block((d, d), layer)],
        out_specs=pl.BlockSpec((tm, d), lambda i: (i, 0)),
        compiler_params=_cparams(("arbitrary",)),
        name="merge",
    )(x2d, proj, proj, proj, oa, ob, oc, prm["mod"], prm["w_branch"], prm["w_out"])


FF_CHUNK = 256
HALO = BF16_ROWS


def _ffn_kernel(x_ref, xp_ref, xn_ref, mod_ref, nw_ref, wup_ref, cw_ref, wd_ref, o_ref, h_scr, act_scr, *,
                tm, seq_len):
    i = pl.program_id(0)
    nseg = max(1, tm // seq_len)
    seg = tm // nseg
    nw, sh, sc = nw_ref[...], mod_ref[0, 3:4, :], mod_ref[0, 4:5, :]

    def norm(x):
        return _rms_rows(x, nw) * (1.0 + sc) + sh

    has_prev = ((i * tm) & (seq_len - 1)) != 0
    has_next = (((i + 1) * tm) & (seq_len - 1)) != 0
    zero_halo = jnp.zeros((HALO, x_ref.shape[1]), BF16)
    for s in range(nseg):
        h_scr[s, HALO:HALO + seg, :] = norm(x_ref[s * seg:(s + 1) * seg, :]).astype(BF16)
        if s == 0:
            h_scr[s, 0:HALO, :] = (norm(xp_ref[...]) * jnp.where(has_prev, 1.0, 0.0)).astype(BF16)
        else:
            h_scr[s, 0:HALO, :] = zero_halo
        if s == nseg - 1:
            h_scr[s, HALO + seg:2 * HALO + seg, :] = (norm(xn_ref[...]) * jnp.where(has_next, 1.0, 0.0)).astype(BF16)
        else:
            h_scr[s, HALO + seg:2 * HALO + seg, :] = zero_halo

    def conv(u, cw):
        return (u[HALO - 1:HALO - 1 + seg] * cw[0:1, :] + u[HALO:HALO + seg] * cw[1:2, :]
                + u[HALO + 1:HALO + 1 + seg] * cw[2:3, :])

    for lo, hi in _col_chunks(D_FF, FF_CHUNK):
        for s in range(nseg):
            h = h_scr[s]
            a = conv(jnp.dot(h, wup_ref[:, lo:hi], preferred_element_type=F32), cw_ref[:, lo:hi])
            u = conv(jnp.dot(h, wup_ref[:, D_FF + lo:D_FF + hi], preferred_element_type=F32),
                     cw_ref[:, D_FF + lo:D_FF + hi])
            act_scr[s * seg:(s + 1) * seg, lo:hi] = (_silu(a) * u).astype(BF16)

    o_ref[...] = x_ref[...] + mod_ref[0, 5:6, :] * jnp.dot(act_scr[...], wd_ref[...], preferred_element_type=F32)


def _ffn_call(x2d, prm, layer, cond, tm, seq_len):
    m, d = x2d.shape
    hb = tm // HALO
    last = m // HALO - 1
    nseg = max(1, tm // seq_len)
    seg = tm // nseg
    return pl.pallas_call(
        functools.partial(_ffn_kernel, tm=tm, seq_len=seq_len),
        out_shape=jax.ShapeDtypeStruct((m, d), F32),
        grid=(m // tm,),
        in_specs=[pl.BlockSpec((tm, d), lambda i: (i, 0)),
                  pl.BlockSpec((HALO, d), lambda i: (jnp.maximum(i * hb - 1, 0), 0)),
                  pl.BlockSpec((HALO, d), lambda i: (jnp.minimum((i + 1) * hb, last), 0)),
                  _mod_block(d, layer, cond), _layer_block((1, d), layer), _layer_block((d, 2 * D_FF), layer),
                  _layer_block((3, 2 * D_FF), layer), _layer_block((D_FF, d), layer)],
        out_specs=pl.BlockSpec((tm, d), lambda i: (i, 0)),
        scratch_shapes=[pltpu.VMEM((nseg, seg + 2 * HALO, d), BF16), pltpu.VMEM((tm, D_FF), BF16)],
        compiler_params=_cparams(("arbitrary",)),
        name="ffn",
    )(x2d, x2d, x2d, prm["mod"], prm["norm2_w"], prm["w_up"], prm["conv_ffn"], prm["w_down"])


def _permute_w_in(w):
    s = _SRC
    w16 = jnp.concatenate([w[..., s["mg"]:s["end"]], w[..., s["qa"]:s["beta"]], w[..., s["qb"]:s["fb"]],
                           w[..., s["gb"]:s["mg"]]], axis=-1)
    n_gate = s["qb"] - s["beta"]
    gates = jnp.pad(w[..., s["beta"]:s["qb"]], ((0, 0), (0, 0), (0, LANES - n_gate)))
    w32 = jnp.concatenate([w[..., s["fb"]:s["gb"]], gates], axis=-1)
    return w16.astype(BF16), w32.astype(BF16)


def _gate_rows(proj32, nseq, t, c):
    g = proj32[:, COL32_GATES:COL32_GATES + 4 * HEADS].reshape(nseq, t // c, c, 2, 2, HEADS)
    return jnp.transpose(g, (0, 5, 1, 3, 4, 2)).reshape(nseq, HEADS, t // c, 4, c)


def _row_tile(rows, t, tm):
    while rows % tm or (t % tm and tm % t):
        tm //= 2
    return tm


def _group_forward(x3d, first_cond, shared_cond, prm, past, tables):
    nseq, t, d = x3d.shape
    x = x3d.reshape(nseq * t, d)
    def cond_of(tile):
        tiles_per_seq = max(1, t // tile)
        return (lambda i: first_cond) if shared_cond else (lambda i: first_cond + i // tiles_per_seq)

    tm = _row_tile(nseq * t, t, 512)
    tm_merge = _row_tile(nseq * t, t, 1024 if shared_cond else min(1024, t))
    cond = cond_of(tm)
    emit = past is None
    states_a, states_b, keys, vals = [], [], [], []
    for l in range(prm["w16"].shape[0]):
        proj, proj32 = _in_proj_call(x, prm, l, cond, tm)
        gates_r = _gate_rows(proj32, nseq, t, min(CHUNK_A, t))
        res_ab = _mixer_ab_call(proj, proj32, gates_r, prm, past, l, nseq, t, emit)
        if emit:
            oc, kn, vn = _attn_ctx_call(proj, prm, l, nseq, t)
            states_a.append(res_ab[2])
            states_b.append(res_ab[3])
            keys.append(kn.reshape(nseq, t, C_KVHEADS, C_HD))
            vals.append(vn.reshape(nseq, t, C_KVHEADS, C_HD))
        else:
            oc = _attn_lat_call(proj, past[2], past[3], prm, tables[0], tables[1], l, nseq, t)
        x = _merge_call(x, proj, res_ab[0], res_ab[1], oc, prm, l, cond_of(tm_merge), tm_merge)
        x = _ffn_call(x, prm, l, cond, tm, t)
    return x.reshape(nseq, t, d), states_a, states_b, keys, vals


def kernel(x_prompt, x_sample, state_delta, state_hgrn, cache_k, cache_v, c, c_ctx, ada_w, ada_b, norm1_w, w_in, conv_a, a_log, dt_bias, norm_a, lb_logits, norm_b, q_norm, k_norm, sink, w_branch, w_out, norm2_w, w_up, conv_ffn, w_down):
    depth = w_in.shape[0]
    d = x_prompt.shape[-1]

    cond = jnp.concatenate([c_ctx[None, :], c], axis=0)
    rows = -(-cond.shape[0] // SUBLANES) * SUBLANES
    cond = jnp.pad(cond, ((0, rows - cond.shape[0]), (0, 0)))
    mod_all = _mod_call(cond, ada_w, ada_b).reshape(depth, rows, 6, d)

    w16, w32 = _permute_w_in(w_in)
    prm = dict(
        mod=mod_all, w16=w16, w32=w32,
        norm1_w=norm1_w.reshape(depth, 1, d), norm2_w=norm2_w.reshape(depth, 1, d),
        conv_a=conv_a, a_log=a_log, dt_bias=dt_bias, norm_a=norm_a.reshape(depth, 1, HEAD_W),
        lb_logits=lb_logits, norm_b=norm_b.reshape(depth, 1, HEAD_W),
        q_norm2=jnp.tile(q_norm, (1, 2)).reshape(depth, 1, LANES), k_norm2=jnp.tile(k_norm, (1, 2)).reshape(depth, 1, LANES),
        sink=sink, w_branch=w_branch.astype(BF16), w_out=w_out.astype(BF16),
        w_up=w_up.astype(BF16), conv_ffn=conv_ffn, w_down=w_down.astype(BF16))

    y_prompt, st_a, st_b, keys, vals = _group_forward(x_prompt, 0, True, prm, None, None)

    past_len = cache_k.shape[2]
    past = (state_delta, state_hgrn,
            cache_k.reshape(cache_k.shape[0], depth, past_len, C_KVHEADS * C_HD),
            cache_v.reshape(cache_v.shape[0], depth, past_len, C_KVHEADS * C_HD))
    y_sample, _, _, _, _ = _group_forward(x_sample, 1, False, prm, past, _rope_tables(x_sample.shape[1]))

    return (y_prompt, y_sample, jnp.stack(st_a, axis=1), jnp.stack(st_b, axis=1),
            jnp.stack(keys, axis=1), jnp.stack(vals, axis=1))
```
